```python
import jax, jax.numpy as jnp
from jax import lax
import numpy as np

D_MODEL = 1024
BATCH = 8
SEQ = 8192
DEPTH = 2

NORM_EPS = 1e-6
HEAD_DIM = 64
MIX_WIDTH = D_MODEL
GROUP_WIDTH = MIX_WIDTH // 4
A_HEADS = GROUP_WIDTH // HEAD_DIM
A_WIDTH = A_HEADS * HEAD_DIM
MOBA_BLOCK = 256
MOBA_TOPK = 3
MOBA_QCHUNK = 64
POOL_WINDOWS = (2, 4, 8, 16)
POOL_GROUPS = 4
B_WIDTH = GROUP_WIDTH
POOL_GROUP_DIM = B_WIDTH // POOL_GROUPS
C_HEADS = GROUP_WIDTH // HEAD_DIM
C_KV_HEADS = C_HEADS // 2
C_WIDTH = C_HEADS * HEAD_DIM
C_KV_WIDTH = C_KV_HEADS * HEAD_DIM
SWA_WINDOW = 128
SWA_BLOCK = 128
D_WIDTH = GROUP_WIDTH
CONV_WIDTH = 3
PROJ_SIZES = (A_WIDTH, A_WIDTH, A_WIDTH, A_WIDTH,
              B_WIDTH, B_WIDTH,
              C_WIDTH, C_KV_WIDTH, C_KV_WIDTH, C_WIDTH,
              D_WIDTH, D_WIDTH, D_WIDTH, D_WIDTH)
IN_PROJ_WIDTH = 4 * A_WIDTH + 2 * B_WIDTH + 2 * C_WIDTH + 2 * C_KV_WIDTH + 4 * D_WIDTH

kernel_name = "hymba_style_moba_pool_swa_conv_hybrid"


def rms_norm(x, gain):
    xf = x.astype(jnp.float32)
    y = xf * lax.rsqrt(jnp.mean(xf * xf, axis=-1, keepdims=True) + NORM_EPS)
    return (y * gain.astype(jnp.float32)).astype(x.dtype)


def alibi_slopes(n):
    return jnp.exp2(-(8.0 / n) * jnp.arange(1, n + 1, dtype=jnp.float32))


def moba_attention(q, k, v, slopes):
    B, H, S, Dh = q.shape
    nb = max(-(-S // MOBA_BLOCK), MOBA_TOPK)
    pad = nb * MOBA_BLOCK - S
    kb = jnp.pad(k, ((0, 0), (0, 0), (0, pad), (0, 0))).reshape(B, H, nb, MOBA_BLOCK, Dh)
    vb = jnp.pad(v, ((0, 0), (0, 0), (0, pad), (0, 0))).reshape(B, H, nb, MOBA_BLOCK, Dh)
    kmean = jnp.mean(kb.astype(jnp.float32), axis=3)
    nq = S // MOBA_QCHUNK
    qc = q.reshape(B, H, nq, MOBA_QCHUNK, Dh).transpose(2, 0, 1, 3, 4)
    scale = Dh ** -0.5
    bidx = jnp.arange(B)[:, None, None, None]
    hidx = jnp.arange(H)[None, :, None, None]
    sl = slopes.astype(jnp.float32)

    def one_chunk(args):
        qi, c = args
        t0 = c * MOBA_QCHUNK
        own = t0 // MOBA_BLOCK
        tq = t0 + jnp.arange(MOBA_QCHUNK)
        qf = qi.astype(jnp.float32)
        bscore = jnp.einsum('bhqd,bhnd->bhqn', qf, kmean)
        bscore = jnp.where(jnp.arange(nb) < own, bscore, -jnp.inf)
        _, idx = lax.top_k(bscore, MOBA_TOPK)
        sel_valid = jnp.arange(MOBA_TOPK) < own
        ksel = kb[bidx, hidx, idx].astype(jnp.float32)
        vsel = vb[bidx, hidx, idx].astype(jnp.float32)
        kpos = idx[..., None] * MOBA_BLOCK + jnp.arange(MOBA_BLOCK)
        dsel = (tq[None, None, :, None, None] - kpos).astype(jnp.float32)
        s_sel = jnp.einsum('bhqd,bhqjkd->bhqjk', qf, ksel) * scale - sl[None, :, None, None, None] * dsel
        s_sel = jnp.where(sel_valid[:, None], s_sel, -jnp.inf)
        s_sel = s_sel.reshape(B, H, MOBA_QCHUNK, MOBA_TOPK * MOBA_BLOCK)
        kown = lax.dynamic_index_in_dim(kb, own, axis=2, keepdims=False).astype(jnp.float32)
        vown = lax.dynamic_index_in_dim(vb, own, axis=2, keepdims=False).astype(jnp.float32)
        dist = tq[:, None] - (own * MOBA_BLOCK + jnp.arange(MOBA_BLOCK))[None, :]
        s_own = jnp.einsum('bhqd,bhkd->bhqk', qf, kown) * scale - sl[None, :, None, None] * dist.astype(jnp.float32)
        s_own = jnp.where(dist >= 0, s_own, -jnp.inf)
        p = jax.nn.softmax(jnp.concatenate([s_sel, s_own], axis=-1), axis=-1)
        p_sel = p[..., :MOBA_TOPK * MOBA_BLOCK].reshape(B, H, MOBA_QCHUNK, MOBA_TOPK, MOBA_BLOCK)
        p_own = p[..., MOBA_TOPK * MOBA_BLOCK:]
        out = (jnp.einsum('bhqjk,bhqjkd->bhqd', p_sel, vsel)
               + jnp.einsum('bhqk,bhkd->bhqd', p_own, vown))
        return out.astype(q.dtype)

    out = lax.map(one_chunk, (qc, jnp.arange(nq)))
    return out.transpose(1, 2, 0, 3, 4).reshape(B, H, S, Dh)


def swa_attention(q, k, v, sinks, slopes):
    B, Hq, S, Dh = q.shape
    Hkv = k.shape[1]
    G = Hq // Hkv
    W = SWA_BLOCK
    nblk = S // W
    qb = q.reshape(B, Hkv, G, nblk, W, Dh).astype(jnp.float32)
    kb = k.reshape(B, Hkv, nblk, W, Dh).astype(jnp.float32)
    vb = v.reshape(B, Hkv, nblk, W, Dh).astype(jnp.float32)
    kk = jnp.concatenate([jnp.pad(kb, ((0, 0), (0, 0), (1, 0), (0, 0), (0, 0)))[:, :, :-1], kb], axis=3)
    vv = jnp.concatenate([jnp.pad(vb, ((0, 0), (0, 0), (1, 0), (0, 0), (0, 0)))[:, :, :-1], vb], axis=3)
    s = jnp.einsum('bkgnqd,bkncd->bkgnqc', qb, kk) * (Dh ** -0.5)
    kpos = jnp.arange(2 * W) - W
    dist = jnp.arange(W)[:, None] - kpos[None, :]
    mask = ((dist >= 0) & (dist < SWA_WINDOW))[None] & ((jnp.arange(nblk)[:, None, None] > 0) | (kpos[None, None, :] >= 0))
    sl = slopes.astype(jnp.float32).reshape(Hkv, G)
    s = s - sl[None, :, :, None, None, None] * dist.astype(jnp.float32)
    s = jnp.where(mask, s, -jnp.inf)
    sink = jnp.broadcast_to(sinks.astype(jnp.float32).reshape(Hkv, G)[None, :, :, None, None, None], s.shape[:-1] + (1,))
    p = jax.nn.softmax(jnp.concatenate([s, sink], axis=-1), axis=-1)[..., :2 * W]
    o = jnp.einsum('bkgnqc,bkncd->bkgnqd', p, vv)
    return o.reshape(B, Hq, S, Dh).astype(q.dtype)


def multiscale_pool(u):
    S = u.shape[1]
    uf = u.astype(jnp.float32)
    cs = jnp.cumsum(uf, axis=1)
    pos = jnp.arange(S)
    outs = []
    for gi, w in enumerate(POOL_WINDOWS):
        c = cs[..., gi * POOL_GROUP_DIM:(gi + 1) * POOL_GROUP_DIM]
        shifted = jnp.pad(c, ((0, 0), (w, 0), (0, 0)))[:, :S]
        cnt = jnp.minimum(pos + 1, w).astype(jnp.float32)[None, :, None]
        outs.append((c - shifted) / cnt)
    return (jnp.concatenate(outs, axis=-1) - uf).astype(u.dtype)


def short_conv(u, w):
    C = u.shape[-1]
    return lax.conv_general_dilated(u, w[:, None, :].astype(u.dtype), window_strides=(1,),
                                    padding=((CONV_WIDTH - 1, 0),),
                                    dimension_numbers=('NWC', 'WIO', 'NWC'),
                                    feature_group_count=C)


def hybrid_layer(x, norm_g, w_in, w_out, a_qn, a_kn, pool_w, pool_scale, c_qn, c_kn, c_sinks, conv_w, slopes_a, slopes_c):
    B, S, _ = x.shape
    h = rms_norm(x, norm_g)
    proj = h @ w_in.astype(h.dtype)
    split_points = [int(p) for p in np.cumsum(PROJ_SIZES)[:-1]]
    aq, ak, av, ag, bu, bg, cq, ck, cv, cg, dh, db, dc, dg = jnp.split(proj, split_points, axis=-1)

    def to_heads(t):
        return t.reshape(B, S, -1, HEAD_DIM).transpose(0, 2, 1, 3)

    def from_heads(t):
        return t.transpose(0, 2, 1, 3).reshape(B, S, -1)

    ya = moba_attention(rms_norm(to_heads(aq), a_qn), rms_norm(to_heads(ak), a_kn), to_heads(av), slopes_a)
    ya = from_heads(ya) * jax.nn.silu(ag)
    pooled = multiscale_pool(bu).reshape(B, S, POOL_GROUPS, POOL_GROUP_DIM)
    yb = jnp.einsum('bsgc,gcd->bsgd', pooled, pool_w.astype(pooled.dtype)).reshape(B, S, B_WIDTH)
    yb = yb * pool_scale * jax.nn.silu(bg)
    yc = swa_attention(rms_norm(to_heads(cq), c_qn), rms_norm(to_heads(ck), c_kn), to_heads(cv), c_sinks, slopes_c)
    yc = from_heads(yc) * jax.nn.silu(cg)
    yd = db * short_conv(dc * dh, conv_w) * jax.nn.silu(dg)

    y = jnp.concatenate([ya, yb, yc, yd], axis=-1) @ w_out.astype(x.dtype)
    return x + y


def setup_inputs(seed: int = 0) -> dict:
    key = jax.random.key(seed)
    ks = jax.random.split(key, 14)
    f32 = jnp.float32
    nrm = lambda k, s: jax.random.normal(k, s, dtype=f32)
    return {
        "x": nrm(ks[0], (BATCH, SEQ, D_MODEL)),
        "norm_g": 1.0 + 0.02 * nrm(ks[1], (DEPTH, D_MODEL)),
        "w_in": nrm(ks[2], (DEPTH, D_MODEL, IN_PROJ_WIDTH)) * D_MODEL ** -0.5,
        "w_out": nrm(ks[3], (DEPTH, MIX_WIDTH, D_MODEL)) * MIX_WIDTH ** -0.5,
        "a_q_norm": 1.0 + 0.02 * nrm(ks[4], (DEPTH, HEAD_DIM)),
        "a_k_norm": 1.0 + 0.02 * nrm(ks[5], (DEPTH, HEAD_DIM)),
        "pool_w": nrm(ks[6], (DEPTH, POOL_GROUPS, POOL_GROUP_DIM, POOL_GROUP_DIM)) * POOL_GROUP_DIM ** -0.5,
        "pool_scale": 1.0 + 0.1 * nrm(ks[7], (DEPTH, B_WIDTH)),
        "c_q_norm": 1.0 + 0.02 * nrm(ks[8], (DEPTH, HEAD_DIM)),
        "c_k_norm": 1.0 + 0.02 * nrm(ks[9], (DEPTH, HEAD_DIM)),
        "c_sinks": nrm(ks[10], (DEPTH, C_HEADS)),
        "conv_w": nrm(ks[11], (DEPTH, CONV_WIDTH, D_WIDTH)) * CONV_WIDTH ** -0.5,
    }


def reference(x, norm_g, w_in, w_out, a_q_norm, a_k_norm, pool_w, pool_scale, c_q_norm, c_k_norm, c_sinks, conv_w):
    slopes = alibi_slopes(A_HEADS + C_HEADS)
    slopes_c = slopes[:C_HEADS]
    slopes_a = slopes[C_HEADS:]
    for l in range(DEPTH):
        x = hybrid_layer(x, norm_g[l], w_in[l], w_out[l], a_q_norm[l], a_k_norm[l],
                         pool_w[l], pool_scale[l], c_q_norm[l], c_k_norm[l], c_sinks[l],
                         conv_w[l], slopes_a, slopes_c)
    return x
```

```python
import functools
import math

import jax
import jax.numpy as jnp
import numpy as np
from jax import lax
from jax.experimental import pallas as pl
from jax.experimental.pallas import tpu as pltpu

F32 = jnp.float32
BF16 = jnp.bfloat16

D_MODEL = 1024
HEAD_DIM = 64
GROUP_WIDTH = 256
N_HEADS = 4
C_KV_HEADS = 2
NORM_EPS = 1e-6
MOBA_BLOCK = 256
MOBA_TOPK = 3
SWA_BLOCK = 128
POOL_WINDOWS = (2, 4, 8, 16)
CONV_WIDTH = 3
HALO = 16

AUG = 128
COL_ONEHOT = 64
COL_POS = 96
NEG_BIG = -(2.0 ** 100)
LOG2E = math.log2(math.e)
QK_SCALE = HEAD_DIM ** -0.5

ROW_TILE = 512
Q_TILE = 256
VMEM_LIMIT = 56 * 1024 * 1024

SEG = dict(aq=0, ak=256, av=512, ag=768, bu=1024, bg=1280, cq=1536, ck=1792, cv=1920,
           cg=2048, dh=2304, db=2560, dc=2816, dg=3072)
IN_PROJ_WIDTH = 3328
REST_WIDTH = 2048


def _dot(a, b):
    return jnp.dot(a, b, preferred_element_type=F32)


def _in_proj_kernel(x_ref, g_ref, w_ref, bd_ref, gq_ref, gk_ref, gcq_ref, gck_ref, kca_ref,
                    qT_ref, kaug_ref, vT_ref, cqT_ref, ck_ref, cvT_ref, rest_ref):
    tm = x_ref.shape[1]
    x = x_ref[0]
    ms = jnp.mean(x * x, axis=-1, keepdims=True)
    h = (x * lax.rsqrt(ms + NORM_EPS) * g_ref[...]).astype(BF16)

    def proj(lo, width):
        return _dot(h, w_ref[:, lo:lo + width])

    def norm_t(p, gain_ref):
        pt = p.T
        outs = []
        for hh in range(N_HEADS):
            ph = pt[hh * HEAD_DIM:(hh + 1) * HEAD_DIM]
            ss = jnp.sum(ph * ph, axis=0, keepdims=True) * (1.0 / HEAD_DIM)
            outs.append(ph * lax.rsqrt(ss + NORM_EPS))
        return (jnp.concatenate(outs, axis=0) * gain_ref[...]).astype(BF16)

    def norm_rows(p, gain_ref, width):
        sq = (p * p)
        hi = sq.astype(BF16)
        lo = (sq - hi.astype(F32)).astype(BF16)
        bd = bd_ref[0:width, 0:width]
        ss = (_dot(hi, bd) + _dot(lo, bd)) * (1.0 / HEAD_DIM)
        return p * lax.rsqrt(ss + NORM_EPS) * gain_ref[...]

    def build_kaug(kn, kconst, n_heads, out_ref):
        lane = lax.broadcasted_iota(jnp.int32, (tm, AUG), 1)
        for hh in range(n_heads):
            col = kn[:, (hh // 2) * AUG:(hh // 2 + 1) * AUG]
            if hh % 2 == 1:
                col = pltpu.roll(col, HEAD_DIM, axis=1)
            out_ref[0, hh] = jnp.where(lane < HEAD_DIM, col, kconst).astype(BF16)

    nblk = tm // Q_TILE
    qT = norm_t(proj(SEG['aq'], 256), gq_ref)
    for t in range(nblk):
        qT_ref[0, t] = qT[:, t * Q_TILE:(t + 1) * Q_TILE]
    kn = norm_rows(proj(SEG['ak'], 256), gk_ref, 256)
    build_kaug(kn, kca_ref[...].astype(F32), N_HEADS, kaug_ref)
    vT = proj(SEG['av'], 256).T.astype(BF16)
    for t in range(nblk):
        vT_ref[0, t] = vT[:, t * Q_TILE:(t + 1) * Q_TILE]
    cqT = norm_t(proj(SEG['cq'], 256), gcq_ref)
    for t in range(nblk):
        cqT_ref[0, t] = cqT[:, t * Q_TILE:(t + 1) * Q_TILE]
    ck_ref[0] = norm_rows(proj(SEG['ck'], 128), gck_ref, 128).astype(BF16)
    cvT = proj(SEG['cv'], 128).T.astype(BF16)
    for t in range(tm // SWA_BLOCK):
        cvT_ref[0, t] = cvT[:, t * SWA_BLOCK:(t + 1) * SWA_BLOCK]
    rest_ref[0, :, 0:768] = proj(SEG['ag'], 768).astype(BF16)
    rest_ref[0, :, 768:1024] = proj(SEG['cg'], 256).astype(BF16)
    rest_ref[0, :, 1024:2048] = proj(SEG['dh'], 1024).astype(BF16)


def _in_proj(x, g, w, bd, gq, gk, gcq, gck, kca):
    B, S, _ = x.shape
    tm = ROW_TILE
    ns = S // tm
    nq = S // Q_TILE
    nkb = S // SWA_BLOCK
    const = lambda *shape: pl.BlockSpec(shape, lambda b, s: (0,) * len(shape))
    out_shape = (
        jax.ShapeDtypeStruct((B, nq, 256, Q_TILE), BF16),
        jax.ShapeDtypeStruct((B, N_HEADS, S, AUG), BF16),
        jax.ShapeDtypeStruct((B, nq, 256, Q_TILE), BF16),
        jax.ShapeDtypeStruct((B, nq, 256, Q_TILE), BF16),
        jax.ShapeDtypeStruct((B, S, 128), BF16),
        jax.ShapeDtypeStruct((B, nkb, 128, SWA_BLOCK), BF16),
        jax.ShapeDtypeStruct((B, S, REST_WIDTH), BF16),
    )
    r = tm // Q_TILE
    out_specs = (
        pl.BlockSpec((1, r, 256, Q_TILE), lambda b, s: (b, s, 0, 0)),
        pl.BlockSpec((1, N_HEADS, tm, AUG), lambda b, s: (b, 0, s, 0)),
        pl.BlockSpec((1, r, 256, Q_TILE), lambda b, s: (b, s, 0, 0)),
        pl.BlockSpec((1, r, 256, Q_TILE), lambda b, s: (b, s, 0, 0)),
        pl.BlockSpec((1, tm, 128), lambda b, s: (b, s, 0)),
        pl.BlockSpec((1, tm // SWA_BLOCK, 128, SWA_BLOCK), lambda b, s: (b, s, 0, 0)),
        pl.BlockSpec((1, tm, REST_WIDTH), lambda b, s: (b, s, 0)),
    )
    in_specs = [
        pl.BlockSpec((1, tm, D_MODEL), lambda b, s: (b, s, 0)),
        const(1, D_MODEL),
        const(D_MODEL, IN_PROJ_WIDTH),
        const(256, 256),
        const(256, tm), const(1, 256), const(256, tm), const(1, 128),
        pl.BlockSpec((tm, AUG), lambda b, s: (s, 0)),
    ]
    return pl.pallas_call(
        _in_proj_kernel, grid=(B, ns), in_specs=in_specs, out_specs=out_specs, out_shape=out_shape,
        compiler_params=pltpu.CompilerParams(dimension_semantics=("parallel", "parallel"),
                                             vmem_limit_bytes=VMEM_LIMIT),
        name="in_proj",
    )(x, g, w, bd, gq, gk, gcq, gck, kca)


def _moba_kernel(qT_ref, kaug_ref, vT_ref, avg_ref, crow_ref, causal_ref, out_ref, kmean_ref):
    i = pl.program_id(1)
    S = kaug_ref.shape[2]
    nb = S // MOBA_BLOCK
    chunk = 1024

    @pl.when(i == 0)
    def _():
        for hh in range(N_HEADS):
            acc = jnp.zeros((nb, AUG), F32)
            for c in range(S // chunk):
                acc = acc + _dot(avg_ref[:, c * chunk:(c + 1) * chunk],
                                 kaug_ref[0, hh, c * chunk:(c + 1) * chunk, :])
            kmean_ref[hh] = acc

    row = lax.broadcasted_iota(jnp.int32, (nb, Q_TILE), 0).astype(F32)
    i_f = i.astype(F32)
    neg_inf = jnp.float32(-jnp.inf)
    outs = []
    for hh in range(N_HEADS):
        qh = qT_ref[0, 0, hh * HEAD_DIM:(hh + 1) * HEAD_DIM, :]
        km = kmean_ref[hh][:, 0:HEAD_DIM]
        km_hi = km.astype(BF16)
        km_lo = (km - km_hi.astype(F32)).astype(BF16)
        bs = _dot(km_hi, qh) + _dot(km_lo, qh)
        xs = jnp.where(row < i_f, bs, neg_inf)
        sel = row == i_f
        for _ in range(MOBA_TOPK):
            mx = jnp.max(xs, axis=0, keepdims=True)
            first = jnp.min(jnp.where(xs == mx, row, jnp.float32(nb)), axis=0, keepdims=True)
            pick = row == first
            sel = sel | (pick & (mx > neg_inf))
            xs = jnp.where(pick, neg_inf, xs)
        negmask = jnp.where(sel, 0.0, NEG_BIG).astype(BF16)
        pad = jnp.zeros((COL_POS - COL_ONEHOT - nb, Q_TILE), BF16) if nb < 32 else None
        parts = [qh, negmask] + ([pad] if pad is not None else []) + [crow_ref[hh]]
        qaug = jnp.concatenate(parts, axis=0)
        vh = lambda j: vT_ref[0, j, hh * HEAD_DIM:(hh + 1) * HEAD_DIM, :]
        kb = lambda j: kaug_ref[0, hh, pl.ds(pl.multiple_of(j * MOBA_BLOCK, MOBA_BLOCK), MOBA_BLOCK), :]

        s = _dot(kb(i), qaug) + causal_ref[...]
        m = jnp.max(s, axis=0, keepdims=True)
        p = jnp.exp2(s - m)
        l = jnp.sum(p, axis=0, keepdims=True)
        acc = _dot(vh(i), p.astype(BF16))

        def body(j, carry):
            m, l, acc = carry
            s = _dot(kb(j), qaug)
            m_new = jnp.maximum(m, jnp.max(s, axis=0, keepdims=True))
            alpha = jnp.exp2(m - m_new)
            p = jnp.exp2(s - m_new)
            l = alpha * l + jnp.sum(p, axis=0, keepdims=True)
            acc = alpha * acc + _dot(vh(j), p.astype(BF16))
            return m_new, l, acc

        m, l, acc = lax.fori_loop(0, i, body, (m, l, acc))
        outs.append(acc / l)
    out_ref[0] = jnp.concatenate(outs, axis=0).T.astype(BF16)


def _moba(qT, kaug, vT, avg, crow, causal):
    B, nq = qT.shape[0], qT.shape[1]
    S = kaug.shape[2]
    nb = S // MOBA_BLOCK
    in_specs = [
        pl.BlockSpec((1, 1, 256, Q_TILE), lambda b, i: (b, i, 0, 0)),
        pl.BlockSpec((1, N_HEADS, S, AUG), lambda b, i: (b, 0, 0, 0)),
        pl.BlockSpec((1, nq, 256, Q_TILE), lambda b, i: (b, 0, 0, 0)),
        pl.BlockSpec((nb, S), lambda b, i: (0, 0)),
        pl.BlockSpec((N_HEADS, 32, Q_TILE), lambda b, i: (0, 0, 0)),
        pl.BlockSpec((MOBA_BLOCK, Q_TILE), lambda b, i: (0, 0)),
    ]
    return pl.pallas_call(
        _moba_kernel, grid=(B, nq), in_specs=in_specs,
        out_specs=pl.BlockSpec((1, Q_TILE, 256), lambda b, i: (b, i, 0)),
        out_shape=jax.ShapeDtypeStruct((B, S, 256), BF16),
        scratch_shapes=[pltpu.VMEM((N_HEADS, nb, AUG), F32)],
        compiler_params=pltpu.CompilerParams(dimension_semantics=("parallel", "arbitrary"),
                                             vmem_limit_bytes=VMEM_LIMIT),
        name="moba",
    )(qT, kaug, vT, avg, crow, causal)


def _swa_kernel(cqT_ref, ck_ref, cvT_ref, band_ref, sink_ref, out_ref):
    i = pl.program_id(1)
    halves = Q_TILE // SWA_BLOCK
    zrows = jnp.zeros((HEAD_DIM, 2 * SWA_BLOCK), BF16)
    outs = {}
    for t in range(halves):
        n = i * halves + t
        prev = jnp.maximum(n - 1, 0)
        kwin = jnp.concatenate([
            ck_ref[0, pl.ds(pl.multiple_of(prev * SWA_BLOCK, SWA_BLOCK), SWA_BLOCK), :],
            ck_ref[0, pl.ds(pl.multiple_of(n * SWA_BLOCK, SWA_BLOCK), SWA_BLOCK), :]], axis=0)
        for kv in range(C_KV_HEADS):
            vwin = jnp.concatenate([cvT_ref[0, prev, kv * HEAD_DIM:(kv + 1) * HEAD_DIM, :],
                                    cvT_ref[0, n, kv * HEAD_DIM:(kv + 1) * HEAD_DIM, :]], axis=1)
            q2 = jnp.concatenate([
                cqT_ref[0, 0, (2 * kv + g) * HEAD_DIM:(2 * kv + g + 1) * HEAD_DIM,
                        t * SWA_BLOCK:(t + 1) * SWA_BLOCK] for g in range(2)], axis=1)
            qz = jnp.concatenate([q2, zrows] if kv == 0 else [zrows, q2], axis=0)
            band = jnp.where(n == 0, band_ref[kv, 0], band_ref[kv, 1])
            s = _dot(kwin, qz) + band
            m = jnp.max(s, axis=0, keepdims=True)
            p = jnp.exp2(s - m)
            l = jnp.sum(p, axis=0, keepdims=True) + jnp.exp2(sink_ref[kv] - m)
            outs[kv, t] = _dot(vwin, p.astype(BF16)) / l
    rows = []
    for kv in range(C_KV_HEADS):
        for g in range(2):
            rows.append(jnp.concatenate(
                [outs[kv, t][:, g * SWA_BLOCK:(g + 1) * SWA_BLOCK] for t in range(halves)], axis=1))
    out_ref[0] = jnp.concatenate(rows, axis=0).T.astype(BF16)


def _swa(cqT, ck, cvT, band, sink):
    B, nq = cqT.shape[0], cqT.shape[1]
    S = ck.shape[1]
    nkb = S // SWA_BLOCK
    in_specs = [
        pl.BlockSpec((1, 1, 256, Q_TILE), lambda b, i: (b, i, 0, 0)),
        pl.BlockSpec((1, S, 128), lambda b, i: (b, 0, 0)),
        pl.BlockSpec((1, nkb, 128, SWA_BLOCK), lambda b, i: (b, 0, 0, 0)),
        pl.BlockSpec((C_KV_HEADS, 2, 2 * SWA_BLOCK, 2 * SWA_BLOCK), lambda b, i: (0, 0, 0, 0)),
        pl.BlockSpec((C_KV_HEADS, 1, 2 * SWA_BLOCK), lambda b, i: (0, 0, 0)),
    ]
    return pl.pallas_call(
        _swa_kernel, grid=(B, nq), in_specs=in_specs,
        out_specs=pl.BlockSpec((1, Q_TILE, 256), lambda b, i: (b, i, 0)),
        out_shape=jax.ShapeDtypeStruct((B, S, 256), BF16),
        compiler_params=pltpu.CompilerParams(dimension_semantics=("parallel", "parallel"),
                                             vmem_limit_bytes=VMEM_LIMIT),
        name="swa",
    )(cqT, ck, cvT, band, sink)


def _silu(v):
    return v * (1.0 / (1.0 + jnp.exp(-v)))


def _out_proj_kernel(x_ref, ya_ref, yc_ref, rest_ref, halo_ref, invcnt_ref, pw_ref, pscale_ref,
                     cw_ref, wo_ref, out_ref, ubuf, cbuf):
    s = pl.program_id(1)
    tm = x_ref.shape[1]
    seg = lambda k: rest_ref[0, :, k * 256:(k + 1) * 256].astype(F32)
    ag, bu, bg, cg, dh, db, dc, dg = (seg(k) for k in range(8))
    live = (s > 0).astype(F32)
    hseg = lambda k: halo_ref[0, :, k * 256:(k + 1) * 256].astype(F32) * live

    ubuf[0:HALO, :] = hseg(1)
    ubuf[HALO:HALO + tm, :] = bu
    lane = lax.broadcasted_iota(jnp.int32, (tm, 128), 1)
    sh = lambda buf, k, c: buf[HALO - k:HALO - k + tm, c * 128:(c + 1) * 128]
    a2 = sh(ubuf, 0, 0) + sh(ubuf, 1, 0)
    a4 = a2 + sh(ubuf, 2, 0) + sh(ubuf, 3, 0)
    left = jnp.where(lane < 64, a2, a4)
    a8 = sh(ubuf, 0, 1)
    for k in range(1, 8):
        a8 = a8 + sh(ubuf, k, 1)
    a16 = a8
    for k in range(8, 16):
        a16 = a16 + sh(ubuf, k, 1)
    right = jnp.where(lane < 64, a8, a16)
    pooled = jnp.concatenate([left, right], axis=1) * invcnt_ref[0] - bu
    yb = _dot(pooled.astype(BF16), pw_ref[...]) * pscale_ref[...] * _silu(bg)

    cbuf[0:HALO, :] = hseg(6) * hseg(4)
    cbuf[HALO:HALO + tm, :] = dc * dh
    conv = (cw_ref[0:1, :] * cbuf[HALO - 2:HALO - 2 + tm, :]
            + cw_ref[1:2, :] * cbuf[HALO - 1:HALO - 1 + tm, :]
            + cw_ref[2:3, :] * cbuf[HALO:HALO + tm, :])
    yd = db * conv * _silu(dg)

    ya = ya_ref[0].astype(F32) * _silu(ag)
    yc = yc_ref[0].astype(F32) * _silu(cg)
    y = (_dot(ya.astype(BF16), wo_ref[0:256, :]) + _dot(yb.astype(BF16), wo_ref[256:512, :])
         + _dot(yc.astype(BF16), wo_ref[512:768, :]) + _dot(yd.astype(BF16), wo_ref[768:1024, :]))
    out_ref[0] = x_ref[0] + y


def _out_proj(x, ya, yc, rest, invcnt, pw, pscale, cw, wo):
    B, S, _ = x.shape
    tm = ROW_TILE
    ns = S // tm
    hb = tm // HALO
    const = lambda *shape: pl.BlockSpec(shape, lambda b, s: (0,) * len(shape))
    in_specs = [
        pl.BlockSpec((1, tm, D_MODEL), lambda b, s: (b, s, 0)),
        pl.BlockSpec((1, tm, 256), lambda b, s: (b, s, 0)),
        pl.BlockSpec((1, tm, 256), lambda b, s: (b, s, 0)),
        pl.BlockSpec((1, tm, REST_WIDTH), lambda b, s: (b, s, 0)),
        pl.BlockSpec((1, HALO, REST_WIDTH), lambda b, s: (b, jnp.maximum(s * hb - 1, 0), 0)),
        pl.BlockSpec((1, tm, 256), lambda b, s: (jnp.minimum(s, 1), 0, 0)),
        const(256, 256), const(1, 256), const(CONV_WIDTH, 256), const(D_MODEL, D_MODEL),
    ]
    return pl.pallas_call(
        _out_proj_kernel, grid=(B, ns), in_specs=in_specs,
        out_specs=pl.BlockSpec((1, tm, D_MODEL), lambda b, s: (b, s, 0)),
        out_shape=jax.ShapeDtypeStruct((B, S, D_MODEL), F32),
        scratch_shapes=[pltpu.VMEM((HALO + tm, 256), F32), pltpu.VMEM((HALO + tm, 256), F32)],
        compiler_params=pltpu.CompilerParams(dimension_semantics=("parallel", "parallel"),
                                             vmem_limit_bytes=VMEM_LIMIT),
        name="out_proj",
    )(x, ya, yc, rest, rest, invcnt, pw, pscale, cw, wo)


def _tables(S):
    tm = ROW_TILE
    slopes = np.exp2(-(8.0 / (2 * N_HEADS)) * np.arange(1, 2 * N_HEADS + 1)).astype(np.float32)
    slopes_c, slopes_a = slopes[:N_HEADS], slopes[N_HEADS:]
    pos = np.arange(S)

    kca = np.zeros((S, AUG), np.float32)
    blk, r = pos // MOBA_BLOCK, pos % MOBA_BLOCK
    kca[pos, COL_ONEHOT + blk] = 1.0
    kca[:, COL_POS] = r
    kca[:, COL_POS + 1] = r
    kca[:, COL_POS + 2] = blk
    kca[:, COL_POS + 3] = blk
    kca = jnp.asarray(kca, BF16)

    c = slopes_a * np.float32(LOG2E)
    hi = c.astype(BF16).astype(np.float32)
    lo = c - hi
    rows = np.stack([hi, lo, hi * MOBA_BLOCK, lo * MOBA_BLOCK], axis=-1)
    rows = np.pad(rows, ((0, 0), (0, AUG - COL_POS - 4)))
    crow_a = jnp.asarray(np.broadcast_to(rows[:, :, None], rows.shape + (Q_TILE,)), BF16)

    nb = S // MOBA_BLOCK
    avg = jnp.asarray((pos[None, :] // MOBA_BLOCK == np.arange(nb)[:, None]) / MOBA_BLOCK, BF16)
    kq = np.arange(MOBA_BLOCK)
    causal = jnp.asarray(np.where(kq[:, None] <= kq[None, :], 0.0, NEG_BIG), F32)
    u = np.arange(2 * SWA_BLOCK)[:, None]
    t = np.arange(SWA_BLOCK)[None, :]
    ok = (u > t) & (u <= t + SWA_BLOCK)
    dist = (SWA_BLOCK + t - u).astype(np.float32)
    sc = np.asarray(slopes_c, np.float32) * np.float32(LOG2E)
    band = []
    for kv in range(C_KV_HEADS):
        bias = np.concatenate([-sc[2 * kv] * dist, -sc[2 * kv + 1] * dist], axis=1)
        ok2 = np.concatenate([ok, ok], axis=1)
        band.append(np.stack([np.where(ok2 & (u >= SWA_BLOCK), bias, NEG_BIG),
                              np.where(ok2, bias, NEG_BIG)]))
    band = jnp.asarray(np.stack(band), F32)
    w = np.repeat(np.asarray(POOL_WINDOWS, np.float32), 64)[None, :]
    first = 1.0 / np.minimum(np.arange(tm, dtype=np.float32)[:, None] + 1.0, w)
    invcnt = jnp.asarray(np.stack([first, np.broadcast_to(1.0 / w, (tm, 256))]), F32)
    return dict(kca=kca, crow_a=crow_a, avg=avg, causal=causal, band=band, invcnt=invcnt)


def kernel(x, norm_g, w_in, w_out, a_q_norm, a_k_norm, pool_w, pool_scale, c_q_norm, c_k_norm, c_sinks, conv_w):
    B, S, _ = x.shape
    depth = norm_g.shape[0]
    assert S % ROW_TILE == 0 and S // MOBA_BLOCK <= 32
    tb = _tables(S)
    tm = ROW_TILE
    d = np.arange(256)
    bd = jnp.asarray(d[:, None] // HEAD_DIM == d[None, :] // HEAD_DIM, BF16)
    qscale = QK_SCALE * LOG2E
    for l in range(depth):
        gq = jnp.broadcast_to(jnp.tile(a_q_norm[l] * qscale, N_HEADS)[:, None], (256, tm))
        gcq = jnp.broadcast_to(jnp.tile(c_q_norm[l] * qscale, N_HEADS)[:, None], (256, tm))
        gk = jnp.tile(a_k_norm[l], N_HEADS)[None, :]
        gck = jnp.tile(c_k_norm[l], C_KV_HEADS)[None, :]
        qT, kaug, vT, cqT, ck, cvT, rest = _in_proj(
            x, norm_g[l][None, :], w_in[l].astype(BF16), bd, gq, gk, gcq, gck, tb['kca'])
        ya = _moba(qT, kaug, vT, tb['avg'], tb['crow_a'], tb['causal'])
        sink = jnp.repeat((c_sinks[l] * LOG2E).reshape(C_KV_HEADS, 2), SWA_BLOCK, axis=1)[:, None, :]
        yc = _swa(cqT, ck, cvT, tb['band'], sink)
        pw = jnp.zeros((256, 256), F32)
        for g in range(4):
            pw = pw.at[g * 64:(g + 1) * 64, g * 64:(g + 1) * 64].set(pool_w[l, g])
        x = _out_proj(x, ya, yc, rest, tb['invcnt'], pw.astype(BF16), pool_scale[l][None, :],
                      conv_w[l], w_out[l].astype(BF16))
    return x
```

```python
import functools
import math

import jax
import jax.numpy as jnp
import numpy as np
from jax import lax
from jax.experimental import pallas as pl
from jax.experimental.pallas import tpu as pltpu

F32 = jnp.float32
BF16 = jnp.bfloat16

D_MODEL = 1024
HEAD_DIM = 64
GROUP_WIDTH = 256
N_HEADS = 4
C_KV_HEADS = 2
NORM_EPS = 1e-6
MOBA_BLOCK = 256
MOBA_TOPK = 3
SWA_BLOCK = 128
POOL_WINDOWS = (2, 4, 8, 16)
CONV_WIDTH = 3
HALO = 16

AUG = 128
COL_ONEHOT = 64
COL_POS = 96
NEG_BIG = -(2.0 ** 100)
LOG2E = math.log2(math.e)
QK_SCALE = HEAD_DIM ** -0.5

ROW_TILE = 512
Q_TILE = 256
VMEM_LIMIT = 56 * 1024 * 1024

SEG = dict(aq=0, ak=256, av=512, ag=768, bu=1024, bg=1280, cq=1536, ck=1792, cv=1920,
           cg=2048, dh=2304, db=2560, dc=2816, dg=3072)
IN_PROJ_WIDTH = 3328
REST_WIDTH = 2048


def _dot(a, b):
    return jnp.dot(a, b, preferred_element_type=F32)


def _in_proj_kernel(x_ref, g_ref, w_ref, bd_ref, gq_ref, gk_ref, gcq_ref, gck_ref, kca_ref,
                    qT_ref, kaug_ref, vT_ref, cqT_ref, ck_ref, cvT_ref, rest_ref):
    tm = x_ref.shape[1]
    x = x_ref[0]
    ms = jnp.mean(x * x, axis=-1, keepdims=True)
    h = (x * lax.rsqrt(ms + NORM_EPS) * g_ref[...]).astype(BF16)

    def proj(lo, width):
        return _dot(h, w_ref[:, lo:lo + width])

    def norm_t(p, gain_ref):
        pt = p.T
        outs = []
        for hh in range(N_HEADS):
            ph = pt[hh * HEAD_DIM:(hh + 1) * HEAD_DIM]
            ss = jnp.sum(ph * ph, axis=0, keepdims=True) * (1.0 / HEAD_DIM)
            outs.append(ph * lax.rsqrt(ss + NORM_EPS))
        return (jnp.concatenate(outs, axis=0) * gain_ref[...]).astype(BF16)

    def norm_rows(p, gain_ref, width):
        sq = (p * p)
        hi = sq.astype(BF16)
        lo = (sq - hi.astype(F32)).astype(BF16)
        bd = bd_ref[0:width, 0:width]
        ss = (_dot(hi, bd) + _dot(lo, bd)) * (1.0 / HEAD_DIM)
        return p * lax.rsqrt(ss + NORM_EPS) * gain_ref[...]

    def build_kaug(kn, kconst, n_heads, out_ref):
        lane = lax.broadcasted_iota(jnp.int32, (tm, AUG), 1)
        for hh in range(n_heads):
            col = kn[:, (hh // 2) * AUG:(hh // 2 + 1) * AUG]
            if hh % 2 == 1:
                col = pltpu.roll(col, HEAD_DIM, axis=1)
            out_ref[0, hh] = jnp.where(lane < HEAD_DIM, col, kconst).astype(BF16)

    nblk = tm // Q_TILE
    qT = norm_t(proj(SEG['aq'], 256), gq_ref)
    for t in range(nblk):
        qT_ref[0, t] = qT[:, t * Q_TILE:(t + 1) * Q_TILE]
    kn = norm_rows(proj(SEG['ak'], 256), gk_ref, 256)
    build_kaug(kn, kca_ref[...].astype(F32), N_HEADS, kaug_ref)
    vT = proj(SEG['av'], 256).T.astype(BF16)
    for t in range(nblk):
        vT_ref[0, t] = vT[:, t * Q_TILE:(t + 1) * Q_TILE]
    cqT = norm_t(proj(SEG['cq'], 256), gcq_ref)
    for t in range(nblk):
        cqT_ref[0, t] = cqT[:, t * Q_TILE:(t + 1) * Q_TILE]
    ck_ref[0] = norm_rows(proj(SEG['ck'], 128), gck_ref, 128).astype(BF16)
    cvT = proj(SEG['cv'], 128).T.astype(BF16)
    for t in range(tm // SWA_BLOCK):
        cvT_ref[0, t] = cvT[:, t * SWA_BLOCK:(t + 1) * SWA_BLOCK]
    rest_ref[0, :, 0:768] = proj(SEG['ag'], 768).astype(BF16)
    rest_ref[0, :, 768:1024] = proj(SEG['cg'], 256).astype(BF16)
    rest_ref[0, :, 1024:2048] = proj(SEG['dh'], 1024).astype(BF16)


def _in_proj(x, g, w, bd, gq, gk, gcq, gck, kca):
    B, S, _ = x.shape
    tm = ROW_TILE
    ns = S // tm
    nq = S // Q_TILE
    nkb = S // SWA_BLOCK
    const = lambda *shape: pl.BlockSpec(shape, lambda b, s: (0,) * len(shape))
    out_shape = (
        jax.ShapeDtypeStruct((B, nq, 256, Q_TILE), BF16),
        jax.ShapeDtypeStruct((B, N_HEADS, S, AUG), BF16),
        jax.ShapeDtypeStruct((B, nq, 256, Q_TILE), BF16),
        jax.ShapeDtypeStruct((B, nq, 256, Q_TILE), BF16),
        jax.ShapeDtypeStruct((B, S, 128), BF16),
        jax.ShapeDtypeStruct((B, nkb, 128, SWA_BLOCK), BF16),
        jax.ShapeDtypeStruct((B, S, REST_WIDTH), BF16),
    )
    r = tm // Q_TILE
    out_specs = (
        pl.BlockSpec((1, r, 256, Q_TILE), lambda b, s: (b, s, 0, 0)),
        pl.BlockSpec((1, N_HEADS, tm, AUG), lambda b, s: (b, 0, s, 0)),
        pl.BlockSpec((1, r, 256, Q_TILE), lambda b, s: (b, s, 0, 0)),
        pl.BlockSpec((1, r, 256, Q_TILE), lambda b, s: (b, s, 0, 0)),
        pl.BlockSpec((1, tm, 128), lambda b, s: (b, s, 0)),
        pl.BlockSpec((1, tm // SWA_BLOCK, 128, SWA_BLOCK), lambda b, s: (b, s, 0, 0)),
        pl.BlockSpec((1, tm, REST_WIDTH), lambda b, s: (b, s, 0)),
    )
    in_specs = [
        pl.BlockSpec((1, tm, D_MODEL), lambda b, s: (b, s, 0)),
        const(1, D_MODEL),
        const(D_MODEL, IN_PROJ_WIDTH),
        const(256, 256),
        const(256, tm), const(1, 256), const(256, tm), const(1, 128),
        pl.BlockSpec((tm, AUG), lambda b, s: (s, 0)),
    ]
    return pl.pallas_call(
        _in_proj_kernel, grid=(B, ns), in_specs=in_specs, out_specs=out_specs, out_shape=out_shape,
        compiler_params=pltpu.CompilerParams(dimension_semantics=("parallel", "parallel"),
                                             vmem_limit_bytes=VMEM_LIMIT),
        name="in_proj",
    )(x, g, w, bd, gq, gk, gcq, gck, kca)


def _moba_kernel(qT_ref, kaug_ref, vT_ref, avg_ref, crow_ref, causal_ref, out_ref,
                 kmean_ref, qaug_ref, acc_ref, s_ref, p_ref, mx_ref):
    i = pl.program_id(1)
    S = kaug_ref.shape[2]
    nb = S // MOBA_BLOCK
    chunk = 1024

    @pl.when(i == 0)
    def _():
        for hh in range(N_HEADS):
            acc = jnp.zeros((nb, AUG), F32)
            for c in range(S // chunk):
                acc = acc + _dot(avg_ref[:, c * chunk:(c + 1) * chunk],
                                 kaug_ref[0, hh, c * chunk:(c + 1) * chunk, :])
            kmean_ref[hh] = acc

    row = lax.broadcasted_iota(jnp.int32, (nb, Q_TILE), 0).astype(F32)
    i_f = i.astype(F32)
    neg_inf = jnp.float32(-jnp.inf)
    heads = range(N_HEADS)
    vh = lambda hh, j: vT_ref[0, j, hh * HEAD_DIM:(hh + 1) * HEAD_DIM, :]
    kb = lambda hh, j: kaug_ref[0, hh, pl.ds(pl.multiple_of(j * MOBA_BLOCK, MOBA_BLOCK), MOBA_BLOCK), :]

    for hh in heads:
        qh = qT_ref[0, 0, hh * HEAD_DIM:(hh + 1) * HEAD_DIM, :]
        km = kmean_ref[hh][:, 0:HEAD_DIM]
        km_hi = km.astype(BF16)
        km_lo = (km - km_hi.astype(F32)).astype(BF16)
        bs = _dot(km_hi, qh) + _dot(km_lo, qh)
        xs = jnp.where(row < i_f, bs, neg_inf)
        sel = row == i_f
        for _ in range(MOBA_TOPK):
            mx = jnp.max(xs, axis=0, keepdims=True)
            first = jnp.min(jnp.where(xs == mx, row, jnp.float32(nb)), axis=0, keepdims=True)
            pick = row == first
            sel = sel | (pick & (mx > neg_inf))
            xs = jnp.where(pick, neg_inf, xs)
        negmask = jnp.where(sel, 0.0, NEG_BIG).astype(BF16)
        pad = jnp.zeros((COL_POS - COL_ONEHOT - nb, Q_TILE), BF16) if nb < 32 else None
        parts = [qh, negmask] + ([pad] if pad is not None else []) + [crow_ref[hh]]
        qaug_ref[hh] = jnp.concatenate(parts, axis=0)

    def blk(pos):
        past = jnp.minimum(i + 1, nb - 1)
        return jnp.where(pos <= 0, i, jnp.where(pos <= i, pos - 1, past))

    kblk = MOBA_BLOCK
    sub = 8
    rows = 64

    def fold8(x, op):
        out = x[0:sub]
        for g in range(1, x.shape[0] // sub):
            out = op(out, x[g * sub:(g + 1) * sub])
        return out

    def scores(hh, half, s):
        s_ref[hh, half * kblk:(half + 1) * kblk] = s
        mx_ref[hh, half * sub:(half + 1) * sub] = fold8(s, jnp.maximum)

    for hh in heads:
        scores(hh, 0, _dot(kb(hh, i), qaug_ref[hh]) + causal_ref[...])
        scores(hh, 1, _dot(kb(hh, blk(1)), qaug_ref[hh]))
    p_ref[...] = jnp.zeros(p_ref.shape, BF16)
    acc_ref[...] = jnp.zeros(acc_ref.shape, F32)
    neg = jnp.full((1, Q_TILE), NEG_BIG, F32)
    one = jnp.ones((1, Q_TILE), F32)

    def values_step(t_prev, alphas):
        for hh in heads:
            acc_ref[hh] = (alphas[hh] * acc_ref[hh]
                           + _dot(vh(hh, blk(2 * t_prev)), p_ref[hh, 0:kblk])
                           + _dot(vh(hh, blk(2 * t_prev + 1)), p_ref[hh, kblk:2 * kblk]))

    def body(t, carry):
        ms, ls, alphas = carry
        values_step(t - 1, alphas)
        ms_new, ls_new, alphas_new = [], [], []
        for hh in heads:
            m_new = jnp.maximum(ms[hh], jnp.max(mx_ref[hh], axis=0, keepdims=True))
            alpha = jnp.exp2(ms[hh] - m_new)
            l8 = alpha * ls[hh]
            for c in range(2 * kblk // rows):
                p = jnp.exp2(s_ref[hh, c * rows:(c + 1) * rows] - m_new)
                p_ref[hh, c * rows:(c + 1) * rows] = p.astype(BF16)
                l8 = l8 + fold8(p, jnp.add)
            ms_new.append(m_new)
            ls_new.append(l8)
            alphas_new.append(alpha)
        for hh in heads:
            scores(hh, 0, _dot(kb(hh, blk(2 * t + 2)), qaug_ref[hh]))
            scores(hh, 1, _dot(kb(hh, blk(2 * t + 3)), qaug_ref[hh]))
        return tuple(ms_new), tuple(ls_new), tuple(alphas_new)

    zero8 = jnp.zeros((sub, Q_TILE), F32)
    init = ((neg,) * N_HEADS, (zero8,) * N_HEADS, (one,) * N_HEADS)
    steps = (i + 2) // 2
    ms, ls, alphas = lax.fori_loop(0, steps, body, init)
    values_step(steps - 1, alphas)
    outs = [acc_ref[hh] / jnp.sum(ls[hh], axis=0, keepdims=True) for hh in heads]
    out_ref[0] = jnp.concatenate(outs, axis=0).T.astype(BF16)


def _moba(qT, kaug, vT, avg, crow, causal):
    B, nq = qT.shape[0], qT.shape[1]
    S = kaug.shape[2]
    nb = S // MOBA_BLOCK
    in_specs = [
        pl.BlockSpec((1, 1, 256, Q_TILE), lambda b, i: (b, i, 0, 0)),
        pl.BlockSpec((1, N_HEADS, S, AUG), lambda b, i: (b, 0, 0, 0)),
        pl.BlockSpec((1, nq, 256, Q_TILE), lambda b, i: (b, 0, 0, 0)),
        pl.BlockSpec((nb, S), lambda b, i: (0, 0)),
        pl.BlockSpec((N_HEADS, 32, Q_TILE), lambda b, i: (0, 0, 0)),
        pl.BlockSpec((MOBA_BLOCK, Q_TILE), lambda b, i: (0, 0)),
    ]
    return pl.pallas_call(
        _moba_kernel, grid=(B, nq), in_specs=in_specs,
        out_specs=pl.BlockSpec((1, Q_TILE, 256), lambda b, i: (b, i, 0)),
        out_shape=jax.ShapeDtypeStruct((B, S, 256), BF16),
        scratch_shapes=[pltpu.VMEM((N_HEADS, nb, AUG), F32),
                        pltpu.VMEM((N_HEADS, AUG, Q_TILE), BF16),
                        pltpu.VMEM((N_HEADS, HEAD_DIM, Q_TILE), F32),
                        pltpu.VMEM((N_HEADS, 2 * MOBA_BLOCK, Q_TILE), F32),
                        pltpu.VMEM((N_HEADS, 2 * MOBA_BLOCK, Q_TILE), BF16),
                        pltpu.VMEM((N_HEADS, 16, Q_TILE), F32)],
        compiler_params=pltpu.CompilerParams(dimension_semantics=("parallel", "arbitrary"),
                                             vmem_limit_bytes=VMEM_LIMIT),
        name="moba",
    )(qT, kaug, vT, avg, crow, causal)


def _swa_kernel(cqT_ref, ck_ref, cvT_ref, band_ref, sink_ref, out_ref):
    i = pl.program_id(1)
    halves = Q_TILE // SWA_BLOCK
    zrows = jnp.zeros((HEAD_DIM, 2 * SWA_BLOCK), BF16)
    outs = {}
    for t in range(halves):
        n = i * halves + t
        prev = jnp.maximum(n - 1, 0)
        kwin = jnp.concatenate([
            ck_ref[0, pl.ds(pl.multiple_of(prev * SWA_BLOCK, SWA_BLOCK), SWA_BLOCK), :],
            ck_ref[0, pl.ds(pl.multiple_of(n * SWA_BLOCK, SWA_BLOCK), SWA_BLOCK), :]], axis=0)
        for kv in range(C_KV_HEADS):
            vwin = jnp.concatenate([cvT_ref[0, prev, kv * HEAD_DIM:(kv + 1) * HEAD_DIM, :],
                                    cvT_ref[0, n, kv * HEAD_DIM:(kv + 1) * HEAD_DIM, :]], axis=1)
            q2 = jnp.concatenate([
                cqT_ref[0, 0, (2 * kv + g) * HEAD_DIM:(2 * kv + g + 1) * HEAD_DIM,
                        t * SWA_BLOCK:(t + 1) * SWA_BLOCK] for g in range(2)], axis=1)
            qz = jnp.concatenate([q2, zrows] if kv == 0 else [zrows, q2], axis=0)
            band = jnp.where(n == 0, band_ref[kv, 0], band_ref[kv, 1])
            s = _dot(kwin, qz) + band
            m = jnp.max(s, axis=0, keepdims=True)
            p = jnp.exp2(s - m)
            l = jnp.sum(p, axis=0, keepdims=True) + jnp.exp2(sink_ref[kv] - m)
            outs[kv, t] = _dot(vwin, p.astype(BF16)) / l
    rows = []
    for kv in range(C_KV_HEADS):
        for g in range(2):
            rows.append(jnp.concatenate(
                [outs[kv, t][:, g * SWA_BLOCK:(g + 1) * SWA_BLOCK] for t in range(halves)], axis=1))
    out_ref[0] = jnp.concatenate(rows, axis=0).T.astype(BF16)


def _swa(cqT, ck, cvT, band, sink):
    B, nq = cqT.shape[0], cqT.shape[1]
    S = ck.shape[1]
    nkb = S // SWA_BLOCK
    in_specs = [
        pl.BlockSpec((1, 1, 256, Q_TILE), lambda b, i: (b, i, 0, 0)),
        pl.BlockSpec((1, S, 128), lambda b, i: (b, 0, 0)),
        pl.BlockSpec((1, nkb, 128, SWA_BLOCK), lambda b, i: (b, 0, 0, 0)),
        pl.BlockSpec((C_KV_HEADS, 2, 2 * SWA_BLOCK, 2 * SWA_BLOCK), lambda b, i: (0, 0, 0, 0)),
        pl.BlockSpec((C_KV_HEADS, 1, 2 * SWA_BLOCK), lambda b, i: (0, 0, 0)),
    ]
    return pl.pallas_call(
        _swa_kernel, grid=(B, nq), in_specs=in_specs,
        out_specs=pl.BlockSpec((1, Q_TILE, 256), lambda b, i: (b, i, 0)),
        out_shape=jax.ShapeDtypeStruct((B, S, 256), BF16),
        compiler_params=pltpu.CompilerParams(dimension_semantics=("parallel", "parallel"),
                                             vmem_limit_bytes=VMEM_LIMIT),
        name="swa",
    )(cqT, ck, cvT, band, sink)


def _silu(v):
    return v * (1.0 / (1.0 + jnp.exp(-v)))


def _out_proj_kernel(x_ref, ya_ref, yc_ref, rest_ref, halo_ref, invcnt_ref, pw_ref, pscale_ref,
                     cw_ref, wo_ref, out_ref, ubuf, cbuf):
    s = pl.program_id(1)
    tm = x_ref.shape[1]
    seg = lambda k: rest_ref[0, :, k * 256:(k + 1) * 256].astype(F32)
    ag, bu, bg, cg, dh, db, dc, dg = (seg(k) for k in range(8))
    live = (s > 0).astype(F32)
    hseg = lambda k: halo_ref[0, :, k * 256:(k + 1) * 256].astype(F32) * live

    ubuf[0:HALO, :] = hseg(1)
    ubuf[HALO:HALO + tm, :] = bu
    lane = lax.broadcasted_iota(jnp.int32, (tm, 128), 1)
    sh = lambda buf, k, c: buf[HALO - k:HALO - k + tm, c * 128:(c + 1) * 128]
    a2 = sh(ubuf, 0, 0) + sh(ubuf, 1, 0)
    a4 = a2 + sh(ubuf, 2, 0) + sh(ubuf, 3, 0)
    left = jnp.where(lane < 64, a2, a4)
    a8 = sh(ubuf, 0, 1)
    for k in range(1, 8):
        a8 = a8 + sh(ubuf, k, 1)
    a16 = a8
    for k in range(8, 16):
        a16 = a16 + sh(ubuf, k, 1)
    right = jnp.where(lane < 64, a8, a16)
    pooled = jnp.concatenate([left, right], axis=1) * invcnt_ref[0] - bu
    yb = _dot(pooled.astype(BF16), pw_ref[...]) * pscale_ref[...] * _silu(bg)

    cbuf[0:HALO, :] = hseg(6) * hseg(4)
    cbuf[HALO:HALO + tm, :] = dc * dh
    conv = (cw_ref[0:1, :] * cbuf[HALO - 2:HALO - 2 + tm, :]
            + cw_ref[1:2, :] * cbuf[HALO - 1:HALO - 1 + tm, :]
            + cw_ref[2:3, :] * cbuf[HALO:HALO + tm, :])
    yd = db * conv * _silu(dg)

    ya = ya_ref[0].astype(F32) * _silu(ag)
    yc = yc_ref[0].astype(F32) * _silu(cg)
    y = (_dot(ya.astype(BF16), wo_ref[0:256, :]) + _dot(yb.astype(BF16), wo_ref[256:512, :])
         + _dot(yc.astype(BF16), wo_ref[512:768, :]) + _dot(yd.astype(BF16), wo_ref[768:1024, :]))
    out_ref[0] = x_ref[0] + y


def _out_proj(x, ya, yc, rest, invcnt, pw, pscale, cw, wo):
    B, S, _ = x.shape
    tm = ROW_TILE
    ns = S // tm
    hb = tm // HALO
    const = lambda *shape: pl.BlockSpec(shape, lambda b, s: (0,) * len(shape))
    in_specs = [
        pl.BlockSpec((1, tm, D_MODEL), lambda b, s: (b, s, 0)),
        pl.BlockSpec((1, tm, 256), lambda b, s: (b, s, 0)),
        pl.BlockSpec((1, tm, 256), lambda b, s: (b, s, 0)),
        pl.BlockSpec((1, tm, REST_WIDTH), lambda b, s: (b, s, 0)),
        pl.BlockSpec((1, HALO, REST_WIDTH), lambda b, s: (b, jnp.maximum(s * hb - 1, 0), 0)),
        pl.BlockSpec((1, tm, 256), lambda b, s: (jnp.minimum(s, 1), 0, 0)),
        const(256, 256), const(1, 256), const(CONV_WIDTH, 256), const(D_MODEL, D_MODEL),
    ]
    return pl.pallas_call(
        _out_proj_kernel, grid=(B, ns), in_specs=in_specs,
        out_specs=pl.BlockSpec((1, tm, D_MODEL), lambda b, s: (b, s, 0)),
        out_shape=jax.ShapeDtypeStruct((B, S, D_MODEL), F32),
        scratch_shapes=[pltpu.VMEM((HALO + tm, 256), F32), pltpu.VMEM((HALO + tm, 256), F32)],
        compiler_params=pltpu.CompilerParams(dimension_semantics=("parallel", "parallel"),
                                             vmem_limit_bytes=VMEM_LIMIT),
        name="out_proj",
    )(x, ya, yc, rest, rest, invcnt, pw, pscale, cw, wo)


def _tables(S):
    tm = ROW_TILE
    slopes = np.exp2(-(8.0 / (2 * N_HEADS)) * np.arange(1, 2 * N_HEADS + 1)).astype(np.float32)
    slopes_c, slopes_a = slopes[:N_HEADS], slopes[N_HEADS:]
    pos = np.arange(S)

    kca = np.zeros((S, AUG), np.float32)
    blk, r = pos // MOBA_BLOCK, pos % MOBA_BLOCK
    kca[pos, COL_ONEHOT + blk] = 1.0
    kca[:, COL_POS] = r
    kca[:, COL_POS + 1] = r
    kca[:, COL_POS + 2] = blk
    kca[:, COL_POS + 3] = blk
    kca = jnp.asarray(kca, BF16)

    c = slopes_a * np.float32(LOG2E)
    hi = c.astype(BF16).astype(np.float32)
    lo = c - hi
    rows = np.stack([hi, lo, hi * MOBA_BLOCK, lo * MOBA_BLOCK], axis=-1)
    rows = np.pad(rows, ((0, 0), (0, AUG - COL_POS - 4)))
    crow_a = jnp.asarray(np.broadcast_to(rows[:, :, None], rows.shape + (Q_TILE,)), BF16)

    nb = S // MOBA_BLOCK
    avg = jnp.asarray((pos[None, :] // MOBA_BLOCK == np.arange(nb)[:, None]) / MOBA_BLOCK, BF16)
    kq = np.arange(MOBA_BLOCK)
    causal = jnp.asarray(np.where(kq[:, None] <= kq[None, :], 0.0, NEG_BIG), F32)
    u = np.arange(2 * SWA_BLOCK)[:, None]
    t = np.arange(SWA_BLOCK)[None, :]
    ok = (u > t) & (u <= t + SWA_BLOCK)
    dist = (SWA_BLOCK + t - u).astype(np.float32)
    sc = np.asarray(slopes_c, np.float32) * np.float32(LOG2E)
    band = []
    for kv in range(C_KV_HEADS):
        bias = np.concatenate([-sc[2 * kv] * dist, -sc[2 * kv + 1] * dist], axis=1)
        ok2 = np.concatenate([ok, ok], axis=1)
        band.append(np.stack([np.where(ok2 & (u >= SWA_BLOCK), bias, NEG_BIG),
                              np.where(ok2, bias, NEG_BIG)]))
    band = jnp.asarray(np.stack(band), F32)
    w = np.repeat(np.asarray(POOL_WINDOWS, np.float32), 64)[None, :]
    first = 1.0 / np.minimum(np.arange(tm, dtype=np.float32)[:, None] + 1.0, w)
    invcnt = jnp.asarray(np.stack([first, np.broadcast_to(1.0 / w, (tm, 256))]), F32)
    return dict(kca=kca, crow_a=crow_a, avg=avg, causal=causal, band=band, invcnt=invcnt)


def kernel(x, norm_g, w_in, w_out, a_q_norm, a_k_norm, pool_w, pool_scale, c_q_norm, c_k_norm, c_sinks, conv_w):
    B, S, _ = x.shape
    depth = norm_g.shape[0]
    assert S % ROW_TILE == 0 and S // MOBA_BLOCK <= 32
    tb = _tables(S)
    tm = ROW_TILE
    d = np.arange(256)
    bd = jnp.asarray(d[:, None] // HEAD_DIM == d[None, :] // HEAD_DIM, BF16)
    qscale = QK_SCALE * LOG2E
    for l in range(depth):
        gq = jnp.broadcast_to(jnp.tile(a_q_norm[l] * qscale, N_HEADS)[:, None], (256, tm))
        gcq = jnp.broadcast_to(jnp.tile(c_q_norm[l] * qscale, N_HEADS)[:, None], (256, tm))
        gk = jnp.tile(a_k_norm[l], N_HEADS)[None, :]
        gck = jnp.tile(c_k_norm[l], C_KV_HEADS)[None, :]
        qT, kaug, vT, cqT, ck, cvT, rest = _in_proj(
            x, norm_g[l][None, :], w_in[l].astype(BF16), bd, gq, gk, gcq, gck, tb['kca'])
        ya = _moba(qT, kaug, vT, tb['avg'], tb['crow_a'], tb['causal'])
        sink = jnp.repeat((c_sinks[l] * LOG2E).reshape(C_KV_HEADS, 2), SWA_BLOCK, axis=1)[:, None, :]
        yc = _swa(cqT, ck, cvT, tb['band'], sink)
        pw = jnp.zeros((256, 256), F32)
        for g in range(4):
            pw = pw.at[g * 64:(g + 1) * 64, g * 64:(g + 1) * 64].set(pool_w[l, g])
        x = _out_proj(x, ya, yc, rest, tb['invcnt'], pw.astype(BF16), pool_scale[l][None, :],
                      conv_w[l], w_out[l].astype(BF16))
    return x
```

```python
import functools
import math

import jax
import jax.numpy as jnp
import numpy as np
from jax import lax
from jax.experimental import pallas as pl
from jax.experimental.pallas import tpu as pltpu

F32 = jnp.float32
BF16 = jnp.bfloat16

D_MODEL = 1024
HEAD_DIM = 64
GROUP_WIDTH = 256
N_HEADS = 4
C_KV_HEADS = 2
NORM_EPS = 1e-6
MOBA_BLOCK = 256
MOBA_TOPK = 3
SWA_BLOCK = 128
POOL_WINDOWS = (2, 4, 8, 16)
CONV_WIDTH = 3
HALO = 16

AUG = 128
COL_ONEHOT = 64
COL_POS = 96
NEG_BIG = -(2.0 ** 100)
LOG2E = math.log2(math.e)
QK_SCALE = HEAD_DIM ** -0.5

ROW_TILE = 512
Q_TILE = 256
SWA_TILE = 512
VMEM_LIMIT = 56 * 1024 * 1024

SEG = dict(aq=0, ak=256, av=512, ag=768, bu=1024, bg=1280, cq=1536, ck=1792, cv=1920,
           cg=2048, dh=2304, db=2560, dc=2816, dg=3072)
IN_PROJ_WIDTH = 3328
REST_WIDTH = 2048


def _dot(a, b):
    return jnp.dot(a, b, preferred_element_type=F32)


def _in_proj_kernel(x_ref, g_ref, w_ref, bd_ref, gq_ref, gk_ref, gcq_ref, gck_ref, kca_ref,
                    qT_ref, kaug_ref, vT_ref, cqT_ref, ck_ref, cvT_ref, rest_ref):
    tm = x_ref.shape[1]
    x = x_ref[0]
    ms = jnp.mean(x * x, axis=-1, keepdims=True)
    h = (x * lax.rsqrt(ms + NORM_EPS) * g_ref[...]).astype(BF16)

    def proj(lo, width):
        return _dot(h, w_ref[:, lo:lo + width])

    def norm_t(p, gain_ref):
        pt = p.T
        outs = []
        for hh in range(N_HEADS):
            ph = pt[hh * HEAD_DIM:(hh + 1) * HEAD_DIM]
            ss = jnp.sum(ph * ph, axis=0, keepdims=True) * (1.0 / HEAD_DIM)
            outs.append(ph * lax.rsqrt(ss + NORM_EPS))
        return (jnp.concatenate(outs, axis=0) * gain_ref[...]).astype(BF16)

    def norm_rows(p, gain_ref, width):
        ss = _dot((p * p).astype(BF16), bd_ref[0:width, 0:width]) * (1.0 / HEAD_DIM)
        return p * lax.rsqrt(ss + NORM_EPS) * gain_ref[...]

    def build_kaug(kn, kconst, n_heads, out_ref):
        lane = lax.broadcasted_iota(jnp.int32, (tm, AUG), 1)
        for hh in range(n_heads):
            col = kn[:, (hh // 2) * AUG:(hh // 2 + 1) * AUG]
            if hh % 2 == 1:
                col = pltpu.roll(col, HEAD_DIM, axis=1)
            out_ref[0, hh] = jnp.where(lane < HEAD_DIM, col, kconst).astype(BF16)

    nblk = tm // Q_TILE
    qT = norm_t(proj(SEG['aq'], 256), gq_ref)
    for t in range(nblk):
        qT_ref[0, t] = qT[:, t * Q_TILE:(t + 1) * Q_TILE]
    kn = norm_rows(proj(SEG['ak'], 256), gk_ref, 256)
    build_kaug(kn, kca_ref[...].astype(F32), N_HEADS, kaug_ref)
    vT = proj(SEG['av'], 256).T.astype(BF16)
    for t in range(nblk):
        vT_ref[0, t] = vT[:, t * Q_TILE:(t + 1) * Q_TILE]
    cqT = norm_t(proj(SEG['cq'], 256), gcq_ref)
    for t in range(nblk):
        cqT_ref[0, t] = cqT[:, t * Q_TILE:(t + 1) * Q_TILE]
    ck_ref[0] = norm_rows(proj(SEG['ck'], 128), gck_ref, 128).astype(BF16)
    cvT = proj(SEG['cv'], 128).T.astype(BF16)
    for t in range(tm // SWA_BLOCK):
        cvT_ref[0, t] = cvT[:, t * SWA_BLOCK:(t + 1) * SWA_BLOCK]
    rest_ref[0, :, 0:768] = proj(SEG['ag'], 768).astype(BF16)
    rest_ref[0, :, 768:1024] = proj(SEG['cg'], 256).astype(BF16)
    rest_ref[0, :, 1024:2048] = proj(SEG['dh'], 1024).astype(BF16)


def _in_proj(x, g, w, bd, gq, gk, gcq, gck, kca):
    B, S, _ = x.shape
    tm = ROW_TILE
    ns = S // tm
    nq = S // Q_TILE
    nkb = S // SWA_BLOCK
    const = lambda *shape: pl.BlockSpec(shape, lambda b, s: (0,) * len(shape))
    out_shape = (
        jax.ShapeDtypeStruct((B, nq, 256, Q_TILE), BF16),
        jax.ShapeDtypeStruct((B, N_HEADS, S, AUG), BF16),
        jax.ShapeDtypeStruct((B, nq, 256, Q_TILE), BF16),
        jax.ShapeDtypeStruct((B, nq, 256, Q_TILE), BF16),
        jax.ShapeDtypeStruct((B, S, 128), BF16),
        jax.ShapeDtypeStruct((B, nkb, 128, SWA_BLOCK), BF16),
        jax.ShapeDtypeStruct((B, S, REST_WIDTH), BF16),
    )
    r = tm // Q_TILE
    out_specs = (
        pl.BlockSpec((1, r, 256, Q_TILE), lambda b, s: (b, s, 0, 0)),
        pl.BlockSpec((1, N_HEADS, tm, AUG), lambda b, s: (b, 0, s, 0)),
        pl.BlockSpec((1, r, 256, Q_TILE), lambda b, s: (b, s, 0, 0)),
        pl.BlockSpec((1, r, 256, Q_TILE), lambda b, s: (b, s, 0, 0)),
        pl.BlockSpec((1, tm, 128), lambda b, s: (b, s, 0)),
        pl.BlockSpec((1, tm // SWA_BLOCK, 128, SWA_BLOCK), lambda b, s: (b, s, 0, 0)),
        pl.BlockSpec((1, tm, REST_WIDTH), lambda b, s: (b, s, 0)),
    )
    in_specs = [
        pl.BlockSpec((1, tm, D_MODEL), lambda b, s: (b, s, 0)),
        const(1, D_MODEL),
        const(D_MODEL, IN_PROJ_WIDTH),
        const(256, 256),
        const(256, tm), const(1, 256), const(256, tm), const(1, 128),
        pl.BlockSpec((tm, AUG), lambda b, s: (s, 0)),
    ]
    return pl.pallas_call(
        _in_proj_kernel, grid=(B, ns), in_specs=in_specs, out_specs=out_specs, out_shape=out_shape,
        compiler_params=pltpu.CompilerParams(dimension_semantics=("parallel", "parallel"),
                                             vmem_limit_bytes=VMEM_LIMIT),
        name="in_proj",
    )(x, g, w, bd, gq, gk, gcq, gck, kca)


def _moba_kernel(qT_ref, kaug_ref, vT_ref, avg_ref, crow_ref, causal_ref, out_ref,
                 kmean_ref, qaug_ref, acc_ref, s_ref, p_ref, mx_ref):
    i = pl.program_id(1)
    S = kaug_ref.shape[2]
    nb = S // MOBA_BLOCK
    chunk = 1024

    @pl.when(i == 0)
    def _():
        for hh in range(N_HEADS):
            acc = jnp.zeros((nb, AUG), F32)
            for c in range(S // chunk):
                acc = acc + _dot(avg_ref[:, c * chunk:(c + 1) * chunk],
                                 kaug_ref[0, hh, c * chunk:(c + 1) * chunk, :])
            kmean_ref[hh] = acc

    row = lax.broadcasted_iota(jnp.int32, (nb, Q_TILE), 0).astype(F32)
    i_f = i.astype(F32)
    neg_inf = jnp.float32(-jnp.inf)
    heads = range(N_HEADS)
    vh = lambda hh, j: vT_ref[0, j, hh * HEAD_DIM:(hh + 1) * HEAD_DIM, :]
    kb = lambda hh, j: kaug_ref[0, hh, pl.ds(pl.multiple_of(j * MOBA_BLOCK, MOBA_BLOCK), MOBA_BLOCK), :]

    for hh in heads:
        qh = qT_ref[0, 0, hh * HEAD_DIM:(hh + 1) * HEAD_DIM, :]
        km = kmean_ref[hh][:, 0:HEAD_DIM]
        km_hi = km.astype(BF16)
        km_lo = (km - km_hi.astype(F32)).astype(BF16)
        bs = _dot(km_hi, qh) + _dot(km_lo, qh)
        xs = jnp.where(row < i_f, bs, neg_inf)
        sel = row == i_f
        for _ in range(MOBA_TOPK):
            mx = jnp.max(xs, axis=0, keepdims=True)
            first = jnp.min(jnp.where(xs == mx, row, jnp.float32(nb)), axis=0, keepdims=True)
            pick = row == first
            sel = sel | (pick & (mx > neg_inf))
            xs = jnp.where(pick, neg_inf, xs)
        negmask = jnp.where(sel, 0.0, NEG_BIG).astype(BF16)
        pad = jnp.zeros((COL_POS - COL_ONEHOT - nb, Q_TILE), BF16) if nb < 32 else None
        parts = [qh, negmask] + ([pad] if pad is not None else []) + [crow_ref[hh]]
        qaug_ref[hh] = jnp.concatenate(parts, axis=0)

    def blk(pos):
        past = jnp.minimum(i + 1, nb - 1)
        return jnp.where(pos <= 0, i, jnp.where(pos <= i, pos - 1, past))

    kblk = MOBA_BLOCK
    sub = 8
    rows = 64

    fold8 = _fold8

    def scores(hh, half, s):
        s_ref[hh, half * kblk:(half + 1) * kblk] = s
        mx_ref[hh, half * sub:(half + 1) * sub] = fold8(s, jnp.maximum)

    for hh in heads:
        scores(hh, 0, _dot(kb(hh, i), qaug_ref[hh]) + causal_ref[...])
        scores(hh, 1, _dot(kb(hh, blk(1)), qaug_ref[hh]))
    p_ref[...] = jnp.zeros(p_ref.shape, BF16)
    acc_ref[...] = jnp.zeros(acc_ref.shape, F32)
    neg = jnp.full((1, Q_TILE), NEG_BIG, F32)
    one = jnp.ones((1, Q_TILE), F32)

    def values_step(t_prev, alphas):
        for hh in heads:
            acc_ref[hh] = (alphas[hh] * acc_ref[hh]
                           + _dot(vh(hh, blk(2 * t_prev)), p_ref[hh, 0:kblk])
                           + _dot(vh(hh, blk(2 * t_prev + 1)), p_ref[hh, kblk:2 * kblk]))

    def body(t, carry):
        ms, ls, alphas = carry
        values_step(t - 1, alphas)
        ms_new, ls_new, alphas_new = [], [], []
        for hh in heads:
            m_new = jnp.maximum(ms[hh], jnp.max(mx_ref[hh], axis=0, keepdims=True))
            alpha = jnp.exp2(ms[hh] - m_new)
            l8 = alpha * ls[hh]
            for c in range(2 * kblk // rows):
                p = jnp.exp2(s_ref[hh, c * rows:(c + 1) * rows] - m_new)
                p_ref[hh, c * rows:(c + 1) * rows] = p.astype(BF16)
                l8 = l8 + fold8(p, jnp.add)
            ms_new.append(m_new)
            ls_new.append(l8)
            alphas_new.append(alpha)
        for hh in heads:
            scores(hh, 0, _dot(kb(hh, blk(2 * t + 2)), qaug_ref[hh]))
            scores(hh, 1, _dot(kb(hh, blk(2 * t + 3)), qaug_ref[hh]))
        return tuple(ms_new), tuple(ls_new), tuple(alphas_new)

    zero8 = jnp.zeros((sub, Q_TILE), F32)
    init = ((neg,) * N_HEADS, (zero8,) * N_HEADS, (one,) * N_HEADS)
    steps = (i + 2) // 2
    ms, ls, alphas = lax.fori_loop(0, steps, body, init)
    values_step(steps - 1, alphas)
    outs = [acc_ref[hh] / jnp.sum(ls[hh], axis=0, keepdims=True) for hh in heads]
    out_ref[0] = jnp.concatenate(outs, axis=0).T.astype(BF16)


def _moba(qT, kaug, vT, avg, crow, causal):
    B, nq = qT.shape[0], qT.shape[1]
    S = kaug.shape[2]
    nb = S // MOBA_BLOCK
    in_specs = [
        pl.BlockSpec((1, 1, 256, Q_TILE), lambda b, i: (b, i, 0, 0)),
        pl.BlockSpec((1, N_HEADS, S, AUG), lambda b, i: (b, 0, 0, 0)),
        pl.BlockSpec((1, nq, 256, Q_TILE), lambda b, i: (b, 0, 0, 0)),
        pl.BlockSpec((nb, S), lambda b, i: (0, 0)),
        pl.BlockSpec((N_HEADS, 32, Q_TILE), lambda b, i: (0, 0, 0)),
        pl.BlockSpec((MOBA_BLOCK, Q_TILE), lambda b, i: (0, 0)),
    ]
    return pl.pallas_call(
        _moba_kernel, grid=(B, nq), in_specs=in_specs,
        out_specs=pl.BlockSpec((1, Q_TILE, 256), lambda b, i: (b, i, 0)),
        out_shape=jax.ShapeDtypeStruct((B, S, 256), BF16),
        scratch_shapes=[pltpu.VMEM((N_HEADS, nb, AUG), F32),
                        pltpu.VMEM((N_HEADS, AUG, Q_TILE), BF16),
                        pltpu.VMEM((N_HEADS, HEAD_DIM, Q_TILE), F32),
                        pltpu.VMEM((N_HEADS, 2 * MOBA_BLOCK, Q_TILE), F32),
                        pltpu.VMEM((N_HEADS, 2 * MOBA_BLOCK, Q_TILE), BF16),
                        pltpu.VMEM((N_HEADS, 16, Q_TILE), F32)],
        compiler_params=pltpu.CompilerParams(dimension_semantics=("parallel", "arbitrary"),
                                             vmem_limit_bytes=VMEM_LIMIT),
        name="moba",
    )(qT, kaug, vT, avg, crow, causal)


def _fold8(x, op):
    out = x[0:8]
    for g in range(1, x.shape[0] // 8):
        out = op(out, x[g * 8:(g + 1) * 8])
    return out


def _swa_kernel(cqT_ref, ck_ref, cvT_ref, band0_ref, band_ref, sink_ref, out_ref, s_ref, p_ref, mx_ref):
    i = pl.program_id(1)
    blocks = SWA_TILE // SWA_BLOCK
    win = 2 * SWA_BLOCK
    rows = 64
    zrows = jnp.zeros((HEAD_DIM, win), BF16)
    chains = [(t, kv) for t in range(blocks) for kv in range(C_KV_HEADS)]
    for t in range(blocks):
        n = i * blocks + t
        prev = jnp.maximum(n - 1, 0)
        kwin = jnp.concatenate([
            ck_ref[0, pl.ds(pl.multiple_of(prev * SWA_BLOCK, SWA_BLOCK), SWA_BLOCK), :],
            ck_ref[0, pl.ds(pl.multiple_of(n * SWA_BLOCK, SWA_BLOCK), SWA_BLOCK), :]], axis=0)
        for kv in range(C_KV_HEADS):
            c = chains.index((t, kv))
            q2 = jnp.concatenate([
                cqT_ref[0, t // 2, (2 * kv + g) * HEAD_DIM:(2 * kv + g + 1) * HEAD_DIM,
                        (t % 2) * SWA_BLOCK:(t % 2 + 1) * SWA_BLOCK] for g in range(2)], axis=1)
            qz = jnp.concatenate([q2, zrows] if kv == 0 else [zrows, q2], axis=0)
            band = band0_ref[kv, 0] if t == 0 else band_ref[kv, 0]
            s = _dot(kwin, qz) + band
            s_ref[c] = s
            mx_ref[c] = _fold8(s, jnp.maximum)
    inv = []
    for c, (t, kv) in enumerate(chains):
        m = jnp.max(mx_ref[c], axis=0, keepdims=True)
        l8 = jnp.zeros((8, win), F32)
        for r in range(win // rows):
            p = jnp.exp2(s_ref[c, r * rows:(r + 1) * rows] - m)
            p_ref[c, r * rows:(r + 1) * rows] = p.astype(BF16)
            l8 = l8 + _fold8(p, jnp.add)
        inv.append(1.0 / (jnp.sum(l8, axis=0, keepdims=True) + jnp.exp2(sink_ref[kv] - m)))
    outs = {}
    for c, (t, kv) in enumerate(chains):
        n = i * blocks + t
        prev = jnp.maximum(n - 1, 0)
        vwin = jnp.concatenate([cvT_ref[0, prev, kv * HEAD_DIM:(kv + 1) * HEAD_DIM, :],
                                cvT_ref[0, n, kv * HEAD_DIM:(kv + 1) * HEAD_DIM, :]], axis=1)
        outs[t, kv] = _dot(vwin, p_ref[c]) * inv[c]
    for u in range(SWA_TILE // Q_TILE):
        head_rows = []
        for kv in range(C_KV_HEADS):
            for g in range(2):
                head_rows.append(jnp.concatenate(
                    [outs[t, kv][:, g * SWA_BLOCK:(g + 1) * SWA_BLOCK] for t in (2 * u, 2 * u + 1)], axis=1))
        out_ref[0, u * Q_TILE:(u + 1) * Q_TILE, :] = jnp.concatenate(head_rows, axis=0).T.astype(BF16)


def _swa(cqT, ck, cvT, band, sink):
    B, nq = cqT.shape[0], cqT.shape[1]
    S = ck.shape[1]
    nkb = S // SWA_BLOCK
    r = SWA_TILE // Q_TILE
    nchain = (SWA_TILE // SWA_BLOCK) * C_KV_HEADS
    win = 2 * SWA_BLOCK
    in_specs = [
        pl.BlockSpec((1, r, 256, Q_TILE), lambda b, i: (b, i, 0, 0)),
        pl.BlockSpec((1, S, 128), lambda b, i: (b, 0, 0)),
        pl.BlockSpec((1, nkb, 128, SWA_BLOCK), lambda b, i: (b, 0, 0, 0)),
        pl.BlockSpec((C_KV_HEADS, 1, win, win), lambda b, i: (0, jnp.minimum(i, 1), 0, 0)),
        pl.BlockSpec((C_KV_HEADS, 1, win, win), lambda b, i: (0, 1, 0, 0)),
        pl.BlockSpec((C_KV_HEADS, 1, win), lambda b, i: (0, 0, 0)),
    ]
    return pl.pallas_call(
        _swa_kernel, grid=(B, S // SWA_TILE), in_specs=in_specs,
        out_specs=pl.BlockSpec((1, SWA_TILE, 256), lambda b, i: (b, i, 0)),
        out_shape=jax.ShapeDtypeStruct((B, S, 256), BF16),
        scratch_shapes=[pltpu.VMEM((nchain, win, win), F32),
                        pltpu.VMEM((nchain, win, win), BF16),
                        pltpu.VMEM((nchain, 8, win), F32)],
        compiler_params=pltpu.CompilerParams(dimension_semantics=("parallel", "parallel"),
                                             vmem_limit_bytes=VMEM_LIMIT),
        name="swa",
    )(cqT, ck, cvT, band, band, sink)


def _silu(v):
    h = 0.5 * v
    return h + h * jnp.tanh(h)


def _out_proj_kernel(x_ref, ya_ref, yc_ref, rest_ref, halo_ref, invcnt_ref, pw_ref, pscale_ref,
                     cw_ref, wo_ref, out_ref):
    s = pl.program_id(1)
    tm = x_ref.shape[1]
    seg = lambda k: rest_ref[0, :, k * 256:(k + 1) * 256].astype(F32)
    ag, bu, bg, cg, dh, db, dc, dg = (seg(k) for k in range(8))
    live = (s > 0).astype(F32)
    hseg = lambda k: halo_ref[0, :, k * 256:(k + 1) * 256].astype(F32) * live
    back = lambda v, k: pltpu.roll(v, k, axis=0)

    e = jnp.concatenate([hseg(1), bu], axis=0)
    a2 = e + back(e, 1)
    a4 = a2 + back(a2, 2)
    a4r = a4[:, 128:256]
    a8 = a4r + back(a4r, 4)
    a16 = a8 + back(a8, 8)
    lane = lax.broadcasted_iota(jnp.int32, (HALO + tm, 128), 1)
    sums = jnp.concatenate([jnp.where(lane < 64, a2[:, 0:128], a4[:, 0:128]),
                            jnp.where(lane < 64, a8, a16)], axis=1)
    pooled = sums[HALO:] * invcnt_ref[0] - bu
    yb = _dot(pooled.astype(BF16), pw_ref[...]) * pscale_ref[...] * _silu(bg)

    u = jnp.concatenate([hseg(6) * hseg(4), dc * dh], axis=0)
    conv = (cw_ref[0:1, :] * back(u, 2) + cw_ref[1:2, :] * back(u, 1) + cw_ref[2:3, :] * u)[HALO:]
    yd = db * conv * _silu(dg)

    ya = ya_ref[0].astype(F32) * _silu(ag)
    yc = yc_ref[0].astype(F32) * _silu(cg)
    y = (_dot(ya.astype(BF16), wo_ref[0:256, :]) + _dot(yb.astype(BF16), wo_ref[256:512, :])
         + _dot(yc.astype(BF16), wo_ref[512:768, :]) + _dot(yd.astype(BF16), wo_ref[768:1024, :]))
    out_ref[0] = x_ref[0] + y


def _out_proj(x, ya, yc, rest, invcnt, pw, pscale, cw, wo):
    B, S, _ = x.shape
    tm = ROW_TILE
    ns = S // tm
    hb = tm // HALO
    const = lambda *shape: pl.BlockSpec(shape, lambda b, s: (0,) * len(shape))
    in_specs = [
        pl.BlockSpec((1, tm, D_MODEL), lambda b, s: (b, s, 0)),
        pl.BlockSpec((1, tm, 256), lambda b, s: (b, s, 0)),
        pl.BlockSpec((1, tm, 256), lambda b, s: (b, s, 0)),
        pl.BlockSpec((1, tm, REST_WIDTH), lambda b, s: (b, s, 0)),
        pl.BlockSpec((1, HALO, REST_WIDTH), lambda b, s: (b, jnp.maximum(s * hb - 1, 0), 0)),
        pl.BlockSpec((1, tm, 256), lambda b, s: (jnp.minimum(s, 1), 0, 0)),
        const(256, 256), const(1, 256), const(CONV_WIDTH, 256), const(D_MODEL, D_MODEL),
    ]
    return pl.pallas_call(
        _out_proj_kernel, grid=(B, ns), in_specs=in_specs,
        out_specs=pl.BlockSpec((1, tm, D_MODEL), lambda b, s: (b, s, 0)),
        out_shape=jax.ShapeDtypeStruct((B, S, D_MODEL), F32),
        compiler_params=pltpu.CompilerParams(dimension_semantics=("parallel", "parallel"),
                                             vmem_limit_bytes=VMEM_LIMIT),
        name="out_proj",
    )(x, ya, yc, rest, rest, invcnt, pw, pscale, cw, wo)


def _tables(S):
    tm = ROW_TILE
    slopes = np.exp2(-(8.0 / (2 * N_HEADS)) * np.arange(1, 2 * N_HEADS + 1)).astype(np.float32)
    slopes_c, slopes_a = slopes[:N_HEADS], slopes[N_HEADS:]
    pos = np.arange(S)

    kca = np.zeros((S, AUG), np.float32)
    blk, r = pos // MOBA_BLOCK, pos % MOBA_BLOCK
    kca[pos, COL_ONEHOT + blk] = 1.0
    kca[:, COL_POS] = r
    kca[:, COL_POS + 1] = r
    kca[:, COL_POS + 2] = blk
    kca[:, COL_POS + 3] = blk
    kca = jnp.asarray(kca, BF16)

    c = slopes_a * np.float32(LOG2E)
    hi = c.astype(BF16).astype(np.float32)
    lo = c - hi
    rows = np.stack([hi, lo, hi * MOBA_BLOCK, lo * MOBA_BLOCK], axis=-1)
    rows = np.pad(rows, ((0, 0), (0, AUG - COL_POS - 4)))
    crow_a = jnp.asarray(np.broadcast_to(rows[:, :, None], rows.shape + (Q_TILE,)), BF16)

    nb = S // MOBA_BLOCK
    avg = jnp.asarray((pos[None, :] // MOBA_BLOCK == np.arange(nb)[:, None]) / MOBA_BLOCK, BF16)
    kq = np.arange(MOBA_BLOCK)
    causal = jnp.asarray(np.where(kq[:, None] <= kq[None, :], 0.0, NEG_BIG), F32)
    u = np.arange(2 * SWA_BLOCK)[:, None]
    t = np.arange(SWA_BLOCK)[None, :]
    ok = (u > t) & (u <= t + SWA_BLOCK)
    dist = (SWA_BLOCK + t - u).astype(np.float32)
    sc = np.asarray(slopes_c, np.float32) * np.float32(LOG2E)
    band = []
    for kv in range(C_KV_HEADS):
        bias = np.concatenate([-sc[2 * kv] * dist, -sc[2 * kv + 1] * dist], axis=1)
        ok2 = np.concatenate([ok, ok], axis=1)
        band.append(np.stack([np.where(ok2 & (u >= SWA_BLOCK), bias, NEG_BIG),
                              np.where(ok2, bias, NEG_BIG)]))
    band = jnp.asarray(np.stack(band), F32)
    w = np.repeat(np.asarray(POOL_WINDOWS, np.float32), 64)[None, :]
    first = 1.0 / np.minimum(np.arange(tm, dtype=np.float32)[:, None] + 1.0, w)
    invcnt = jnp.asarray(np.stack([first, np.broadcast_to(1.0 / w, (tm, 256))]), F32)
    return dict(kca=kca, crow_a=crow_a, avg=avg, causal=causal, band=band, invcnt=invcnt)


def kernel(x, norm_g, w_in, w_out, a_q_norm, a_k_norm, pool_w, pool_scale, c_q_norm, c_k_norm, c_sinks, conv_w):
    B, S, _ = x.shape
    depth = norm_g.shape[0]
    assert S % ROW_TILE == 0 and S // MOBA_BLOCK <= 32
    tb = _tables(S)
    tm = ROW_TILE
    d = np.arange(256)
    bd = jnp.asarray(d[:, None] // HEAD_DIM == d[None, :] // HEAD_DIM, BF16)
    qscale = QK_SCALE * LOG2E
    for l in range(depth):
        gq = jnp.broadcast_to(jnp.tile(a_q_norm[l] * qscale, N_HEADS)[:, None], (256, tm))
        gcq = jnp.broadcast_to(jnp.tile(c_q_norm[l] * qscale, N_HEADS)[:, None], (256, tm))
        gk = jnp.tile(a_k_norm[l], N_HEADS)[None, :]
        gck = jnp.tile(c_k_norm[l], C_KV_HEADS)[None, :]
        qT, kaug, vT, cqT, ck, cvT, rest = _in_proj(
            x, norm_g[l][None, :], w_in[l].astype(BF16), bd, gq, gk, gcq, gck, tb['kca'])
        ya = _moba(qT, kaug, vT, tb['avg'], tb['crow_a'], tb['causal'])
        sink = jnp.repeat((c_sinks[l] * LOG2E).reshape(C_KV_HEADS, 2), SWA_BLOCK, axis=1)[:, None, :]
        yc = _swa(cqT, ck, cvT, tb['band'], sink)
        pw = jnp.zeros((256, 256), F32)
        for g in range(4):
            pw = pw.at[g * 64:(g + 1) * 64, g * 64:(g + 1) * 64].set(pool_w[l, g])
        x = _out_proj(x, ya, yc, rest, tb['invcnt'], pw.astype(BF16), pool_scale[l][None, :],
                      conv_w[l], w_out[l].astype(BF16))
    return x
```

```python
import functools
import math

import jax
import jax.numpy as jnp
import numpy as np
from jax import lax
from jax.experimental import pallas as pl
from jax.experimental.pallas import tpu as pltpu

F32 = jnp.float32
BF16 = jnp.bfloat16

D_MODEL = 1024
HEAD_DIM = 64
GROUP_WIDTH = 256
N_HEADS = 4
C_KV_HEADS = 2
NORM_EPS = 1e-6
MOBA_BLOCK = 256
MOBA_TOPK = 3
SWA_BLOCK = 128
POOL_WINDOWS = (2, 4, 8, 16)
CONV_WIDTH = 3
HALO = 16

AUG = 128
COL_ONEHOT = 64
COL_POS = 96
NEG_BIG = -(2.0 ** 100)
LOG2E = math.log2(math.e)
QK_SCALE = HEAD_DIM ** -0.5

IN_TILE = 1024
ROW_TILE = 512
Q_TILE = 256
SWA_TILE = 512
VMEM_LIMIT = 56 * 1024 * 1024

SEG = dict(aq=0, ak=256, av=512, ag=768, bu=1024, bg=1280, cq=1536, ck=1792, cv=1920,
           cg=2048, dh=2304, db=2560, dc=2816, dg=3072)
IN_PROJ_WIDTH = 3328


def _dot(a, b):
    return jnp.dot(a, b, preferred_element_type=F32)


def _in_proj_kernel(x_ref, g_ref, w_ref, bd_ref, gq_ref, gk_ref, gcq_ref, gck_ref, kca_ref,
                    invcnt_ref, pw_ref, pscale_ref, cw_ref,
                    qT_ref, kaug_ref, vT_ref, cqT_ref, ck_ref, cvT_ref, gates_ref, ybd_ref,
                    hist_ref):
    tm = x_ref.shape[1]
    x = x_ref[0]
    ms = jnp.mean(x * x, axis=-1, keepdims=True)
    h = (x * lax.rsqrt(ms + NORM_EPS) * g_ref[...]).astype(BF16)

    def proj(lo, width):
        return _dot(h, w_ref[:, lo:lo + width])

    def norm_t(p, gain_ref):
        pt = p.T
        outs = []
        for hh in range(N_HEADS):
            ph = pt[hh * HEAD_DIM:(hh + 1) * HEAD_DIM]
            ss = jnp.sum(ph * ph, axis=0, keepdims=True) * (1.0 / HEAD_DIM)
            outs.append(ph * lax.rsqrt(ss + NORM_EPS))
        return (jnp.concatenate(outs, axis=0) * gain_ref[...]).astype(BF16)

    def norm_rows(p, gain_ref, width):
        ss = _dot((p * p).astype(BF16), bd_ref[0:width, 0:width]) * (1.0 / HEAD_DIM)
        return p * lax.rsqrt(ss + NORM_EPS) * gain_ref[...]

    def build_kaug(kn, kconst, n_heads, out_ref):
        lane = lax.broadcasted_iota(jnp.int32, (tm, AUG), 1)
        for hh in range(n_heads):
            col = kn[:, (hh // 2) * AUG:(hh // 2 + 1) * AUG]
            if hh % 2 == 1:
                col = pltpu.roll(col, HEAD_DIM, axis=1)
            out_ref[0, hh] = jnp.where(lane < HEAD_DIM, col, kconst).astype(BF16)


    @pl.when(pl.program_id(1) == 0)
    def _():
        hist_ref[...] = jnp.zeros(hist_ref.shape, F32)

    back = lambda v, k: pltpu.roll(v, k, axis=0)

    pd = proj(SEG['dh'], 1024)
    dh, db, dc, dg = (pd[:, k * 256:(k + 1) * 256] for k in range(4))
    u = dc * dh
    ue = jnp.concatenate([hist_ref[1], u], axis=0)
    hist_ref[1] = u[tm - HALO:]
    conv = (cw_ref[0:1, :] * back(ue, 2) + cw_ref[1:2, :] * back(ue, 1) + cw_ref[2:3, :] * ue)[HALO:]
    ybd_ref[0, :, 256:512] = (db * conv * _silu(dg)).astype(BF16)

    pb = proj(SEG['bu'], 512)
    bu, bg = pb[:, 0:256], pb[:, 256:512]
    e = jnp.concatenate([hist_ref[0], bu], axis=0)
    hist_ref[0] = bu[tm - HALO:]
    a2 = e + back(e, 1)
    a4 = a2 + back(a2, 2)
    a4r = a4[:, 128:256]
    a8 = a4r + back(a4r, 4)
    a16 = a8 + back(a8, 8)
    lane = lax.broadcasted_iota(jnp.int32, (HALO + tm, 128), 1)
    sums = jnp.concatenate([jnp.where(lane < 64, a2[:, 0:128], a4[:, 0:128]),
                            jnp.where(lane < 64, a8, a16)], axis=1)
    pooled = sums[HALO:] * invcnt_ref[0] - bu
    yb = _dot(pooled.astype(BF16), pw_ref[...]) * pscale_ref[...] * _silu(bg)
    ybd_ref[0, :, 0:256] = yb.astype(BF16)

    nblk = tm // Q_TILE
    qT = norm_t(proj(SEG['aq'], 256), gq_ref)
    for t in range(nblk):
        qT_ref[0, t] = qT[:, t * Q_TILE:(t + 1) * Q_TILE]
    kn = norm_rows(proj(SEG['ak'], 256), gk_ref, 256)
    build_kaug(kn, kca_ref[...].astype(F32), N_HEADS, kaug_ref)
    vT = proj(SEG['av'], 256).T.astype(BF16)
    for t in range(nblk):
        vT_ref[0, t] = vT[:, t * Q_TILE:(t + 1) * Q_TILE]
    cqT = norm_t(proj(SEG['cq'], 256), gcq_ref)
    for t in range(nblk):
        cqT_ref[0, t] = cqT[:, t * Q_TILE:(t + 1) * Q_TILE]
    ckv = proj(SEG['ck'], 256)
    ck_ref[0] = norm_rows(ckv[:, 0:128], gck_ref, 128).astype(BF16)
    cvT = ckv[:, 128:256].T.astype(BF16)
    for t in range(tm // SWA_BLOCK):
        cvT_ref[0, t] = cvT[:, t * SWA_BLOCK:(t + 1) * SWA_BLOCK]
    gates_ref[0, :, 0:256] = _silu(proj(SEG['ag'], 256)).astype(BF16)
    gates_ref[0, :, 256:512] = _silu(proj(SEG['cg'], 256)).astype(BF16)


def _in_proj(x, g, w, bd, gq, gk, gcq, gck, kca, invcnt, pw, pscale, cw):
    B, S, _ = x.shape
    tm = IN_TILE
    ns = S // tm
    nq = S // Q_TILE
    nkb = S // SWA_BLOCK
    const = lambda *shape: pl.BlockSpec(shape, lambda b, s: (0,) * len(shape))
    out_shape = (
        jax.ShapeDtypeStruct((B, nq, 256, Q_TILE), BF16),
        jax.ShapeDtypeStruct((B, N_HEADS, S, AUG), BF16),
        jax.ShapeDtypeStruct((B, nq, 256, Q_TILE), BF16),
        jax.ShapeDtypeStruct((B, nq, 256, Q_TILE), BF16),
        jax.ShapeDtypeStruct((B, S, 128), BF16),
        jax.ShapeDtypeStruct((B, nkb, 128, SWA_BLOCK), BF16),
        jax.ShapeDtypeStruct((B, S, 512), BF16),
        jax.ShapeDtypeStruct((B, S, 512), BF16),
    )
    r = tm // Q_TILE
    out_specs = (
        pl.BlockSpec((1, r, 256, Q_TILE), lambda b, s: (b, s, 0, 0)),
        pl.BlockSpec((1, N_HEADS, tm, AUG), lambda b, s: (b, 0, s, 0)),
        pl.BlockSpec((1, r, 256, Q_TILE), lambda b, s: (b, s, 0, 0)),
        pl.BlockSpec((1, r, 256, Q_TILE), lambda b, s: (b, s, 0, 0)),
        pl.BlockSpec((1, tm, 128), lambda b, s: (b, s, 0)),
        pl.BlockSpec((1, tm // SWA_BLOCK, 128, SWA_BLOCK), lambda b, s: (b, s, 0, 0)),
        pl.BlockSpec((1, tm, 512), lambda b, s: (b, s, 0)),
        pl.BlockSpec((1, tm, 512), lambda b, s: (b, s, 0)),
    )
    in_specs = [
        pl.BlockSpec((1, tm, D_MODEL), lambda b, s: (b, s, 0)),
        const(1, D_MODEL),
        const(D_MODEL, IN_PROJ_WIDTH),
        const(256, 256),
        const(256, tm), const(1, 256), const(256, tm), const(1, 128),
        pl.BlockSpec((tm, AUG), lambda b, s: (s, 0)),
        pl.BlockSpec((1, tm, 256), lambda b, s: (jnp.minimum(s, 1), 0, 0)),
        const(256, 256), const(1, 256), const(CONV_WIDTH, 256),
    ]
    return pl.pallas_call(
        _in_proj_kernel, grid=(B, ns), in_specs=in_specs, out_specs=out_specs, out_shape=out_shape,
        scratch_shapes=[pltpu.VMEM((2, HALO, 256), F32)],
        compiler_params=pltpu.CompilerParams(dimension_semantics=("parallel", "arbitrary"),
                                             vmem_limit_bytes=VMEM_LIMIT),
        name="in_proj",
    )(x, g, w, bd, gq, gk, gcq, gck, kca, invcnt, pw, pscale, cw)


def _moba_kernel(qT_ref, kaug_ref, vT_ref, avg_ref, crow_ref, causal_ref, out_ref,
                 kmean_ref, qaug_ref, acc_ref, s_ref, p_ref, mx_ref):
    i = pl.program_id(1)
    S = kaug_ref.shape[2]
    nb = S // MOBA_BLOCK
    chunk = 1024

    @pl.when(i == 0)
    def _():
        for hh in range(N_HEADS):
            acc = jnp.zeros((nb, AUG), F32)
            for c in range(S // chunk):
                acc = acc + _dot(avg_ref[:, c * chunk:(c + 1) * chunk],
                                 kaug_ref[0, hh, c * chunk:(c + 1) * chunk, :])
            kmean_ref[hh] = acc

    row = lax.broadcasted_iota(jnp.int32, (nb, Q_TILE), 0).astype(F32)
    i_f = i.astype(F32)
    neg_inf = jnp.float32(-jnp.inf)
    heads = range(N_HEADS)
    vh = lambda hh, j: vT_ref[0, j, hh * HEAD_DIM:(hh + 1) * HEAD_DIM, :]
    kb = lambda hh, j: kaug_ref[0, hh, pl.ds(pl.multiple_of(j * MOBA_BLOCK, MOBA_BLOCK), MOBA_BLOCK), :]

    for hh in heads:
        qh = qT_ref[0, 0, hh * HEAD_DIM:(hh + 1) * HEAD_DIM, :]
        km = kmean_ref[hh][:, 0:HEAD_DIM]
        km_hi = km.astype(BF16)
        km_lo = (km - km_hi.astype(F32)).astype(BF16)
        bs = _dot(km_hi, qh) + _dot(km_lo, qh)
        xs = jnp.where(row < i_f, bs, neg_inf)
        sel = row == i_f
        for _ in range(MOBA_TOPK):
            mx = jnp.max(xs, axis=0, keepdims=True)
            first = jnp.min(jnp.where(xs == mx, row, jnp.float32(nb)), axis=0, keepdims=True)
            pick = row == first
            sel = sel | (pick & (mx > neg_inf))
            xs = jnp.where(pick, neg_inf, xs)
        negmask = jnp.where(sel, 0.0, NEG_BIG).astype(BF16)
        pad = jnp.zeros((COL_POS - COL_ONEHOT - nb, Q_TILE), BF16) if nb < 32 else None
        parts = [qh, negmask] + ([pad] if pad is not None else []) + [crow_ref[hh]]
        qaug_ref[hh] = jnp.concatenate(parts, axis=0)

    def blk(pos):
        past = jnp.minimum(i + 1, nb - 1)
        return jnp.where(pos <= 0, i, jnp.where(pos <= i, pos - 1, past))

    kblk = MOBA_BLOCK
    sub = 8
    rows = 64

    fold8 = _fold8

    def scores(hh, half, s):
        s_ref[hh, half * kblk:(half + 1) * kblk] = s
        mx_ref[hh, half * sub:(half + 1) * sub] = fold8(s, jnp.maximum)

    for hh in heads:
        scores(hh, 0, _dot(kb(hh, i), qaug_ref[hh]) + causal_ref[...])
        scores(hh, 1, _dot(kb(hh, blk(1)), qaug_ref[hh]))
    p_ref[...] = jnp.zeros(p_ref.shape, BF16)
    acc_ref[...] = jnp.zeros(acc_ref.shape, F32)
    neg = jnp.full((1, Q_TILE), NEG_BIG, F32)
    one = jnp.ones((1, Q_TILE), F32)

    def values_step(t_prev, alphas):
        for hh in heads:
            acc_ref[hh] = (alphas[hh] * acc_ref[hh]
                           + _dot(vh(hh, blk(2 * t_prev)), p_ref[hh, 0:kblk])
                           + _dot(vh(hh, blk(2 * t_prev + 1)), p_ref[hh, kblk:2 * kblk]))

    def body(t, carry):
        ms, ls, alphas = carry
        values_step(t - 1, alphas)
        ms_new, ls_new, alphas_new = [], [], []
        for hh in heads:
            m_new = jnp.maximum(ms[hh], jnp.max(mx_ref[hh], axis=0, keepdims=True))
            alpha = jnp.exp2(ms[hh] - m_new)
            l8 = alpha * ls[hh]
            for c in range(2 * kblk // rows):
                p = jnp.exp2(s_ref[hh, c * rows:(c + 1) * rows] - m_new)
                p_ref[hh, c * rows:(c + 1) * rows] = p.astype(BF16)
                l8 = l8 + fold8(p, jnp.add)
            ms_new.append(m_new)
            ls_new.append(l8)
            alphas_new.append(alpha)
        for hh in heads:
            scores(hh, 0, _dot(kb(hh, blk(2 * t + 2)), qaug_ref[hh]))
            scores(hh, 1, _dot(kb(hh, blk(2 * t + 3)), qaug_ref[hh]))
        return tuple(ms_new), tuple(ls_new), tuple(alphas_new)

    zero8 = jnp.zeros((sub, Q_TILE), F32)
    init = ((neg,) * N_HEADS, (zero8,) * N_HEADS, (one,) * N_HEADS)
    steps = (i + 2) // 2
    ms, ls, alphas = lax.fori_loop(0, steps, body, init)
    values_step(steps - 1, alphas)
    outs = [acc_ref[hh] / jnp.sum(ls[hh], axis=0, keepdims=True) for hh in heads]
    out_ref[0] = jnp.concatenate(outs, axis=0).T.astype(BF16)


def _moba(qT, kaug, vT, avg, crow, causal):
    B, nq = qT.shape[0], qT.shape[1]
    S = kaug.shape[2]
    nb = S // MOBA_BLOCK
    in_specs = [
        pl.BlockSpec((1, 1, 256, Q_TILE), lambda b, i: (b, i, 0, 0)),
        pl.BlockSpec((1, N_HEADS, S, AUG), lambda b, i: (b, 0, 0, 0)),
        pl.BlockSpec((1, nq, 256, Q_TILE), lambda b, i: (b, 0, 0, 0)),
        pl.BlockSpec((nb, S), lambda b, i: (0, 0)),
        pl.BlockSpec((N_HEADS, 32, Q_TILE), lambda b, i: (0, 0, 0)),
        pl.BlockSpec((MOBA_BLOCK, Q_TILE), lambda b, i: (0, 0)),
    ]
    return pl.pallas_call(
        _moba_kernel, grid=(B, nq), in_specs=in_specs,
        out_specs=pl.BlockSpec((1, Q_TILE, 256), lambda b, i: (b, i, 0)),
        out_shape=jax.ShapeDtypeStruct((B, S, 256), BF16),
        scratch_shapes=[pltpu.VMEM((N_HEADS, nb, AUG), F32),
                        pltpu.VMEM((N_HEADS, AUG, Q_TILE), BF16),
                        pltpu.VMEM((N_HEADS, HEAD_DIM, Q_TILE), F32),
                        pltpu.VMEM((N_HEADS, 2 * MOBA_BLOCK, Q_TILE), F32),
                        pltpu.VMEM((N_HEADS, 2 * MOBA_BLOCK, Q_TILE), BF16),
                        pltpu.VMEM((N_HEADS, 16, Q_TILE), F32)],
        compiler_params=pltpu.CompilerParams(dimension_semantics=("parallel", "arbitrary"),
                                             vmem_limit_bytes=VMEM_LIMIT),
        name="moba",
    )(qT, kaug, vT, avg, crow, causal)


def _fold8(x, op):
    out = x[0:8]
    for g in range(1, x.shape[0] // 8):
        out = op(out, x[g * 8:(g + 1) * 8])
    return out


def _swa_kernel(cqT_ref, ck_ref, cvT_ref, band0_ref, band_ref, sink_ref, out_ref, s_ref, p_ref, mx_ref):
    i = pl.program_id(1)
    blocks = SWA_TILE // SWA_BLOCK
    win = 2 * SWA_BLOCK
    rows = 64
    zrows = jnp.zeros((HEAD_DIM, win), BF16)
    chains = [(t, kv) for t in range(blocks) for kv in range(C_KV_HEADS)]
    for t in range(blocks):
        n = i * blocks + t
        prev = jnp.maximum(n - 1, 0)
        kwin = jnp.concatenate([
            ck_ref[0, pl.ds(pl.multiple_of(prev * SWA_BLOCK, SWA_BLOCK), SWA_BLOCK), :],
            ck_ref[0, pl.ds(pl.multiple_of(n * SWA_BLOCK, SWA_BLOCK), SWA_BLOCK), :]], axis=0)
        for kv in range(C_KV_HEADS):
            c = chains.index((t, kv))
            q2 = jnp.concatenate([
                cqT_ref[0, t // 2, (2 * kv + g) * HEAD_DIM:(2 * kv + g + 1) * HEAD_DIM,
                        (t % 2) * SWA_BLOCK:(t % 2 + 1) * SWA_BLOCK] for g in range(2)], axis=1)
            qz = jnp.concatenate([q2, zrows] if kv == 0 else [zrows, q2], axis=0)
            band = band0_ref[kv, 0] if t == 0 else band_ref[kv, 0]
            s = _dot(kwin, qz) + band
            s_ref[c] = s
            mx_ref[c] = _fold8(s, jnp.maximum)
    inv = []
    for c, (t, kv) in enumerate(chains):
        m = jnp.max(mx_ref[c], axis=0, keepdims=True)
        l8 = jnp.zeros((8, win), F32)
        for r in range(win // rows):
            p = jnp.exp2(s_ref[c, r * rows:(r + 1) * rows] - m)
            p_ref[c, r * rows:(r + 1) * rows] = p.astype(BF16)
            l8 = l8 + _fold8(p, jnp.add)
        inv.append(1.0 / (jnp.sum(l8, axis=0, keepdims=True) + jnp.exp2(sink_ref[kv] - m)))
    outs = {}
    for c, (t, kv) in enumerate(chains):
        n = i * blocks + t
        prev = jnp.maximum(n - 1, 0)
        vwin = jnp.concatenate([cvT_ref[0, prev, kv * HEAD_DIM:(kv + 1) * HEAD_DIM, :],
                                cvT_ref[0, n, kv * HEAD_DIM:(kv + 1) * HEAD_DIM, :]], axis=1)
        outs[t, kv] = _dot(vwin, p_ref[c]) * inv[c]
    for u in range(SWA_TILE // Q_TILE):
        head_rows = []
        for kv in range(C_KV_HEADS):
            for g in range(2):
                head_rows.append(jnp.concatenate(
                    [outs[t, kv][:, g * SWA_BLOCK:(g + 1) * SWA_BLOCK] for t in (2 * u, 2 * u + 1)], axis=1))
        out_ref[0, u * Q_TILE:(u + 1) * Q_TILE, :] = jnp.concatenate(head_rows, axis=0).T.astype(BF16)


def _swa(cqT, ck, cvT, band, sink):
    B, nq = cqT.shape[0], cqT.shape[1]
    S = ck.shape[1]
    nkb = S // SWA_BLOCK
    r = SWA_TILE // Q_TILE
    nchain = (SWA_TILE // SWA_BLOCK) * C_KV_HEADS
    win = 2 * SWA_BLOCK
    in_specs = [
        pl.BlockSpec((1, r, 256, Q_TILE), lambda b, i: (b, i, 0, 0)),
        pl.BlockSpec((1, S, 128), lambda b, i: (b, 0, 0)),
        pl.BlockSpec((1, nkb, 128, SWA_BLOCK), lambda b, i: (b, 0, 0, 0)),
        pl.BlockSpec((C_KV_HEADS, 1, win, win), lambda b, i: (0, jnp.minimum(i, 1), 0, 0)),
        pl.BlockSpec((C_KV_HEADS, 1, win, win), lambda b, i: (0, 1, 0, 0)),
        pl.BlockSpec((C_KV_HEADS, 1, win), lambda b, i: (0, 0, 0)),
    ]
    return pl.pallas_call(
        _swa_kernel, grid=(B, S // SWA_TILE), in_specs=in_specs,
        out_specs=pl.BlockSpec((1, SWA_TILE, 256), lambda b, i: (b, i, 0)),
        out_shape=jax.ShapeDtypeStruct((B, S, 256), BF16),
        scratch_shapes=[pltpu.VMEM((nchain, win, win), F32),
                        pltpu.VMEM((nchain, win, win), BF16),
                        pltpu.VMEM((nchain, 8, win), F32)],
        compiler_params=pltpu.CompilerParams(dimension_semantics=("parallel", "parallel"),
                                             vmem_limit_bytes=VMEM_LIMIT),
        name="swa",
    )(cqT, ck, cvT, band, band, sink)


def _silu(v):
    h = 0.5 * v
    return h + h * jnp.tanh(h)


def _out_proj_kernel(x_ref, ya_ref, yc_ref, gates_ref, ybd_ref, wo_ref, out_ref):
    ya = (ya_ref[0].astype(F32) * gates_ref[0, :, 0:256].astype(F32)).astype(BF16)
    yc = (yc_ref[0].astype(F32) * gates_ref[0, :, 256:512].astype(F32)).astype(BF16)
    y = (_dot(ya, wo_ref[0:256, :]) + _dot(ybd_ref[0, :, 0:256], wo_ref[256:512, :])
         + _dot(yc, wo_ref[512:768, :]) + _dot(ybd_ref[0, :, 256:512], wo_ref[768:1024, :]))
    out_ref[0] = x_ref[0] + y


def _out_proj(x, ya, yc, gates, ybd, wo):
    B, S, _ = x.shape
    tm = ROW_TILE
    rows = lambda width: pl.BlockSpec((1, tm, width), lambda b, s: (b, s, 0))
    in_specs = [rows(D_MODEL), rows(256), rows(256), rows(512), rows(512),
                pl.BlockSpec((D_MODEL, D_MODEL), lambda b, s: (0, 0))]
    return pl.pallas_call(
        _out_proj_kernel, grid=(B, S // tm), in_specs=in_specs,
        out_specs=rows(D_MODEL),
        out_shape=jax.ShapeDtypeStruct((B, S, D_MODEL), F32),
        compiler_params=pltpu.CompilerParams(dimension_semantics=("parallel", "parallel"),
                                             vmem_limit_bytes=VMEM_LIMIT),
        name="out_proj",
    )(x, ya, yc, gates, ybd, wo)


def _tables(S):
    tm = IN_TILE
    slopes = np.exp2(-(8.0 / (2 * N_HEADS)) * np.arange(1, 2 * N_HEADS + 1)).astype(np.float32)
    slopes_c, slopes_a = slopes[:N_HEADS], slopes[N_HEADS:]
    pos = np.arange(S)

    kca = np.zeros((S, AUG), np.float32)
    blk, r = pos // MOBA_BLOCK, pos % MOBA_BLOCK
    kca[pos, COL_ONEHOT + blk] = 1.0
    kca[:, COL_POS] = r
    kca[:, COL_POS + 1] = r
    kca[:, COL_POS + 2] = blk
    kca[:, COL_POS + 3] = blk
    kca = jnp.asarray(kca, BF16)

    c = slopes_a * np.float32(LOG2E)
    hi = c.astype(BF16).astype(np.float32)
    lo = c - hi
    rows = np.stack([hi, lo, hi * MOBA_BLOCK, lo * MOBA_BLOCK], axis=-1)
    rows = np.pad(rows, ((0, 0), (0, AUG - COL_POS - 4)))
    crow_a = jnp.asarray(np.broadcast_to(rows[:, :, None], rows.shape + (Q_TILE,)), BF16)

    nb = S // MOBA_BLOCK
    avg = jnp.asarray((pos[None, :] // MOBA_BLOCK == np.arange(nb)[:, None]) / MOBA_BLOCK, BF16)
    kq = np.arange(MOBA_BLOCK)
    causal = jnp.asarray(np.where(kq[:, None] <= kq[None, :], 0.0, NEG_BIG), F32)
    u = np.arange(2 * SWA_BLOCK)[:, None]
    t = np.arange(SWA_BLOCK)[None, :]
    ok = (u > t) & (u <= t + SWA_BLOCK)
    dist = (SWA_BLOCK + t - u).astype(np.float32)
    sc = np.asarray(slopes_c, np.float32) * np.float32(LOG2E)
    band = []
    for kv in range(C_KV_HEADS):
        bias = np.concatenate([-sc[2 * kv] * dist, -sc[2 * kv + 1] * dist], axis=1)
        ok2 = np.concatenate([ok, ok], axis=1)
        band.append(np.stack([np.where(ok2 & (u >= SWA_BLOCK), bias, NEG_BIG),
                              np.where(ok2, bias, NEG_BIG)]))
    band = jnp.asarray(np.stack(band), F32)
    w = np.repeat(np.asarray(POOL_WINDOWS, np.float32), 64)[None, :]
    first = 1.0 / np.minimum(np.arange(tm, dtype=np.float32)[:, None] + 1.0, w)
    invcnt = jnp.asarray(np.stack([first, np.broadcast_to(1.0 / w, (tm, 256))]), F32)
    return dict(kca=kca, crow_a=crow_a, avg=avg, causal=causal, band=band, invcnt=invcnt)


def kernel(x, norm_g, w_in, w_out, a_q_norm, a_k_norm, pool_w, pool_scale, c_q_norm, c_k_norm, c_sinks, conv_w):
    B, S, _ = x.shape
    depth = norm_g.shape[0]
    assert S % max(ROW_TILE, IN_TILE, SWA_TILE) == 0 and S // MOBA_BLOCK <= 32
    tb = _tables(S)
    tm = IN_TILE
    d = np.arange(256)
    bd = jnp.asarray(d[:, None] // HEAD_DIM == d[None, :] // HEAD_DIM, BF16)
    qscale = QK_SCALE * LOG2E
    for l in range(depth):
        gq = jnp.broadcast_to(jnp.tile(a_q_norm[l] * qscale, N_HEADS)[:, None], (256, tm))
        gcq = jnp.broadcast_to(jnp.tile(c_q_norm[l] * qscale, N_HEADS)[:, None], (256, tm))
        gk = jnp.tile(a_k_norm[l], N_HEADS)[None, :]
        gck = jnp.tile(c_k_norm[l], C_KV_HEADS)[None, :]
        pw = jnp.zeros((256, 256), F32)
        for g in range(4):
            pw = pw.at[g * 64:(g + 1) * 64, g * 64:(g + 1) * 64].set(pool_w[l, g])
        qT, kaug, vT, cqT, ck, cvT, gates, ybd = _in_proj(
            x, norm_g[l][None, :], w_in[l].astype(BF16), bd, gq, gk, gcq, gck, tb['kca'],
            tb['invcnt'], pw.astype(BF16), pool_scale[l][None, :], conv_w[l])
        ya = _moba(qT, kaug, vT, tb['avg'], tb['crow_a'], tb['causal'])
        sink = jnp.repeat((c_sinks[l] * LOG2E).reshape(C_KV_HEADS, 2), SWA_BLOCK, axis=1)[:, None, :]
        yc = _swa(cqT, ck, cvT, tb['band'], sink)
        x = _out_proj(x, ya, yc, gates, ybd, w_out[l].astype(BF16))
    return x
```

```python
import math

import jax
import jax.numpy as jnp
import numpy as np
from jax import lax
from jax.experimental import pallas as pl
from jax.experimental.pallas import tpu as pltpu

F32 = jnp.float32
BF16 = jnp.bfloat16

D_MODEL = 1024
HEAD_DIM = 64
N_HEADS = 4
C_KV_HEADS = 2
NORM_EPS = 1e-6
MOBA_BLOCK = 256
MOBA_TOPK = 3
SWA_BLOCK = 128
POOL_WINDOWS = (2, 4, 8, 16)
CONV_WIDTH = 3
HALO = 16

AUG = 128
COL_ONEHOT = 64
COL_POS = 96
NEG_BIG = -(2.0 ** 100)
LOG2E = math.log2(math.e)
QK_SCALE = HEAD_DIM ** -0.5
SLOPES = np.exp2(-(8.0 / (2 * N_HEADS)) * np.arange(1, 2 * N_HEADS + 1)).astype(np.float32)
SLOPES_C, SLOPES_A = SLOPES[:N_HEADS], SLOPES[N_HEADS:]

IN_TILE = 1024
ROW_TILE = 1024
Q_TILE = 256
SWA_TILE = 512
VMEM_LIMIT = 56 * 1024 * 1024

SEG = dict(aq=0, ak=256, av=512, ag=768, bu=1024, bg=1280, cq=1536, ck=1792, cv=1920,
           cg=2048, dh=2304, db=2560, dc=2816, dg=3072)
IN_PROJ_WIDTH = 3328


def _dot(a, b):
    return jnp.dot(a, b, preferred_element_type=F32)


def _silu(v):
    h = 0.5 * v
    return h + h * jnp.tanh(h)


def _fold8(x, op):
    out = x[0:8]
    for g in range(1, x.shape[0] // 8):
        out = op(out, x[g * 8:(g + 1) * 8])
    return out


def _in_proj_kernel(x_ref, g_ref, w_ref, bd_ref, gq_ref, gk_ref, gcq_ref, gck_ref, kca_ref,
                    invcnt_ref, pw_ref, pscale_ref, cw_ref,
                    qT_ref, kaug_ref, vT_ref, cqT_ref, ck_ref, cvT_ref, gates_ref, ybd_ref,
                    hist_ref):
    tm = x_ref.shape[1]
    s_idx = pl.program_id(1)
    x = x_ref[0]
    ms = jnp.mean(x * x, axis=-1, keepdims=True)
    h = (x * lax.rsqrt(ms + NORM_EPS) * g_ref[...]).astype(BF16)

    def proj(lo, width):
        return _dot(h, w_ref[:, lo:lo + width])

    def norm_t(p, gain_ref):
        pt = p.T
        outs = []
        for hh in range(N_HEADS):
            ph = pt[hh * HEAD_DIM:(hh + 1) * HEAD_DIM]
            ss = jnp.sum(ph * ph, axis=0, keepdims=True) * (1.0 / HEAD_DIM)
            outs.append(ph * lax.rsqrt(ss + NORM_EPS))
        return (jnp.concatenate(outs, axis=0) * gain_ref[...]).astype(BF16)

    def norm_rows(p, gain_ref, width):
        ss = _dot((p * p).astype(BF16), bd_ref[0:width, 0:width]) * (1.0 / HEAD_DIM)
        return p * lax.rsqrt(ss + NORM_EPS) * gain_ref[...]

    def build_kaug(kn, kconst):
        lane = lax.broadcasted_iota(jnp.int32, (tm, AUG), 1)
        blocks = []
        for hh in range(N_HEADS):
            col = kn[:, (hh // 2) * AUG:(hh // 2 + 1) * AUG]
            if hh % 2 == 1:
                col = pltpu.roll(col, HEAD_DIM, axis=1)
            blocks.append(jnp.where(lane < HEAD_DIM, col, kconst).astype(BF16))
        return blocks


    @pl.when(s_idx == 0)
    def _():
        hist_ref[...] = jnp.zeros(hist_ref.shape, F32)

    back = lambda v, k: pltpu.roll(v, k, axis=0)

    pd = proj(SEG['dh'], 1024)
    dh, db, dc, dg = (pd[:, k * 256:(k + 1) * 256] for k in range(4))
    u = dc * dh
    ue = jnp.concatenate([hist_ref[1], u], axis=0)
    hist_ref[1] = u[tm - HALO:]
    conv = (cw_ref[0:1, :] * back(ue, 2) + cw_ref[1:2, :] * back(ue, 1) + cw_ref[2:3, :] * ue)[HALO:]
    ybd_ref[0, :, 256:512] = (db * conv * _silu(dg)).astype(BF16)

    pb = proj(SEG['bu'], 512)
    bu, bg = pb[:, 0:256], pb[:, 256:512]
    e = jnp.concatenate([hist_ref[0], bu], axis=0)
    hist_ref[0] = bu[tm - HALO:]
    a2 = e + back(e, 1)
    a4 = a2 + back(a2, 2)
    a4r = a4[:, 128:256]
    a8 = a4r + back(a4r, 4)
    a16 = a8 + back(a8, 8)
    lane = lax.broadcasted_iota(jnp.int32, (HALO + tm, 128), 1)
    sums = jnp.concatenate([jnp.where(lane < 64, a2[:, 0:128], a4[:, 0:128]),
                            jnp.where(lane < 64, a8, a16)], axis=1)
    pooled = sums[HALO:] * invcnt_ref[0] - bu
    yb = _dot(pooled.astype(BF16), pw_ref[...]) * pscale_ref[...] * _silu(bg)
    ybd_ref[0, :, 0:256] = yb.astype(BF16)

    nblk = tm // Q_TILE
    qT = norm_t(proj(SEG['aq'], 256), gq_ref)
    for t in range(nblk):
        qT_ref[0, t] = qT[:, t * Q_TILE:(t + 1) * Q_TILE]
    kn = norm_rows(proj(SEG['ak'], 256), gk_ref, 256)
    kaug = build_kaug(kn, kca_ref[...].astype(F32))
    for hh in range(N_HEADS):
        kaug_ref[0, hh] = kaug[hh]
    vT = proj(SEG['av'], 256).T.astype(BF16)
    for t in range(nblk):
        vT_ref[0, t] = vT[:, t * Q_TILE:(t + 1) * Q_TILE]
    cqT = norm_t(proj(SEG['cq'], 256), gcq_ref)
    for t in range(nblk):
        cqT_ref[0, t] = cqT[:, t * Q_TILE:(t + 1) * Q_TILE]
    ckv = proj(SEG['ck'], 256)
    ck_ref[0] = norm_rows(ckv[:, 0:128], gck_ref, 128).astype(BF16)
    cvT = ckv[:, 128:256].T.astype(BF16)
    for t in range(tm // SWA_BLOCK):
        cvT_ref[0, t] = cvT[:, t * SWA_BLOCK:(t + 1) * SWA_BLOCK]
    gates_ref[0, :, 0:256] = _silu(proj(SEG['ag'], 256)).astype(BF16)
    gates_ref[0, :, 256:512] = _silu(proj(SEG['cg'], 256)).astype(BF16)


def _in_proj(x, g, w, bd, gq, gk, gcq, gck, kca, invcnt, pw, pscale, cw):
    B, S, _ = x.shape
    tm = IN_TILE
    ns = S // tm
    nq = S // Q_TILE
    nkb = S // SWA_BLOCK
    const = lambda *shape: pl.BlockSpec(shape, lambda b, s: (0,) * len(shape))
    out_shape = (
        jax.ShapeDtypeStruct((B, nq, 256, Q_TILE), BF16),
        jax.ShapeDtypeStruct((B, N_HEADS, S, AUG), BF16),
        jax.ShapeDtypeStruct((B, nq, 256, Q_TILE), BF16),
        jax.ShapeDtypeStruct((B, nq, 256, Q_TILE), BF16),
        jax.ShapeDtypeStruct((B, S, 128), BF16),
        jax.ShapeDtypeStruct((B, nkb, 128, SWA_BLOCK), BF16),
        jax.ShapeDtypeStruct((B, S, 512), BF16),
        jax.ShapeDtypeStruct((B, S, 512), BF16),
    )
    r = tm // Q_TILE
    out_specs = (
        pl.BlockSpec((1, r, 256, Q_TILE), lambda b, s: (b, s, 0, 0)),
        pl.BlockSpec((1, N_HEADS, tm, AUG), lambda b, s: (b, 0, s, 0)),
        pl.BlockSpec((1, r, 256, Q_TILE), lambda b, s: (b, s, 0, 0)),
        pl.BlockSpec((1, r, 256, Q_TILE), lambda b, s: (b, s, 0, 0)),
        pl.BlockSpec((1, tm, 128), lambda b, s: (b, s, 0)),
        pl.BlockSpec((1, tm // SWA_BLOCK, 128, SWA_BLOCK), lambda b, s: (b, s, 0, 0)),
        pl.BlockSpec((1, tm, 512), lambda b, s: (b, s, 0)),
        pl.BlockSpec((1, tm, 512), lambda b, s: (b, s, 0)),
    )
    in_specs = [
        pl.BlockSpec((1, tm, D_MODEL), lambda b, s: (b, s, 0)),
        const(1, D_MODEL),
        const(D_MODEL, IN_PROJ_WIDTH),
        const(256, 256),
        const(256, tm), const(1, 256), const(256, tm), const(1, 128),
        pl.BlockSpec((tm, AUG), lambda b, s: (s, 0)),
        pl.BlockSpec((1, tm, 256), lambda b, s: (jnp.minimum(s, 1), 0, 0)),
        const(256, 256), const(1, 256), const(CONV_WIDTH, 256),
    ]
    return pl.pallas_call(
        _in_proj_kernel, grid=(B, ns), in_specs=in_specs, out_specs=out_specs, out_shape=out_shape,
        scratch_shapes=[pltpu.VMEM((2, HALO, 256), F32)],
        compiler_params=pltpu.CompilerParams(dimension_semantics=("parallel", "arbitrary"),
                                             vmem_limit_bytes=VMEM_LIMIT),
        name="in_proj",
    )(x, g, w, bd, gq, gk, gcq, gck, kca, invcnt, pw, pscale, cw)


def _moba_kernel(qT_ref, kaug_ref, vT_ref, avg_ref, crow_ref, causal_ref, gate_ref, out_ref,
                 kmean_ref, qaug_ref, acc_ref, s_ref, p_ref, mx_ref):
    i = pl.program_id(1)
    S = kaug_ref.shape[2]
    nb = S // MOBA_BLOCK
    chunk = 1024

    @pl.when(i == 0)
    def _():
        for hh in range(N_HEADS):
            acc = jnp.zeros((nb, AUG), F32)
            for c in range(S // chunk):
                acc = acc + _dot(avg_ref[:, c * chunk:(c + 1) * chunk],
                                 kaug_ref[0, hh, c * chunk:(c + 1) * chunk, :])
            kmean_ref[hh] = acc

    row = lax.broadcasted_iota(jnp.int32, (nb, Q_TILE), 0).astype(F32)
    i_f = i.astype(F32)
    neg_inf = jnp.float32(-jnp.inf)
    heads = range(N_HEADS)
    vh = lambda hh, j: vT_ref[0, j, hh * HEAD_DIM:(hh + 1) * HEAD_DIM, :]
    kb = lambda hh, j: kaug_ref[0, hh, pl.ds(pl.multiple_of(j * MOBA_BLOCK, MOBA_BLOCK), MOBA_BLOCK), :]

    for hh in heads:
        qh = qT_ref[0, 0, hh * HEAD_DIM:(hh + 1) * HEAD_DIM, :]
        km = kmean_ref[hh][:, 0:HEAD_DIM]
        km_hi = km.astype(BF16)
        km_lo = (km - km_hi.astype(F32)).astype(BF16)
        bs = _dot(km_hi, qh) + _dot(km_lo, qh)
        xs = jnp.where(row < i_f, bs, neg_inf)
        sel = row == i_f
        for _ in range(MOBA_TOPK):
            mx = jnp.max(xs, axis=0, keepdims=True)
            first = jnp.min(jnp.where(xs == mx, row, jnp.float32(nb)), axis=0, keepdims=True)
            pick = row == first
            sel = sel | (pick & (mx > neg_inf))
            xs = jnp.where(pick, neg_inf, xs)
        negmask = jnp.where(sel, 0.0, NEG_BIG).astype(BF16)
        pad = jnp.zeros((COL_POS - COL_ONEHOT - nb, Q_TILE), BF16) if nb < 32 else None
        parts = [qh, negmask] + ([pad] if pad is not None else []) + [crow_ref[hh]]
        qaug_ref[hh] = jnp.concatenate(parts, axis=0)

    def blk(pos):
        past = jnp.minimum(i + 1, nb - 1)
        return jnp.where(pos <= 0, i, jnp.where(pos <= i, pos - 1, past))

    kblk = MOBA_BLOCK
    sub = 8
    rows = 64

    def scores(hh, half, s):
        s_ref[hh, half * kblk:(half + 1) * kblk] = s
        mx_ref[hh, half * sub:(half + 1) * sub] = _fold8(s, jnp.maximum)

    for hh in heads:
        scores(hh, 0, _dot(kb(hh, i), qaug_ref[hh]) + causal_ref[...])
        scores(hh, 1, _dot(kb(hh, blk(1)), qaug_ref[hh]))
    p_ref[...] = jnp.zeros(p_ref.shape, BF16)
    acc_ref[...] = jnp.zeros(acc_ref.shape, F32)
    neg = jnp.full((1, Q_TILE), NEG_BIG, F32)
    one = jnp.ones((1, Q_TILE), F32)

    def values_step(t_prev, alphas):
        for hh in heads:
            acc_ref[hh] = (alphas[hh] * acc_ref[hh]
                           + _dot(vh(hh, blk(2 * t_prev)), p_ref[hh, 0:kblk])
                           + _dot(vh(hh, blk(2 * t_prev + 1)), p_ref[hh, kblk:2 * kblk]))

    def body(t, carry):
        ms, ls, alphas = carry
        values_step(t - 1, alphas)
        ms_new, ls_new, alphas_new = [], [], []
        for hh in heads:
            m_new = jnp.maximum(ms[hh], jnp.max(mx_ref[hh], axis=0, keepdims=True))
            alpha = jnp.exp2(ms[hh] - m_new)
            l8 = alpha * ls[hh]
            for c in range(2 * kblk // rows):
                p = jnp.exp2(s_ref[hh, c * rows:(c + 1) * rows] - m_new)
                p_ref[hh, c * rows:(c + 1) * rows] = p.astype(BF16)
                l8 = l8 + _fold8(p, jnp.add)
            ms_new.append(m_new)
            ls_new.append(l8)
            alphas_new.append(alpha)
        for hh in heads:
            scores(hh, 0, _dot(kb(hh, blk(2 * t + 2)), qaug_ref[hh]))
            scores(hh, 1, _dot(kb(hh, blk(2 * t + 3)), qaug_ref[hh]))
        return tuple(ms_new), tuple(ls_new), tuple(alphas_new)

    zero8 = jnp.zeros((sub, Q_TILE), F32)
    init = ((neg,) * N_HEADS, (zero8,) * N_HEADS, (one,) * N_HEADS)
    steps = (i + 2) // 2
    ms, ls, alphas = lax.fori_loop(0, steps, body, init)
    values_step(steps - 1, alphas)
    outs = [acc_ref[hh] / jnp.sum(ls[hh], axis=0, keepdims=True) for hh in heads]
    o = jnp.concatenate(outs, axis=0).T
    out_ref[0] = (o * gate_ref[0].astype(F32)).astype(BF16)


def _moba(qT, kaug, vT, avg, crow, causal, gates):
    B, nq = qT.shape[0], qT.shape[1]
    S = kaug.shape[2]
    nb = S // MOBA_BLOCK
    in_specs = [
        pl.BlockSpec((1, 1, 256, Q_TILE), lambda b, i: (b, i, 0, 0)),
        pl.BlockSpec((1, N_HEADS, S, AUG), lambda b, i: (b, 0, 0, 0)),
        pl.BlockSpec((1, nq, 256, Q_TILE), lambda b, i: (b, 0, 0, 0)),
        pl.BlockSpec((nb, S), lambda b, i: (0, 0)),
        pl.BlockSpec((N_HEADS, AUG - COL_POS, Q_TILE), lambda b, i: (0, 0, 0)),
        pl.BlockSpec((MOBA_BLOCK, Q_TILE), lambda b, i: (0, 0)),
        pl.BlockSpec((1, Q_TILE, 256), lambda b, i: (b, i, 0)),
    ]
    return pl.pallas_call(
        _moba_kernel, grid=(B, nq), in_specs=in_specs,
        out_specs=pl.BlockSpec((1, Q_TILE, 256), lambda b, i: (b, i, 0)),
        out_shape=jax.ShapeDtypeStruct((B, S, 256), BF16),
        scratch_shapes=[pltpu.VMEM((N_HEADS, nb, AUG), F32),
                        pltpu.VMEM((N_HEADS, AUG, Q_TILE), BF16),
                        pltpu.VMEM((N_HEADS, HEAD_DIM, Q_TILE), F32),
                        pltpu.VMEM((N_HEADS, 2 * MOBA_BLOCK, Q_TILE), F32),
                        pltpu.VMEM((N_HEADS, 2 * MOBA_BLOCK, Q_TILE), BF16),
                        pltpu.VMEM((N_HEADS, 16, Q_TILE), F32)],
        compiler_params=pltpu.CompilerParams(dimension_semantics=("parallel", "arbitrary"),
                                             vmem_limit_bytes=VMEM_LIMIT),
        name="moba",
    )(qT, kaug, vT, avg, crow, causal, gates)


def _swa_kernel(cqT_ref, ck_ref, cvT_ref, band0_ref, band_ref, sink_ref, gate_ref, out_ref,
                s_ref, p_ref, mx_ref):
    i = pl.program_id(1)
    blocks = SWA_TILE // SWA_BLOCK
    win = 2 * SWA_BLOCK
    rows = 64
    zrows = jnp.zeros((HEAD_DIM, win), BF16)
    chains = [(t, kv) for t in range(blocks) for kv in range(C_KV_HEADS)]
    for t in range(blocks):
        n = i * blocks + t
        prev = jnp.maximum(n - 1, 0)
        kwin = jnp.concatenate([
            ck_ref[0, pl.ds(pl.multiple_of(prev * SWA_BLOCK, SWA_BLOCK), SWA_BLOCK), :],
            ck_ref[0, pl.ds(pl.multiple_of(n * SWA_BLOCK, SWA_BLOCK), SWA_BLOCK), :]], axis=0)
        for kv in range(C_KV_HEADS):
            c = chains.index((t, kv))
            q2 = jnp.concatenate([
                cqT_ref[0, t // 2, (2 * kv + g) * HEAD_DIM:(2 * kv + g + 1) * HEAD_DIM,
                        (t % 2) * SWA_BLOCK:(t % 2 + 1) * SWA_BLOCK] for g in range(2)], axis=1)
            qz = jnp.concatenate([q2, zrows] if kv == 0 else [zrows, q2], axis=0)
            band = band0_ref[kv, 0] if t == 0 else band_ref[kv, 0]
            s = _dot(kwin, qz) + band
            s_ref[c] = s
            mx_ref[c] = _fold8(s, jnp.maximum)
    inv = []
    for c, (t, kv) in enumerate(chains):
        m = jnp.max(mx_ref[c], axis=0, keepdims=True)
        l8 = jnp.zeros((8, win), F32)
        for r in range(win // rows):
            p = jnp.exp2(s_ref[c, r * rows:(r + 1) * rows] - m)
            p_ref[c, r * rows:(r + 1) * rows] = p.astype(BF16)
            l8 = l8 + _fold8(p, jnp.add)
        inv.append(1.0 / (jnp.sum(l8, axis=0, keepdims=True) + jnp.exp2(sink_ref[kv] - m)))
    outs = {}
    for c, (t, kv) in enumerate(chains):
        n = i * blocks + t
        prev = jnp.maximum(n - 1, 0)
        vwin = jnp.concatenate([cvT_ref[0, prev, kv * HEAD_DIM:(kv + 1) * HEAD_DIM, :],
                                cvT_ref[0, n, kv * HEAD_DIM:(kv + 1) * HEAD_DIM, :]], axis=1)
        outs[t, kv] = _dot(vwin, p_ref[c]) * inv[c]
    for u in range(SWA_TILE // Q_TILE):
        head_rows = []
        for kv in range(C_KV_HEADS):
            for g in range(2):
                head_rows.append(jnp.concatenate(
                    [outs[t, kv][:, g * SWA_BLOCK:(g + 1) * SWA_BLOCK] for t in (2 * u, 2 * u + 1)], axis=1))
        o = jnp.concatenate(head_rows, axis=0).T
        gate = gate_ref[0, u * Q_TILE:(u + 1) * Q_TILE, :].astype(F32)
        out_ref[0, u * Q_TILE:(u + 1) * Q_TILE, :] = (o * gate).astype(BF16)


def _swa(cqT, ck, cvT, band, sink, gates):
    B = cqT.shape[0]
    S = ck.shape[1]
    nkb = S // SWA_BLOCK
    r = SWA_TILE // Q_TILE
    nchain = (SWA_TILE // SWA_BLOCK) * C_KV_HEADS
    win = 2 * SWA_BLOCK
    in_specs = [
        pl.BlockSpec((1, r, 256, Q_TILE), lambda b, i: (b, i, 0, 0)),
        pl.BlockSpec((1, S, 128), lambda b, i: (b, 0, 0)),
        pl.BlockSpec((1, nkb, 128, SWA_BLOCK), lambda b, i: (b, 0, 0, 0)),
        pl.BlockSpec((C_KV_HEADS, 1, win, win), lambda b, i: (0, jnp.minimum(i, 1), 0, 0)),
        pl.BlockSpec((C_KV_HEADS, 1, win, win), lambda b, i: (0, 1, 0, 0)),
        pl.BlockSpec((C_KV_HEADS, 1, win), lambda b, i: (0, 0, 0)),
        pl.BlockSpec((1, SWA_TILE, 256), lambda b, i: (b, i, 1)),
    ]
    return pl.pallas_call(
        _swa_kernel, grid=(B, S // SWA_TILE), in_specs=in_specs,
        out_specs=pl.BlockSpec((1, SWA_TILE, 256), lambda b, i: (b, i, 0)),
        out_shape=jax.ShapeDtypeStruct((B, S, 256), BF16),
        scratch_shapes=[pltpu.VMEM((nchain, win, win), F32),
                        pltpu.VMEM((nchain, win, win), BF16),
                        pltpu.VMEM((nchain, 8, win), F32)],
        compiler_params=pltpu.CompilerParams(dimension_semantics=("parallel", "parallel"),
                                             vmem_limit_bytes=VMEM_LIMIT),
        name="swa",
    )(cqT, ck, cvT, band, band, sink, gates)


def _out_proj_kernel(x_ref, ya_ref, yc_ref, ybd_ref, wo_ref, out_ref):
    y = (_dot(ya_ref[0], wo_ref[0:256, :]) + _dot(ybd_ref[0, :, 0:256], wo_ref[256:512, :])
         + _dot(yc_ref[0], wo_ref[512:768, :]) + _dot(ybd_ref[0, :, 256:512], wo_ref[768:1024, :]))
    out_ref[0] = x_ref[0] + y


def _out_proj(x, ya, yc, ybd, wo):
    B, S, _ = x.shape
    tm = ROW_TILE
    rows = lambda width: pl.BlockSpec((1, tm, width), lambda b, s: (b, s, 0))
    in_specs = [rows(D_MODEL), rows(256), rows(256), rows(512),
                pl.BlockSpec((D_MODEL, D_MODEL), lambda b, s: (0, 0))]
    return pl.pallas_call(
        _out_proj_kernel, grid=(B, S // tm), in_specs=in_specs,
        out_specs=rows(D_MODEL),
        out_shape=jax.ShapeDtypeStruct((B, S, D_MODEL), F32),
        compiler_params=pltpu.CompilerParams(dimension_semantics=("parallel", "parallel"),
                                             vmem_limit_bytes=VMEM_LIMIT),
        name="out_proj",
    )(x, ya, yc, ybd, wo)


def _tables(S):
    tm = IN_TILE
    slopes_c, slopes_a = SLOPES_C, SLOPES_A
    pos = np.arange(S)

    kca = np.zeros((S, AUG), np.float32)
    blk, r = pos // MOBA_BLOCK, pos % MOBA_BLOCK
    kca[pos, COL_ONEHOT + blk] = 1.0
    kca[:, COL_POS] = r
    kca[:, COL_POS + 1] = r
    kca[:, COL_POS + 2] = blk
    kca[:, COL_POS + 3] = blk
    kca = jnp.asarray(kca, BF16)

    c = slopes_a * np.float32(LOG2E)
    hi = c.astype(BF16).astype(np.float32)
    lo = c - hi
    rows = np.stack([hi, lo, hi * MOBA_BLOCK, lo * MOBA_BLOCK], axis=-1)
    rows = np.pad(rows, ((0, 0), (0, AUG - COL_POS - 4)))
    crow_a = jnp.asarray(np.broadcast_to(rows[:, :, None], rows.shape + (Q_TILE,)), BF16)

    nb = S // MOBA_BLOCK
    avg = jnp.asarray((pos[None, :] // MOBA_BLOCK == np.arange(nb)[:, None]) / MOBA_BLOCK, BF16)
    kq = np.arange(MOBA_BLOCK)
    causal = jnp.asarray(np.where(kq[:, None] <= kq[None, :], 0.0, NEG_BIG), F32)
    u = np.arange(2 * SWA_BLOCK)[:, None]
    t = np.arange(SWA_BLOCK)[None, :]
    ok = (u > t) & (u <= t + SWA_BLOCK)
    dist = (SWA_BLOCK + t - u).astype(np.float32)
    sc = np.asarray(slopes_c, np.float32) * np.float32(LOG2E)
    band = []
    for kv in range(C_KV_HEADS):
        bias = np.concatenate([-sc[2 * kv] * dist, -sc[2 * kv + 1] * dist], axis=1)
        ok2 = np.concatenate([ok, ok], axis=1)
        band.append(np.stack([np.where(ok2 & (u >= SWA_BLOCK), bias, NEG_BIG),
                              np.where(ok2, bias, NEG_BIG)]))
    band = jnp.asarray(np.stack(band), F32)
    w = np.repeat(np.asarray(POOL_WINDOWS, np.float32), 64)[None, :]
    first = 1.0 / np.minimum(np.arange(tm, dtype=np.float32)[:, None] + 1.0, w)
    invcnt = jnp.asarray(np.stack([first, np.broadcast_to(1.0 / w, (tm, 256))]), F32)
    return dict(kca=kca, crow_a=crow_a, avg=avg, causal=causal, band=band, invcnt=invcnt)


def kernel(x, norm_g, w_in, w_out, a_q_norm, a_k_norm, pool_w, pool_scale, c_q_norm, c_k_norm, c_sinks, conv_w):
    B, S, _ = x.shape
    depth = norm_g.shape[0]
    assert S % max(ROW_TILE, IN_TILE, SWA_TILE) == 0 and S // MOBA_BLOCK <= 32
    tb = _tables(S)
    tm = IN_TILE
    d = np.arange(256)
    bd = jnp.asarray(d[:, None] // HEAD_DIM == d[None, :] // HEAD_DIM, BF16)
    qscale = QK_SCALE * LOG2E
    for l in range(depth):
        gq = jnp.broadcast_to(jnp.tile(a_q_norm[l] * qscale, N_HEADS)[:, None], (256, tm))
        gcq = jnp.broadcast_to(jnp.tile(c_q_norm[l] * qscale, N_HEADS)[:, None], (256, tm))
        gk = jnp.tile(a_k_norm[l], N_HEADS)[None, :]
        gck = jnp.tile(c_k_norm[l], C_KV_HEADS)[None, :]
        pw = jnp.zeros((256, 256), F32)
        for g in range(4):
            pw = pw.at[g * 64:(g + 1) * 64, g * 64:(g + 1) * 64].set(pool_w[l, g])
        qT, kaug, vT, cqT, ck, cvT, gates, ybd = _in_proj(
            x, norm_g[l][None, :], w_in[l].astype(BF16), bd, gq, gk, gcq, gck, tb['kca'],
            tb['invcnt'], pw.astype(BF16), pool_scale[l][None, :], conv_w[l])
        ya = _moba(qT, kaug, vT, tb['avg'], tb['crow_a'], tb['causal'], gates)
        sink = jnp.repeat((c_sinks[l] * LOG2E).reshape(C_KV_HEADS, 2), SWA_BLOCK, axis=1)[:, None, :]
        yc = _swa(cqT, ck, cvT, tb['band'], sink, gates)
        x = _out_proj(x, ya, yc, ybd, w_out[l].astype(BF16))
    return x
```

```python
import math

import jax
import jax.numpy as jnp
import numpy as np
from jax import lax
from jax.experimental import pallas as pl
from jax.experimental.pallas import tpu as pltpu

F32 = jnp.float32
BF16 = jnp.bfloat16

D_MODEL = 1024
HEAD_DIM = 64
N_HEADS = 4
C_KV_HEADS = 2
NORM_EPS = 1e-6
MOBA_BLOCK = 256
MOBA_TOPK = 3
SWA_BLOCK = 128
POOL_WINDOWS = (2, 4, 8, 16)
CONV_WIDTH = 3
HALO = 16

AUG = 128
COL_ONEHOT = 64
COL_POS = 96
NEG_BIG = -(2.0 ** 100)
LOG2E = math.log2(math.e)
QK_SCALE = HEAD_DIM ** -0.5
SLOPES = np.exp2(-(8.0 / (2 * N_HEADS)) * np.arange(1, 2 * N_HEADS + 1)).astype(np.float32)
SLOPES_C, SLOPES_A = SLOPES[:N_HEADS], SLOPES[N_HEADS:]

IN_TILE = 1024
ROW_TILE = 1024
Q_TILE = 256
VMEM_LIMIT = 56 * 1024 * 1024

SEG = dict(aq=0, ak=256, av=512, ag=768, bu=1024, bg=1280, cq=1536, ck=1792, cv=1920,
           cg=2048, dh=2304, db=2560, dc=2816, dg=3072)
IN_PROJ_WIDTH = 3328


def _dot(a, b):
    return jnp.dot(a, b, preferred_element_type=F32)


def _silu(v):
    h = 0.5 * v
    return h + h * jnp.tanh(h)


def _fold8(x, op):
    out = x[0:8]
    for g in range(1, x.shape[0] // 8):
        out = op(out, x[g * 8:(g + 1) * 8])
    return out


def _in_proj_kernel(x_ref, g_ref, w_ref, bd_ref, gq_ref, gk_ref, gcq_ref, gck_ref, kca_ref,
                    invcnt_ref, pw_ref, pscale_ref, cw_ref,
                    qT_ref, kaug_ref, vT_ref, cqT_ref, ck_ref, cvT_ref, gates_ref, ybd_ref,
                    hist_ref):
    tm = x_ref.shape[1]
    s_idx = pl.program_id(1)
    x = x_ref[0]
    ms = jnp.mean(x * x, axis=-1, keepdims=True)
    h = (x * lax.rsqrt(ms + NORM_EPS) * g_ref[...]).astype(BF16)

    def proj(lo, width):
        return _dot(h, w_ref[:, lo:lo + width])

    def norm_t(p, gain_ref):
        pt = p.T
        outs = []
        for hh in range(N_HEADS):
            ph = pt[hh * HEAD_DIM:(hh + 1) * HEAD_DIM]
            ss = jnp.sum(ph * ph, axis=0, keepdims=True) * (1.0 / HEAD_DIM)
            outs.append(ph * lax.rsqrt(ss + NORM_EPS))
        return (jnp.concatenate(outs, axis=0) * gain_ref[...]).astype(BF16)

    def norm_rows(p, gain_ref, width):
        ss = _dot((p * p).astype(BF16), bd_ref[0:width, 0:width]) * (1.0 / HEAD_DIM)
        return p * lax.rsqrt(ss + NORM_EPS) * gain_ref[...]

    def build_kaug(kn, kconst):
        lane = lax.broadcasted_iota(jnp.int32, (tm, AUG), 1)
        blocks = []
        for hh in range(N_HEADS):
            col = kn[:, (hh // 2) * AUG:(hh // 2 + 1) * AUG]
            if hh % 2 == 1:
                col = pltpu.roll(col, HEAD_DIM, axis=1)
            blocks.append(jnp.where(lane < HEAD_DIM, col, kconst).astype(BF16))
        return blocks


    @pl.when(s_idx == 0)
    def _():
        hist_ref[...] = jnp.zeros(hist_ref.shape, F32)

    back = lambda v, k: pltpu.roll(v, k, axis=0)

    pd = proj(SEG['dh'], 1024)
    dh, db, dc, dg = (pd[:, k * 256:(k + 1) * 256] for k in range(4))
    u = dc * dh
    ue = jnp.concatenate([hist_ref[1], u], axis=0)
    hist_ref[1] = u[tm - HALO:]
    conv = (cw_ref[0:1, :] * back(ue, 2) + cw_ref[1:2, :] * back(ue, 1) + cw_ref[2:3, :] * ue)[HALO:]
    ybd_ref[0, :, 256:512] = (db * conv * _silu(dg)).astype(BF16)

    pb = proj(SEG['bu'], 512)
    bu, bg = pb[:, 0:256], pb[:, 256:512]
    e = jnp.concatenate([hist_ref[0], bu], axis=0)
    hist_ref[0] = bu[tm - HALO:]
    a2 = e + back(e, 1)
    a4 = a2 + back(a2, 2)
    a4r = a4[:, 128:256]
    a8 = a4r + back(a4r, 4)
    a16 = a8 + back(a8, 8)
    lane = lax.broadcasted_iota(jnp.int32, (HALO + tm, 128), 1)
    sums = jnp.concatenate([jnp.where(lane < 64, a2[:, 0:128], a4[:, 0:128]),
                            jnp.where(lane < 64, a8, a16)], axis=1)
    pooled = sums[HALO:] * invcnt_ref[0] - bu
    yb = _dot(pooled.astype(BF16), pw_ref[...]) * pscale_ref[...] * _silu(bg)
    ybd_ref[0, :, 0:256] = yb.astype(BF16)

    nblk = tm // Q_TILE
    qT = norm_t(proj(SEG['aq'], 256), gq_ref)
    for t in range(nblk):
        qT_ref[0, t] = qT[:, t * Q_TILE:(t + 1) * Q_TILE]
    kn = norm_rows(proj(SEG['ak'], 256), gk_ref, 256)
    kaug = build_kaug(kn, kca_ref[...].astype(F32))
    for hh in range(N_HEADS):
        kaug_ref[0, hh] = kaug[hh]
    vT = proj(SEG['av'], 256).T.astype(BF16)
    for t in range(nblk):
        vT_ref[0, t] = vT[:, t * Q_TILE:(t + 1) * Q_TILE]
    cqT = norm_t(proj(SEG['cq'], 256), gcq_ref)
    for t in range(nblk):
        cqT_ref[0, t] = cqT[:, t * Q_TILE:(t + 1) * Q_TILE]
    ckv = proj(SEG['ck'], 256)
    ck_ref[0] = norm_rows(ckv[:, 0:128], gck_ref, 128).astype(BF16)
    cvT = ckv[:, 128:256].T.astype(BF16)
    for t in range(tm // SWA_BLOCK):
        cvT_ref[0, t] = cvT[:, t * SWA_BLOCK:(t + 1) * SWA_BLOCK]
    gates_ref[0, :, 0:256] = _silu(proj(SEG['ag'], 256)).astype(BF16)
    gates_ref[0, :, 256:512] = _silu(proj(SEG['cg'], 256)).astype(BF16)


def _in_proj(x, g, w, bd, gq, gk, gcq, gck, kca, invcnt, pw, pscale, cw):
    B, S, _ = x.shape
    tm = IN_TILE
    ns = S // tm
    nq = S // Q_TILE
    nkb = S // SWA_BLOCK
    const = lambda *shape: pl.BlockSpec(shape, lambda b, s: (0,) * len(shape))
    out_shape = (
        jax.ShapeDtypeStruct((B, nq, 256, Q_TILE), BF16),
        jax.ShapeDtypeStruct((B, N_HEADS, S, AUG), BF16),
        jax.ShapeDtypeStruct((B, nq, 256, Q_TILE), BF16),
        jax.ShapeDtypeStruct((B, nq, 256, Q_TILE), BF16),
        jax.ShapeDtypeStruct((B, S, 128), BF16),
        jax.ShapeDtypeStruct((B, nkb, 128, SWA_BLOCK), BF16),
        jax.ShapeDtypeStruct((B, S, 512), BF16),
        jax.ShapeDtypeStruct((B, S, 512), BF16),
    )
    r = tm // Q_TILE
    out_specs = (
        pl.BlockSpec((1, r, 256, Q_TILE), lambda b, s: (b, s, 0, 0)),
        pl.BlockSpec((1, N_HEADS, tm, AUG), lambda b, s: (b, 0, s, 0)),
        pl.BlockSpec((1, r, 256, Q_TILE), lambda b, s: (b, s, 0, 0)),
        pl.BlockSpec((1, r, 256, Q_TILE), lambda b, s: (b, s, 0, 0)),
        pl.BlockSpec((1, tm, 128), lambda b, s: (b, s, 0)),
        pl.BlockSpec((1, tm // SWA_BLOCK, 128, SWA_BLOCK), lambda b, s: (b, s, 0, 0)),
        pl.BlockSpec((1, tm, 512), lambda b, s: (b, s, 0)),
        pl.BlockSpec((1, tm, 512), lambda b, s: (b, s, 0)),
    )
    in_specs = [
        pl.BlockSpec((1, tm, D_MODEL), lambda b, s: (b, s, 0)),
        const(1, D_MODEL),
        const(D_MODEL, IN_PROJ_WIDTH),
        const(256, 256),
        const(256, tm), const(1, 256), const(256, tm), const(1, 128),
        pl.BlockSpec((tm, AUG), lambda b, s: (s, 0)),
        pl.BlockSpec((1, tm, 256), lambda b, s: (jnp.minimum(s, 1), 0, 0)),
        const(256, 256), const(1, 256), const(CONV_WIDTH, 256),
    ]
    return pl.pallas_call(
        _in_proj_kernel, grid=(B, ns), in_specs=in_specs, out_specs=out_specs, out_shape=out_shape,
        scratch_shapes=[pltpu.VMEM((2, HALO, 256), F32)],
        compiler_params=pltpu.CompilerParams(dimension_semantics=("parallel", "arbitrary"),
                                             vmem_limit_bytes=VMEM_LIMIT),
        name="in_proj",
    )(x, g, w, bd, gq, gk, gcq, gck, kca, invcnt, pw, pscale, cw)


def _attn_kernel(qT_ref, kaug_ref, vT_ref, avg_ref, crow_ref, causal_ref, gate_ref,
                 cqT_ref, ck_ref, cvT_ref, band0_ref, band_ref, sink_ref, cgate_ref,
                 out_ref, cout_ref,
                 kmean_ref, qaug_ref, acc_ref, s_ref, p_ref, mx_ref, cs_ref, cp_ref, cmx_ref):
    i = pl.program_id(1)
    swa_blocks = Q_TILE // SWA_BLOCK
    S = kaug_ref.shape[2]
    nb = S // MOBA_BLOCK
    chunk = 1024

    @pl.when(i == 0)
    def _():
        for hh in range(N_HEADS):
            acc = jnp.zeros((nb, AUG), F32)
            for c in range(S // chunk):
                acc = acc + _dot(avg_ref[:, c * chunk:(c + 1) * chunk],
                                 kaug_ref[0, hh, c * chunk:(c + 1) * chunk, :])
            kmean_ref[hh] = acc

    row = lax.broadcasted_iota(jnp.int32, (nb, Q_TILE), 0).astype(F32)
    i_f = i.astype(F32)
    neg_inf = jnp.float32(-jnp.inf)
    heads = range(N_HEADS)
    vh = lambda hh, j: vT_ref[0, j, hh * HEAD_DIM:(hh + 1) * HEAD_DIM, :]
    kb = lambda hh, j: kaug_ref[0, hh, pl.ds(pl.multiple_of(j * MOBA_BLOCK, MOBA_BLOCK), MOBA_BLOCK), :]

    for hh in heads:
        qh = qT_ref[0, 0, hh * HEAD_DIM:(hh + 1) * HEAD_DIM, :]
        km = kmean_ref[hh][:, 0:HEAD_DIM]
        km_hi = km.astype(BF16)
        km_lo = (km - km_hi.astype(F32)).astype(BF16)
        bs = _dot(km_hi, qh) + _dot(km_lo, qh)
        xs = jnp.where(row < i_f, bs, neg_inf)
        sel = row == i_f
        for _ in range(MOBA_TOPK):
            mx = jnp.max(xs, axis=0, keepdims=True)
            first = jnp.min(jnp.where(xs == mx, row, jnp.float32(nb)), axis=0, keepdims=True)
            pick = row == first
            sel = sel | (pick & (mx > neg_inf))
            xs = jnp.where(pick, neg_inf, xs)
        negmask = jnp.where(sel, 0.0, NEG_BIG).astype(BF16)
        pad = jnp.zeros((COL_POS - COL_ONEHOT - nb, Q_TILE), BF16) if nb < 32 else None
        parts = [qh, negmask] + ([pad] if pad is not None else []) + [crow_ref[hh]]
        qaug_ref[hh] = jnp.concatenate(parts, axis=0)

    _swa_front(i, swa_blocks, cqT_ref, ck_ref, band0_ref, band_ref, sink_ref, cs_ref, cp_ref, cmx_ref)

    def blk(pos):
        past = jnp.minimum(i + 1, nb - 1)
        return jnp.where(pos <= 0, i, jnp.where(pos <= i, pos - 1, past))

    kblk = MOBA_BLOCK
    sub = 8
    rows = 64

    def scores(hh, half, s):
        s_ref[hh, half * kblk:(half + 1) * kblk] = s
        mx_ref[hh, half * sub:(half + 1) * sub] = _fold8(s, jnp.maximum)

    for hh in heads:
        scores(hh, 0, _dot(kb(hh, i), qaug_ref[hh]) + causal_ref[...])
        scores(hh, 1, _dot(kb(hh, blk(1)), qaug_ref[hh]))
    p_ref[...] = jnp.zeros(p_ref.shape, BF16)
    acc_ref[...] = jnp.zeros(acc_ref.shape, F32)
    neg = jnp.full((1, Q_TILE), NEG_BIG, F32)
    one = jnp.ones((1, Q_TILE), F32)

    def values_step(t_prev, alphas):
        for hh in heads:
            acc_ref[hh] = (alphas[hh] * acc_ref[hh]
                           + _dot(vh(hh, blk(2 * t_prev)), p_ref[hh, 0:kblk])
                           + _dot(vh(hh, blk(2 * t_prev + 1)), p_ref[hh, kblk:2 * kblk]))

    def body(t, carry):
        ms, ls, alphas = carry
        values_step(t - 1, alphas)
        ms_new, ls_new, alphas_new = [], [], []
        for hh in heads:
            m_new = jnp.maximum(ms[hh], jnp.max(mx_ref[hh], axis=0, keepdims=True))
            alpha = jnp.exp2(ms[hh] - m_new)
            l8 = alpha * ls[hh]
            for c in range(2 * kblk // rows):
                p = jnp.exp2(s_ref[hh, c * rows:(c + 1) * rows] - m_new)
                p_ref[hh, c * rows:(c + 1) * rows] = p.astype(BF16)
                l8 = l8 + _fold8(p, jnp.add)
            ms_new.append(m_new)
            ls_new.append(l8)
            alphas_new.append(alpha)
        for hh in heads:
            scores(hh, 0, _dot(kb(hh, blk(2 * t + 2)), qaug_ref[hh]))
            scores(hh, 1, _dot(kb(hh, blk(2 * t + 3)), qaug_ref[hh]))
        return tuple(ms_new), tuple(ls_new), tuple(alphas_new)

    zero8 = jnp.zeros((sub, Q_TILE), F32)
    init = ((neg,) * N_HEADS, (zero8,) * N_HEADS, (one,) * N_HEADS)
    steps = (i + 2) // 2
    ms, ls, alphas = lax.fori_loop(0, steps, body, init)
    values_step(steps - 1, alphas)
    _swa_back(i, swa_blocks, cvT_ref, cp_ref, cmx_ref, cgate_ref, cout_ref)
    outs = [acc_ref[hh] / jnp.sum(ls[hh], axis=0, keepdims=True) for hh in heads]
    o = jnp.concatenate(outs, axis=0).T
    out_ref[0] = (o * gate_ref[0].astype(F32)).astype(BF16)


def _attention(qT, kaug, vT, avg, crow, causal, cqT, ck, cvT, band, sink, gates):
    B, nq = qT.shape[0], qT.shape[1]
    S = kaug.shape[2]
    nb = S // MOBA_BLOCK
    nkb = S // SWA_BLOCK
    nchain = (Q_TILE // SWA_BLOCK) * C_KV_HEADS
    win = 2 * SWA_BLOCK
    tile = lambda col: pl.BlockSpec((1, Q_TILE, 256), lambda b, i: (b, i, col))
    in_specs = [
        pl.BlockSpec((1, 1, 256, Q_TILE), lambda b, i: (b, i, 0, 0)),
        pl.BlockSpec((1, N_HEADS, S, AUG), lambda b, i: (b, 0, 0, 0)),
        pl.BlockSpec((1, nq, 256, Q_TILE), lambda b, i: (b, 0, 0, 0)),
        pl.BlockSpec((nb, S), lambda b, i: (0, 0)),
        pl.BlockSpec((N_HEADS, AUG - COL_POS, Q_TILE), lambda b, i: (0, 0, 0)),
        pl.BlockSpec((MOBA_BLOCK, Q_TILE), lambda b, i: (0, 0)),
        tile(0),
        pl.BlockSpec((1, 1, 256, Q_TILE), lambda b, i: (b, i, 0, 0)),
        pl.BlockSpec((1, S, 128), lambda b, i: (b, 0, 0)),
        pl.BlockSpec((1, nkb, 128, SWA_BLOCK), lambda b, i: (b, 0, 0, 0)),
        pl.BlockSpec((C_KV_HEADS, 1, win, win), lambda b, i: (0, jnp.minimum(i, 1), 0, 0)),
        pl.BlockSpec((C_KV_HEADS, 1, win, win), lambda b, i: (0, 1, 0, 0)),
        pl.BlockSpec((C_KV_HEADS, 1, win), lambda b, i: (0, 0, 0)),
        tile(1),
    ]
    out = jax.ShapeDtypeStruct((B, S, 256), BF16)
    return pl.pallas_call(
        _attn_kernel, grid=(B, nq), in_specs=in_specs,
        out_specs=(tile(0), tile(0)), out_shape=(out, out),
        scratch_shapes=[pltpu.VMEM((N_HEADS, nb, AUG), F32),
                        pltpu.VMEM((N_HEADS, AUG, Q_TILE), BF16),
                        pltpu.VMEM((N_HEADS, HEAD_DIM, Q_TILE), F32),
                        pltpu.VMEM((N_HEADS, 2 * MOBA_BLOCK, Q_TILE), F32),
                        pltpu.VMEM((N_HEADS, 2 * MOBA_BLOCK, Q_TILE), BF16),
                        pltpu.VMEM((N_HEADS, 16, Q_TILE), F32),
                        pltpu.VMEM((nchain, win, win), F32),
                        pltpu.VMEM((nchain, win, win), BF16),
                        pltpu.VMEM((nchain, 8, win), F32)],
        compiler_params=pltpu.CompilerParams(dimension_semantics=("parallel", "arbitrary"),
                                             vmem_limit_bytes=VMEM_LIMIT),
        name="attention",
    )(qT, kaug, vT, avg, crow, causal, gates, cqT, ck, cvT, band, band, sink, gates)


def _swa_front(i, blocks, cqT_ref, ck_ref, band0_ref, band_ref, sink_ref, s_ref, p_ref, mx_ref):
    win = 2 * SWA_BLOCK
    rows = 64
    zrows = jnp.zeros((HEAD_DIM, win), BF16)
    chains = [(t, kv) for t in range(blocks) for kv in range(C_KV_HEADS)]
    for t in range(blocks):
        n = i * blocks + t
        prev = jnp.maximum(n - 1, 0)
        kwin = jnp.concatenate([
            ck_ref[0, pl.ds(pl.multiple_of(prev * SWA_BLOCK, SWA_BLOCK), SWA_BLOCK), :],
            ck_ref[0, pl.ds(pl.multiple_of(n * SWA_BLOCK, SWA_BLOCK), SWA_BLOCK), :]], axis=0)
        for kv in range(C_KV_HEADS):
            c = chains.index((t, kv))
            q2 = jnp.concatenate([
                cqT_ref[0, t // 2, (2 * kv + g) * HEAD_DIM:(2 * kv + g + 1) * HEAD_DIM,
                        (t % 2) * SWA_BLOCK:(t % 2 + 1) * SWA_BLOCK] for g in range(2)], axis=1)
            qz = jnp.concatenate([q2, zrows] if kv == 0 else [zrows, q2], axis=0)
            band = band0_ref[kv, 0] if t == 0 else band_ref[kv, 0]
            s = _dot(kwin, qz) + band
            s_ref[c] = s
            mx_ref[c] = _fold8(s, jnp.maximum)
    for c, (t, kv) in enumerate(chains):
        m = jnp.max(mx_ref[c], axis=0, keepdims=True)
        l8 = jnp.zeros((8, win), F32)
        for r in range(win // rows):
            p = jnp.exp2(s_ref[c, r * rows:(r + 1) * rows] - m)
            p_ref[c, r * rows:(r + 1) * rows] = p.astype(BF16)
            l8 = l8 + _fold8(p, jnp.add)
        mx_ref[c, 0:1] = 1.0 / (jnp.sum(l8, axis=0, keepdims=True) + jnp.exp2(sink_ref[kv] - m))


def _swa_back(i, blocks, cvT_ref, p_ref, mx_ref, gate_ref, out_ref):
    chains = [(t, kv) for t in range(blocks) for kv in range(C_KV_HEADS)]
    outs = {}
    for c, (t, kv) in enumerate(chains):
        n = i * blocks + t
        prev = jnp.maximum(n - 1, 0)
        vwin = jnp.concatenate([cvT_ref[0, prev, kv * HEAD_DIM:(kv + 1) * HEAD_DIM, :],
                                cvT_ref[0, n, kv * HEAD_DIM:(kv + 1) * HEAD_DIM, :]], axis=1)
        outs[t, kv] = _dot(vwin, p_ref[c]) * mx_ref[c, 0:1]
    for u in range(blocks * SWA_BLOCK // Q_TILE):
        head_rows = []
        for kv in range(C_KV_HEADS):
            for g in range(2):
                head_rows.append(jnp.concatenate(
                    [outs[t, kv][:, g * SWA_BLOCK:(g + 1) * SWA_BLOCK] for t in (2 * u, 2 * u + 1)], axis=1))
        o = jnp.concatenate(head_rows, axis=0).T
        gate = gate_ref[0, u * Q_TILE:(u + 1) * Q_TILE, :].astype(F32)
        out_ref[0, u * Q_TILE:(u + 1) * Q_TILE, :] = (o * gate).astype(BF16)


def _out_proj_kernel(x_ref, ya_ref, yc_ref, ybd_ref, wo_ref, out_ref):
    y = (_dot(ya_ref[0], wo_ref[0:256, :]) + _dot(ybd_ref[0, :, 0:256], wo_ref[256:512, :])
         + _dot(yc_ref[0], wo_ref[512:768, :]) + _dot(ybd_ref[0, :, 256:512], wo_ref[768:1024, :]))
    out_ref[0] = x_ref[0] + y


def _out_proj(x, ya, yc, ybd, wo):
    B, S, _ = x.shape
    tm = ROW_TILE
    rows = lambda width: pl.BlockSpec((1, tm, width), lambda b, s: (b, s, 0))
    in_specs = [rows(D_MODEL), rows(256), rows(256), rows(512),
                pl.BlockSpec((D_MODEL, D_MODEL), lambda b, s: (0, 0))]
    return pl.pallas_call(
        _out_proj_kernel, grid=(B, S // tm), in_specs=in_specs,
        out_specs=rows(D_MODEL),
        out_shape=jax.ShapeDtypeStruct((B, S, D_MODEL), F32),
        compiler_params=pltpu.CompilerParams(dimension_semantics=("parallel", "parallel"),
                                             vmem_limit_bytes=VMEM_LIMIT),
        name="out_proj",
    )(x, ya, yc, ybd, wo)


def _tables(S):
    tm = IN_TILE
    slopes_c, slopes_a = SLOPES_C, SLOPES_A
    pos = np.arange(S)

    kca = np.zeros((S, AUG), np.float32)
    blk, r = pos // MOBA_BLOCK, pos % MOBA_BLOCK
    kca[pos, COL_ONEHOT + blk] = 1.0
    kca[:, COL_POS] = r
    kca[:, COL_POS + 1] = r
    kca[:, COL_POS + 2] = blk
    kca[:, COL_POS + 3] = blk
    kca = jnp.asarray(kca, BF16)

    c = slopes_a * np.float32(LOG2E)
    hi = c.astype(BF16).astype(np.float32)
    lo = c - hi
    rows = np.stack([hi, lo, hi * MOBA_BLOCK, lo * MOBA_BLOCK], axis=-1)
    rows = np.pad(rows, ((0, 0), (0, AUG - COL_POS - 4)))
    crow_a = jnp.asarray(np.broadcast_to(rows[:, :, None], rows.shape + (Q_TILE,)), BF16)

    nb = S // MOBA_BLOCK
    avg = jnp.asarray((pos[None, :] // MOBA_BLOCK == np.arange(nb)[:, None]) / MOBA_BLOCK, BF16)
    kq = np.arange(MOBA_BLOCK)
    causal = jnp.asarray(np.where(kq[:, None] <= kq[None, :], 0.0, NEG_BIG), F32)
    u = np.arange(2 * SWA_BLOCK)[:, None]
    t = np.arange(SWA_BLOCK)[None, :]
    ok = (u > t) & (u <= t + SWA_BLOCK)
    dist = (SWA_BLOCK + t - u).astype(np.float32)
    sc = np.asarray(slopes_c, np.float32) * np.float32(LOG2E)
    band = []
    for kv in range(C_KV_HEADS):
        bias = np.concatenate([-sc[2 * kv] * dist, -sc[2 * kv + 1] * dist], axis=1)
        ok2 = np.concatenate([ok, ok], axis=1)
        band.append(np.stack([np.where(ok2 & (u >= SWA_BLOCK), bias, NEG_BIG),
                              np.where(ok2, bias, NEG_BIG)]))
    band = jnp.asarray(np.stack(band), F32)
    w = np.repeat(np.asarray(POOL_WINDOWS, np.float32), 64)[None, :]
    first = 1.0 / np.minimum(np.arange(tm, dtype=np.float32)[:, None] + 1.0, w)
    invcnt = jnp.asarray(np.stack([first, np.broadcast_to(1.0 / w, (tm, 256))]), F32)
    return dict(kca=kca, crow_a=crow_a, avg=avg, causal=causal, band=band, invcnt=invcnt)


def kernel(x, norm_g, w_in, w_out, a_q_norm, a_k_norm, pool_w, pool_scale, c_q_norm, c_k_norm, c_sinks, conv_w):
    B, S, _ = x.shape
    depth = norm_g.shape[0]
    assert S % max(ROW_TILE, IN_TILE) == 0 and S // MOBA_BLOCK <= 32
    tb = _tables(S)
    tm = IN_TILE
    d = np.arange(256)
    bd = jnp.asarray(d[:, None] // HEAD_DIM == d[None, :] // HEAD_DIM, BF16)
    qscale = QK_SCALE * LOG2E
    for l in range(depth):
        gq = jnp.broadcast_to(jnp.tile(a_q_norm[l] * qscale, N_HEADS)[:, None], (256, tm))
        gcq = jnp.broadcast_to(jnp.tile(c_q_norm[l] * qscale, N_HEADS)[:, None], (256, tm))
        gk = jnp.tile(a_k_norm[l], N_HEADS)[None, :]
        gck = jnp.tile(c_k_norm[l], C_KV_HEADS)[None, :]
        pw = jnp.zeros((256, 256), F32)
        for g in range(4):
            pw = pw.at[g * 64:(g + 1) * 64, g * 64:(g + 1) * 64].set(pool_w[l, g])
        qT, kaug, vT, cqT, ck, cvT, gates, ybd = _in_proj(
            x, norm_g[l][None, :], w_in[l].astype(BF16), bd, gq, gk, gcq, gck, tb['kca'],
            tb['invcnt'], pw.astype(BF16), pool_scale[l][None, :], conv_w[l])
        sink = jnp.repeat((c_sinks[l] * LOG2E).reshape(C_KV_HEADS, 2), SWA_BLOCK, axis=1)[:, None, :]
        ya, yc = _attention(qT, kaug, vT, tb['avg'], tb['crow_a'], tb['causal'],
                            cqT, ck, cvT, tb['band'], sink, gates)
        x = _out_proj(x, ya, yc, ybd, w_out[l].astype(BF16))
    return x
```

```python
import math

import jax
import jax.numpy as jnp
import numpy as np
from jax import lax
from jax.experimental import pallas as pl
from jax.experimental.pallas import tpu as pltpu

F32 = jnp.float32
BF16 = jnp.bfloat16

D_MODEL = 1024
HEAD_DIM = 64
N_HEADS = 4
C_KV_HEADS = 2
NORM_EPS = 1e-6
MOBA_BLOCK = 256
MOBA_TOPK = 3
SWA_BLOCK = 128
POOL_WINDOWS = (2, 4, 8, 16)
CONV_WIDTH = 3
HALO = 16

AUG = 128
COL_ONEHOT = 64
COL_POS = 96
NEG_BIG = -(2.0 ** 100)
LOG2E = math.log2(math.e)
QK_SCALE = HEAD_DIM ** -0.5
SLOPES = np.exp2(-(8.0 / (2 * N_HEADS)) * np.arange(1, 2 * N_HEADS + 1)).astype(np.float32)
SLOPES_C, SLOPES_A = SLOPES[:N_HEADS], SLOPES[N_HEADS:]

IN_TILE = 1024
ROW_TILE = 1024
Q_TILE = 256
VMEM_LIMIT = 56 * 1024 * 1024

SEG = dict(aq=0, ak=256, av=512, ag=768, bu=1024, bg=1280, cq=1536, ck=1792, cv=1920,
           cg=2048, dh=2304, db=2560, dc=2816, dg=3072)
IN_PROJ_WIDTH = 3328


def _dot(a, b):
    return jnp.dot(a, b, preferred_element_type=F32)


def _silu(v):
    h = 0.5 * v
    return h + h * jnp.tanh(h)


def _fold8(x, op):
    out = x[0:8]
    for g in range(1, x.shape[0] // 8):
        out = op(out, x[g * 8:(g + 1) * 8])
    return out


def _in_proj_kernel(x_ref, g_ref, w_ref, bd_ref, gq_ref, gk_ref, gcq_ref, gck_ref, kca_ref,
                    invcnt_ref, pw_ref, pscale_ref, cw_ref,
                    qT_ref, kaug_ref, vT_ref, cqT_ref, ck_ref, cvT_ref, gates_ref, ybd_ref,
                    hist_ref):
    tm = x_ref.shape[1]
    s_idx = pl.program_id(1)
    x = x_ref[0]
    ms = jnp.mean(x * x, axis=-1, keepdims=True)
    h = (x * lax.rsqrt(ms + NORM_EPS) * g_ref[...]).astype(BF16)

    def proj(lo, width):
        return _dot(h, w_ref[:, lo:lo + width])

    def norm_t(p, gain_ref):
        pt = p.T
        outs = []
        for hh in range(N_HEADS):
            ph = pt[hh * HEAD_DIM:(hh + 1) * HEAD_DIM]
            ss = jnp.sum(ph * ph, axis=0, keepdims=True) * (1.0 / HEAD_DIM)
            outs.append(ph * lax.rsqrt(ss + NORM_EPS))
        return (jnp.concatenate(outs, axis=0) * gain_ref[...]).astype(BF16)

    def norm_rows(p, gain_ref, width):
        ss = _dot((p * p).astype(BF16), bd_ref[0:width, 0:width]) * (1.0 / HEAD_DIM)
        return p * lax.rsqrt(ss + NORM_EPS) * gain_ref[...]

    def build_kaug(kn, kconst):
        lane = lax.broadcasted_iota(jnp.int32, (tm, AUG), 1)
        blocks = []
        for hh in range(N_HEADS):
            col = kn[:, (hh // 2) * AUG:(hh // 2 + 1) * AUG]
            if hh % 2 == 1:
                col = pltpu.roll(col, HEAD_DIM, axis=1)
            blocks.append(jnp.where(lane < HEAD_DIM, col, kconst).astype(BF16))
        return blocks


    @pl.when(s_idx == 0)
    def _():
        hist_ref[...] = jnp.zeros(hist_ref.shape, F32)

    back = lambda v, k: pltpu.roll(v, k, axis=0)

    pd = proj(SEG['dh'], 1024)
    dh, db, dc, dg = (pd[:, k * 256:(k + 1) * 256] for k in range(4))
    u = dc * dh
    ue = jnp.concatenate([hist_ref[1], u], axis=0)
    hist_ref[1] = u[tm - HALO:]
    conv = (cw_ref[0:1, :] * back(ue, 2) + cw_ref[1:2, :] * back(ue, 1) + cw_ref[2:3, :] * ue)[HALO:]
    ybd_ref[0, :, 256:512] = (db * conv * _silu(dg)).astype(BF16)

    pb = proj(SEG['bu'], 512)
    bu, bg = pb[:, 0:256], pb[:, 256:512]
    e = jnp.concatenate([hist_ref[0], bu], axis=0)
    hist_ref[0] = bu[tm - HALO:]
    a2 = e + back(e, 1)
    a4 = a2 + back(a2, 2)
    a4r = a4[:, 128:256]
    a8 = a4r + back(a4r, 4)
    a16 = a8 + back(a8, 8)
    lane = lax.broadcasted_iota(jnp.int32, (HALO + tm, 128), 1)
    sums = jnp.concatenate([jnp.where(lane < 64, a2[:, 0:128], a4[:, 0:128]),
                            jnp.where(lane < 64, a8, a16)], axis=1)
    pooled = sums[HALO:] * invcnt_ref[0] - bu
    yb = _dot(pooled.astype(BF16), pw_ref[...]) * pscale_ref[...] * _silu(bg)
    ybd_ref[0, :, 0:256] = yb.astype(BF16)

    nblk = tm // Q_TILE
    qT = norm_t(proj(SEG['aq'], 256), gq_ref)
    for t in range(nblk):
        qT_ref[0, t] = qT[:, t * Q_TILE:(t + 1) * Q_TILE]
    kn = norm_rows(proj(SEG['ak'], 256), gk_ref, 256)
    kaug = build_kaug(kn, kca_ref[...].astype(F32))
    for hh in range(N_HEADS):
        kaug_ref[0, hh] = kaug[hh]
    vT = proj(SEG['av'], 256).T.astype(BF16)
    for t in range(nblk):
        vT_ref[0, t] = vT[:, t * Q_TILE:(t + 1) * Q_TILE]
    cqT = norm_t(proj(SEG['cq'], 256), gcq_ref)
    for t in range(nblk):
        cqT_ref[0, t] = cqT[:, t * Q_TILE:(t + 1) * Q_TILE]
    ckv = proj(SEG['ck'], 256)
    ck_ref[0] = norm_rows(ckv[:, 0:128], gck_ref, 128).astype(BF16)
    cvT = ckv[:, 128:256].T.astype(BF16)
    for t in range(tm // SWA_BLOCK):
        cvT_ref[0, t] = cvT[:, t * SWA_BLOCK:(t + 1) * SWA_BLOCK]
    gates_ref[0, :, 0:256] = _silu(proj(SEG['ag'], 256)).astype(BF16)
    gates_ref[0, :, 256:512] = _silu(proj(SEG['cg'], 256)).astype(BF16)


def _in_proj(x, g, w, bd, gq, gk, gcq, gck, kca, invcnt, pw, pscale, cw):
    B, S, _ = x.shape
    tm = IN_TILE
    ns = S // tm
    nq = S // Q_TILE
    nkb = S // SWA_BLOCK
    const = lambda *shape: pl.BlockSpec(shape, lambda b, s: (0,) * len(shape))
    out_shape = (
        jax.ShapeDtypeStruct((B, nq, 256, Q_TILE), BF16),
        jax.ShapeDtypeStruct((B, N_HEADS, S, AUG), BF16),
        jax.ShapeDtypeStruct((B, nq, 256, Q_TILE), BF16),
        jax.ShapeDtypeStruct((B, nq, 256, Q_TILE), BF16),
        jax.ShapeDtypeStruct((B, S, 128), BF16),
        jax.ShapeDtypeStruct((B, nkb, 128, SWA_BLOCK), BF16),
        jax.ShapeDtypeStruct((B, S, 512), BF16),
        jax.ShapeDtypeStruct((B, S, 512), BF16),
    )
    r = tm // Q_TILE
    out_specs = (
        pl.BlockSpec((1, r, 256, Q_TILE), lambda b, s: (b, s, 0, 0)),
        pl.BlockSpec((1, N_HEADS, tm, AUG), lambda b, s: (b, 0, s, 0)),
        pl.BlockSpec((1, r, 256, Q_TILE), lambda b, s: (b, s, 0, 0)),
        pl.BlockSpec((1, r, 256, Q_TILE), lambda b, s: (b, s, 0, 0)),
        pl.BlockSpec((1, tm, 128), lambda b, s: (b, s, 0)),
        pl.BlockSpec((1, tm // SWA_BLOCK, 128, SWA_BLOCK), lambda b, s: (b, s, 0, 0)),
        pl.BlockSpec((1, tm, 512), lambda b, s: (b, s, 0)),
        pl.BlockSpec((1, tm, 512), lambda b, s: (b, s, 0)),
    )
    in_specs = [
        pl.BlockSpec((1, tm, D_MODEL), lambda b, s: (b, s, 0)),
        const(1, D_MODEL),
        const(D_MODEL, IN_PROJ_WIDTH),
        const(256, 256),
        const(256, tm), const(1, 256), const(256, tm), const(1, 128),
        pl.BlockSpec((tm, AUG), lambda b, s: (s, 0)),
        pl.BlockSpec((1, tm, 256), lambda b, s: (jnp.minimum(s, 1), 0, 0)),
        const(256, 256), const(1, 256), const(CONV_WIDTH, 256),
    ]
    return pl.pallas_call(
        _in_proj_kernel, grid=(B, ns), in_specs=in_specs, out_specs=out_specs, out_shape=out_shape,
        scratch_shapes=[pltpu.VMEM((2, HALO, 256), F32)],
        compiler_params=pltpu.CompilerParams(dimension_semantics=("parallel", "arbitrary"),
                                             vmem_limit_bytes=VMEM_LIMIT),
        name="in_proj",
    )(x, g, w, bd, gq, gk, gcq, gck, kca, invcnt, pw, pscale, cw)


def _attn_kernel(qT_ref, kaug_ref, vT_ref, avg_ref, crow_ref, causal_ref, gate_ref,
                 cqT_ref, ck_ref, cvT_ref, band0_ref, band_ref, sink_ref, cgate_ref,
                 out_ref, cout_ref,
                 kmean_ref, qaug_ref, acc_ref, s_ref, p_ref, mx_ref, cs_ref, cp_ref, cmx_ref):
    i = pl.program_id(1)
    swa_blocks = Q_TILE // SWA_BLOCK
    S = kaug_ref.shape[2]
    nb = S // MOBA_BLOCK
    chunk = 1024

    @pl.when(i == 0)
    def _():
        for hh in range(N_HEADS):
            acc = jnp.zeros((nb, AUG), F32)
            for c in range(S // chunk):
                acc = acc + _dot(avg_ref[:, c * chunk:(c + 1) * chunk],
                                 kaug_ref[0, hh, c * chunk:(c + 1) * chunk, :])
            kmean_ref[hh] = acc

    row = lax.broadcasted_iota(jnp.int32, (nb, Q_TILE), 0).astype(F32)
    i_f = i.astype(F32)
    neg_inf = jnp.float32(-jnp.inf)
    heads = range(N_HEADS)
    vh = lambda hh, j: vT_ref[0, j, hh * HEAD_DIM:(hh + 1) * HEAD_DIM, :]
    kb = lambda hh, j: kaug_ref[0, hh, pl.ds(pl.multiple_of(j * MOBA_BLOCK, MOBA_BLOCK), MOBA_BLOCK), :]

    for hh in heads:
        qh = qT_ref[0, 0, hh * HEAD_DIM:(hh + 1) * HEAD_DIM, :]
        km = kmean_ref[hh][:, 0:HEAD_DIM]
        km_hi = km.astype(BF16)
        km_lo = (km - km_hi.astype(F32)).astype(BF16)
        bs = _dot(km_hi, qh) + _dot(km_lo, qh)
        xs = jnp.where(row < i_f, bs, neg_inf)
        sel = row == i_f
        for _ in range(MOBA_TOPK):
            mx = jnp.max(xs, axis=0, keepdims=True)
            first = jnp.min(jnp.where(xs == mx, row, jnp.float32(nb)), axis=0, keepdims=True)
            pick = row == first
            sel = sel | (pick & (mx > neg_inf))
            xs = jnp.where(pick, neg_inf, xs)
        negmask = jnp.where(sel, 0.0, NEG_BIG).astype(BF16)
        pad = jnp.zeros((COL_POS - COL_ONEHOT - nb, Q_TILE), BF16) if nb < 32 else None
        parts = [qh, negmask] + ([pad] if pad is not None else []) + [crow_ref[hh]]
        qaug_ref[hh] = jnp.concatenate(parts, axis=0)

    _swa_front(i, swa_blocks, cqT_ref, ck_ref, band0_ref, band_ref, sink_ref, cs_ref, cp_ref, cmx_ref)

    def blk(pos):
        past = jnp.minimum(i + 1, nb - 1)
        return jnp.where(pos <= 0, i, jnp.where(pos <= i, pos - 1, past))

    kblk = MOBA_BLOCK
    sub = 8
    rows = 64

    def scores(hh, j0, j1, mask=None):
        s = _dot(jnp.concatenate([kb(hh, j0), kb(hh, j1)], axis=0), qaug_ref[hh])
        if mask is not None:
            s = s + mask
        s_ref[hh] = s
        mx_ref[hh] = _fold8(s, jnp.maximum)

    for hh in heads:
        scores(hh, i, blk(1), causal_ref[...])
    p_ref[...] = jnp.zeros(p_ref.shape, BF16)
    acc_ref[...] = jnp.zeros(acc_ref.shape, F32)
    neg = jnp.full((1, Q_TILE), NEG_BIG, F32)
    one = jnp.ones((1, Q_TILE), F32)

    def values_step(t_prev, alphas):
        for hh in heads:
            acc_ref[hh] = (alphas[hh] * acc_ref[hh]
                           + _dot(vh(hh, blk(2 * t_prev)), p_ref[hh, 0:kblk])
                           + _dot(vh(hh, blk(2 * t_prev + 1)), p_ref[hh, kblk:2 * kblk]))

    def body(t, carry):
        ms, ls, alphas = carry
        values_step(t - 1, alphas)
        ms_new, ls_new, alphas_new = [], [], []
        for hh in heads:
            m_new = jnp.maximum(ms[hh], jnp.max(mx_ref[hh], axis=0, keepdims=True))
            alpha = jnp.exp2(ms[hh] - m_new)
            l8 = alpha * ls[hh]
            for c in range(2 * kblk // rows):
                p = jnp.exp2(s_ref[hh, c * rows:(c + 1) * rows] - m_new)
                p_ref[hh, c * rows:(c + 1) * rows] = p.astype(BF16)
                l8 = l8 + _fold8(p, jnp.add)
            ms_new.append(m_new)
            ls_new.append(l8)
            alphas_new.append(alpha)
        for hh in heads:
            scores(hh, blk(2 * t + 2), blk(2 * t + 3))
        return tuple(ms_new), tuple(ls_new), tuple(alphas_new)

    zero8 = jnp.zeros((sub, Q_TILE), F32)
    init = ((neg,) * N_HEADS, (zero8,) * N_HEADS, (one,) * N_HEADS)
    steps = (i + 2) // 2
    ms, ls, alphas = lax.fori_loop(0, steps, body, init)
    values_step(steps - 1, alphas)
    _swa_back(i, swa_blocks, cvT_ref, cp_ref, cmx_ref, cgate_ref, cout_ref)
    outs = [acc_ref[hh] / jnp.sum(ls[hh], axis=0, keepdims=True) for hh in heads]
    o = jnp.concatenate(outs, axis=0).T
    out_ref[0] = (o * gate_ref[0].astype(F32)).astype(BF16)


def _attention(qT, kaug, vT, avg, crow, causal, cqT, ck, cvT, band, sink, gates):
    B, nq = qT.shape[0], qT.shape[1]
    S = kaug.shape[2]
    nb = S // MOBA_BLOCK
    nkb = S // SWA_BLOCK
    nchain = (Q_TILE // SWA_BLOCK) * C_KV_HEADS
    win = 2 * SWA_BLOCK
    tile = lambda col: pl.BlockSpec((1, Q_TILE, 256), lambda b, i: (b, i, col))
    in_specs = [
        pl.BlockSpec((1, 1, 256, Q_TILE), lambda b, i: (b, i, 0, 0)),
        pl.BlockSpec((1, N_HEADS, S, AUG), lambda b, i: (b, 0, 0, 0)),
        pl.BlockSpec((1, nq, 256, Q_TILE), lambda b, i: (b, 0, 0, 0)),
        pl.BlockSpec((nb, S), lambda b, i: (0, 0)),
        pl.BlockSpec((N_HEADS, AUG - COL_POS, Q_TILE), lambda b, i: (0, 0, 0)),
        pl.BlockSpec((2 * MOBA_BLOCK, Q_TILE), lambda b, i: (0, 0)),
        tile(0),
        pl.BlockSpec((1, 1, 256, Q_TILE), lambda b, i: (b, i, 0, 0)),
        pl.BlockSpec((1, S, 128), lambda b, i: (b, 0, 0)),
        pl.BlockSpec((1, nkb, 128, SWA_BLOCK), lambda b, i: (b, 0, 0, 0)),
        pl.BlockSpec((C_KV_HEADS, 1, win, win), lambda b, i: (0, jnp.minimum(i, 1), 0, 0)),
        pl.BlockSpec((C_KV_HEADS, 1, win, win), lambda b, i: (0, 1, 0, 0)),
        pl.BlockSpec((C_KV_HEADS, 1, win), lambda b, i: (0, 0, 0)),
        tile(1),
    ]
    out = jax.ShapeDtypeStruct((B, S, 256), BF16)
    return pl.pallas_call(
        _attn_kernel, grid=(B, nq), in_specs=in_specs,
        out_specs=(tile(0), tile(0)), out_shape=(out, out),
        scratch_shapes=[pltpu.VMEM((N_HEADS, nb, AUG), F32),
                        pltpu.VMEM((N_HEADS, AUG, Q_TILE), BF16),
                        pltpu.VMEM((N_HEADS, HEAD_DIM, Q_TILE), F32),
                        pltpu.VMEM((N_HEADS, 2 * MOBA_BLOCK, Q_TILE), F32),
                        pltpu.VMEM((N_HEADS, 2 * MOBA_BLOCK, Q_TILE), BF16),
                        pltpu.VMEM((N_HEADS, 8, Q_TILE), F32),
                        pltpu.VMEM((nchain, win, win), F32),
                        pltpu.VMEM((nchain, win, win), BF16),
                        pltpu.VMEM((nchain, 8, win), F32)],
        compiler_params=pltpu.CompilerParams(dimension_semantics=("parallel", "arbitrary"),
                                             vmem_limit_bytes=VMEM_LIMIT),
        name="attention",
    )(qT, kaug, vT, avg, crow, causal, gates, cqT, ck, cvT, band, band, sink, gates)


def _swa_front(i, blocks, cqT_ref, ck_ref, band0_ref, band_ref, sink_ref, s_ref, p_ref, mx_ref):
    win = 2 * SWA_BLOCK
    rows = 64
    zrows = jnp.zeros((HEAD_DIM, win), BF16)
    chains = [(t, kv) for t in range(blocks) for kv in range(C_KV_HEADS)]
    for t in range(blocks):
        n = i * blocks + t
        prev = jnp.maximum(n - 1, 0)
        kwin = jnp.concatenate([
            ck_ref[0, pl.ds(pl.multiple_of(prev * SWA_BLOCK, SWA_BLOCK), SWA_BLOCK), :],
            ck_ref[0, pl.ds(pl.multiple_of(n * SWA_BLOCK, SWA_BLOCK), SWA_BLOCK), :]], axis=0)
        for kv in range(C_KV_HEADS):
            c = chains.index((t, kv))
            q2 = jnp.concatenate([
                cqT_ref[0, t // 2, (2 * kv + g) * HEAD_DIM:(2 * kv + g + 1) * HEAD_DIM,
                        (t % 2) * SWA_BLOCK:(t % 2 + 1) * SWA_BLOCK] for g in range(2)], axis=1)
            qz = jnp.concatenate([q2, zrows] if kv == 0 else [zrows, q2], axis=0)
            band = band0_ref[kv, 0] if t == 0 else band_ref[kv, 0]
            s = _dot(kwin, qz) + band
            s_ref[c] = s
            mx_ref[c] = _fold8(s, jnp.maximum)
    for c, (t, kv) in enumerate(chains):
        m = jnp.max(mx_ref[c], axis=0, keepdims=True)
        l8 = jnp.zeros((8, win), F32)
        for r in range(win // rows):
            p = jnp.exp2(s_ref[c, r * rows:(r + 1) * rows] - m)
            p_ref[c, r * rows:(r + 1) * rows] = p.astype(BF16)
            l8 = l8 + _fold8(p, jnp.add)
        mx_ref[c, 0:1] = 1.0 / (jnp.sum(l8, axis=0, keepdims=True) + jnp.exp2(sink_ref[kv] - m))


def _swa_back(i, blocks, cvT_ref, p_ref, mx_ref, gate_ref, out_ref):
    chains = [(t, kv) for t in range(blocks) for kv in range(C_KV_HEADS)]
    outs = {}
    for c, (t, kv) in enumerate(chains):
        n = i * blocks + t
        prev = jnp.maximum(n - 1, 0)
        vwin = jnp.concatenate([cvT_ref[0, prev, kv * HEAD_DIM:(kv + 1) * HEAD_DIM, :],
                                cvT_ref[0, n, kv * HEAD_DIM:(kv + 1) * HEAD_DIM, :]], axis=1)
        outs[t, kv] = _dot(vwin, p_ref[c]) * mx_ref[c, 0:1]
    for u in range(blocks * SWA_BLOCK // Q_TILE):
        head_rows = []
        for kv in range(C_KV_HEADS):
            for g in range(2):
                head_rows.append(jnp.concatenate(
                    [outs[t, kv][:, g * SWA_BLOCK:(g + 1) * SWA_BLOCK] for t in (2 * u, 2 * u + 1)], axis=1))
        o = jnp.concatenate(head_rows, axis=0).T
        gate = gate_ref[0, u * Q_TILE:(u + 1) * Q_TILE, :].astype(F32)
        out_ref[0, u * Q_TILE:(u + 1) * Q_TILE, :] = (o * gate).astype(BF16)


def _out_proj_kernel(x_ref, ya_ref, yc_ref, ybd_ref, wo_ref, out_ref):
    y = (_dot(ya_ref[0], wo_ref[0:256, :]) + _dot(ybd_ref[0, :, 0:256], wo_ref[256:512, :])
         + _dot(yc_ref[0], wo_ref[512:768, :]) + _dot(ybd_ref[0, :, 256:512], wo_ref[768:1024, :]))
    out_ref[0] = x_ref[0] + y


def _out_proj(x, ya, yc, ybd, wo):
    B, S, _ = x.shape
    tm = ROW_TILE
    rows = lambda width: pl.BlockSpec((1, tm, width), lambda b, s: (b, s, 0))
    in_specs = [rows(D_MODEL), rows(256), rows(256), rows(512),
                pl.BlockSpec((D_MODEL, D_MODEL), lambda b, s: (0, 0))]
    return pl.pallas_call(
        _out_proj_kernel, grid=(B, S // tm), in_specs=in_specs,
        out_specs=rows(D_MODEL),
        out_shape=jax.ShapeDtypeStruct((B, S, D_MODEL), F32),
        compiler_params=pltpu.CompilerParams(dimension_semantics=("parallel", "parallel"),
                                             vmem_limit_bytes=VMEM_LIMIT),
        name="out_proj",
    )(x, ya, yc, ybd, wo)


def _tables(S):
    tm = IN_TILE
    slopes_c, slopes_a = SLOPES_C, SLOPES_A
    pos = np.arange(S)

    kca = np.zeros((S, AUG), np.float32)
    blk, r = pos // MOBA_BLOCK, pos % MOBA_BLOCK
    kca[pos, COL_ONEHOT + blk] = 1.0
    kca[:, COL_POS] = r
    kca[:, COL_POS + 1] = r
    kca[:, COL_POS + 2] = blk
    kca[:, COL_POS + 3] = blk
    kca = jnp.asarray(kca, BF16)

    c = slopes_a * np.float32(LOG2E)
    hi = c.astype(BF16).astype(np.float32)
    lo = c - hi
    rows = np.stack([hi, lo, hi * MOBA_BLOCK, lo * MOBA_BLOCK], axis=-1)
    rows = np.pad(rows, ((0, 0), (0, AUG - COL_POS - 4)))
    crow_a = jnp.asarray(np.broadcast_to(rows[:, :, None], rows.shape + (Q_TILE,)), BF16)

    nb = S // MOBA_BLOCK
    avg = jnp.asarray((pos[None, :] // MOBA_BLOCK == np.arange(nb)[:, None]) / MOBA_BLOCK, BF16)
    kq = np.arange(MOBA_BLOCK)
    causal = np.where(kq[:, None] <= kq[None, :], 0.0, NEG_BIG)
    causal = jnp.asarray(np.concatenate([causal, np.zeros_like(causal)], axis=0), F32)
    u = np.arange(2 * SWA_BLOCK)[:, None]
    t = np.arange(SWA_BLOCK)[None, :]
    ok = (u > t) & (u <= t + SWA_BLOCK)
    dist = (SWA_BLOCK + t - u).astype(np.float32)
    sc = np.asarray(slopes_c, np.float32) * np.float32(LOG2E)
    band = []
    for kv in range(C_KV_HEADS):
        bias = np.concatenate([-sc[2 * kv] * dist, -sc[2 * kv + 1] * dist], axis=1)
        ok2 = np.concatenate([ok, ok], axis=1)
        band.append(np.stack([np.where(ok2 & (u >= SWA_BLOCK), bias, NEG_BIG),
                              np.where(ok2, bias, NEG_BIG)]))
    band = jnp.asarray(np.stack(band), F32)
    w = np.repeat(np.asarray(POOL_WINDOWS, np.float32), 64)[None, :]
    first = 1.0 / np.minimum(np.arange(tm, dtype=np.float32)[:, None] + 1.0, w)
    invcnt = jnp.asarray(np.stack([first, np.broadcast_to(1.0 / w, (tm, 256))]), F32)
    return dict(kca=kca, crow_a=crow_a, avg=avg, causal=causal, band=band, invcnt=invcnt)


def kernel(x, norm_g, w_in, w_out, a_q_norm, a_k_norm, pool_w, pool_scale, c_q_norm, c_k_norm, c_sinks, conv_w):
    B, S, _ = x.shape
    depth = norm_g.shape[0]
    assert S % max(ROW_TILE, IN_TILE) == 0 and S // MOBA_BLOCK <= 32
    tb = _tables(S)
    tm = IN_TILE
    d = np.arange(256)
    bd = jnp.asarray(d[:, None] // HEAD_DIM == d[None, :] // HEAD_DIM, BF16)
    qscale = QK_SCALE * LOG2E
    for l in range(depth):
        gq = jnp.broadcast_to(jnp.tile(a_q_norm[l] * qscale, N_HEADS)[:, None], (256, tm))
        gcq = jnp.broadcast_to(jnp.tile(c_q_norm[l] * qscale, N_HEADS)[:, None], (256, tm))
        gk = jnp.tile(a_k_norm[l], N_HEADS)[None, :]
        gck = jnp.tile(c_k_norm[l], C_KV_HEADS)[None, :]
        pw = jnp.zeros((256, 256), F32)
        for g in range(4):
            pw = pw.at[g * 64:(g + 1) * 64, g * 64:(g + 1) * 64].set(pool_w[l, g])
        qT, kaug, vT, cqT, ck, cvT, gates, ybd = _in_proj(
            x, norm_g[l][None, :], w_in[l].astype(BF16), bd, gq, gk, gcq, gck, tb['kca'],
            tb['invcnt'], pw.astype(BF16), pool_scale[l][None, :], conv_w[l])
        sink = jnp.repeat((c_sinks[l] * LOG2E).reshape(C_KV_HEADS, 2), SWA_BLOCK, axis=1)[:, None, :]
        ya, yc = _attention(qT, kaug, vT, tb['avg'], tb['crow_a'], tb['causal'],
                            cqT, ck, cvT, tb['band'], sink, gates)
        x = _out_proj(x, ya, yc, ybd, w_out[l].astype(BF16))
    return x
```

```python
import math

import jax
import jax.numpy as jnp
import numpy as np
from jax import lax
from jax.experimental import pallas as pl
from jax.experimental.pallas import tpu as pltpu

F32 = jnp.float32
BF16 = jnp.bfloat16

D_MODEL = 1024
HEAD_DIM = 64
N_HEADS = 4
C_KV_HEADS = 2
NORM_EPS = 1e-6
MOBA_BLOCK = 256
MOBA_TOPK = 3
SWA_BLOCK = 128
POOL_WINDOWS = (2, 4, 8, 16)
CONV_WIDTH = 3
HALO = 16

AUG = 128
COL_ONEHOT = 64
COL_POS = 96
NEG_BIG = -(2.0 ** 100)
LOG2E = math.log2(math.e)
QK_SCALE = HEAD_DIM ** -0.5
SLOPES = np.exp2(-(8.0 / (2 * N_HEADS)) * np.arange(1, 2 * N_HEADS + 1)).astype(np.float32)
SLOPES_C, SLOPES_A = SLOPES[:N_HEADS], SLOPES[N_HEADS:]

IN_TILE = 1024
ROW_TILE = 1024
Q_TILE = 256
VMEM_LIMIT = 56 * 1024 * 1024

SEG = dict(aq=0, ak=256, av=512, ag=768, bu=1024, bg=1280, cq=1536, ck=1792, cv=1920,
           cg=2048, dh=2304, db=2560, dc=2816, dg=3072)
IN_PROJ_WIDTH = 3328


def _dot(a, b):
    return jnp.dot(a, b, preferred_element_type=F32)


def _silu(v):
    h = 0.5 * v
    return h + h * jnp.tanh(h)


def _fold8(x, op):
    out = x[0:8]
    for g in range(1, x.shape[0] // 8):
        out = op(out, x[g * 8:(g + 1) * 8])
    return out


def _in_proj_kernel(x_ref, g_ref, w_ref, bd_ref, gq_ref, gk_ref, gcq_ref, gck_ref, kca_ref,
                    invcnt_ref, pw_ref, pscale_ref, cw_ref,
                    qT_ref, kaug_ref, vT_ref, cqT_ref, ck_ref, cvT_ref, gates_ref, ybd_ref,
                    hist_ref):
    tm = x_ref.shape[1]
    s_idx = pl.program_id(1)
    x = x_ref[0]
    ms = jnp.mean(x * x, axis=-1, keepdims=True)
    h = (x * lax.rsqrt(ms + NORM_EPS) * g_ref[...]).astype(BF16)

    def proj(lo, width):
        return _dot(h, w_ref[:, lo:lo + width])

    def norm_t(p, gain_ref):
        pt = p.T
        outs = []
        for hh in range(N_HEADS):
            ph = pt[hh * HEAD_DIM:(hh + 1) * HEAD_DIM]
            ss = jnp.sum(ph * ph, axis=0, keepdims=True) * (1.0 / HEAD_DIM)
            outs.append(ph * lax.rsqrt(ss + NORM_EPS))
        return (jnp.concatenate(outs, axis=0) * gain_ref[...]).astype(BF16)

    def norm_rows(p, gain_ref, width):
        ss = _dot((p * p).astype(BF16), bd_ref[0:width, 0:width]) * (1.0 / HEAD_DIM)
        return p * lax.rsqrt(ss + NORM_EPS) * gain_ref[...]

    def build_kaug(kn, kconst):
        lane = lax.broadcasted_iota(jnp.int32, (tm, AUG), 1)
        blocks = []
        for hh in range(N_HEADS):
            col = kn[:, (hh // 2) * AUG:(hh // 2 + 1) * AUG]
            if hh % 2 == 1:
                col = pltpu.roll(col, HEAD_DIM, axis=1)
            blocks.append(jnp.where(lane < HEAD_DIM, col, kconst).astype(BF16))
        return blocks


    @pl.when(s_idx == 0)
    def _():
        hist_ref[...] = jnp.zeros(hist_ref.shape, F32)

    back = lambda v, k: pltpu.roll(v, k, axis=0)

    pd = proj(SEG['dh'], 1024)
    dh, db, dc, dg = (pd[:, k * 256:(k + 1) * 256] for k in range(4))
    u = dc * dh
    ue = jnp.concatenate([hist_ref[1], u], axis=0)
    hist_ref[1] = u[tm - HALO:]
    conv = (cw_ref[0:1, :] * back(ue, 2) + cw_ref[1:2, :] * back(ue, 1) + cw_ref[2:3, :] * ue)[HALO:]
    ybd_ref[0, :, 256:512] = (db * conv * _silu(dg)).astype(BF16)

    pb = proj(SEG['bu'], 512)
    bu, bg = pb[:, 0:256], pb[:, 256:512]
    e = jnp.concatenate([hist_ref[0], bu], axis=0)
    hist_ref[0] = bu[tm - HALO:]
    a2 = e + back(e, 1)
    a4 = a2 + back(a2, 2)
    a4r = a4[:, 128:256]
    a8 = a4r + back(a4r, 4)
    a16 = a8 + back(a8, 8)
    lane = lax.broadcasted_iota(jnp.int32, (HALO + tm, 128), 1)
    sums = jnp.concatenate([jnp.where(lane < 64, a2[:, 0:128], a4[:, 0:128]),
                            jnp.where(lane < 64, a8, a16)], axis=1)
    pooled = sums[HALO:] * invcnt_ref[0] - bu
    yb = _dot(pooled.astype(BF16), pw_ref[...]) * pscale_ref[...] * _silu(bg)
    ybd_ref[0, :, 0:256] = yb.astype(BF16)

    nblk = tm // Q_TILE
    qT = norm_t(proj(SEG['aq'], 256), gq_ref)
    for t in range(nblk):
        qT_ref[0, t] = qT[:, t * Q_TILE:(t + 1) * Q_TILE]
    kn = norm_rows(proj(SEG['ak'], 256), gk_ref, 256)
    kaug = build_kaug(kn, kca_ref[...].astype(F32))
    for hh in range(N_HEADS):
        kaug_ref[0, hh] = kaug[hh]
    vT = proj(SEG['av'], 256).T.astype(BF16)
    for t in range(nblk):
        vT_ref[0, t] = vT[:, t * Q_TILE:(t + 1) * Q_TILE]
    cqT = norm_t(proj(SEG['cq'], 256), gcq_ref)
    for t in range(nblk):
        cqT_ref[0, t] = cqT[:, t * Q_TILE:(t + 1) * Q_TILE]
    ckv = proj(SEG['ck'], 256)
    ck_ref[0] = norm_rows(ckv[:, 0:128], gck_ref, 128).astype(BF16)
    cvT = ckv[:, 128:256].T.astype(BF16)
    for t in range(tm // SWA_BLOCK):
        cvT_ref[0, t] = cvT[:, t * SWA_BLOCK:(t + 1) * SWA_BLOCK]
    gates_ref[0, :, 0:256] = _silu(proj(SEG['ag'], 256)).astype(BF16)
    gates_ref[0, :, 256:512] = _silu(proj(SEG['cg'], 256)).astype(BF16)


def _in_proj(x, g, w, bd, gq, gk, gcq, gck, kca, invcnt, pw, pscale, cw):
    B, S, _ = x.shape
    tm = IN_TILE
    ns = S // tm
    nq = S // Q_TILE
    nkb = S // SWA_BLOCK
    const = lambda *shape: pl.BlockSpec(shape, lambda b, s: (0,) * len(shape))
    out_shape = (
        jax.ShapeDtypeStruct((B, nq, 256, Q_TILE), BF16),
        jax.ShapeDtypeStruct((B, N_HEADS, S, AUG), BF16),
        jax.ShapeDtypeStruct((B, nq, 256, Q_TILE), BF16),
        jax.ShapeDtypeStruct((B, nq, 256, Q_TILE), BF16),
        jax.ShapeDtypeStruct((B, S, 128), BF16),
        jax.ShapeDtypeStruct((B, nkb, 128, SWA_BLOCK), BF16),
        jax.ShapeDtypeStruct((B, S, 512), BF16),
        jax.ShapeDtypeStruct((B, S, 512), BF16),
    )
    r = tm // Q_TILE
    out_specs = (
        pl.BlockSpec((1, r, 256, Q_TILE), lambda b, s: (b, s, 0, 0)),
        pl.BlockSpec((1, N_HEADS, tm, AUG), lambda b, s: (b, 0, s, 0)),
        pl.BlockSpec((1, r, 256, Q_TILE), lambda b, s: (b, s, 0, 0)),
        pl.BlockSpec((1, r, 256, Q_TILE), lambda b, s: (b, s, 0, 0)),
        pl.BlockSpec((1, tm, 128), lambda b, s: (b, s, 0)),
        pl.BlockSpec((1, tm // SWA_BLOCK, 128, SWA_BLOCK), lambda b, s: (b, s, 0, 0)),
        pl.BlockSpec((1, tm, 512), lambda b, s: (b, s, 0)),
        pl.BlockSpec((1, tm, 512), lambda b, s: (b, s, 0)),
    )
    in_specs = [
        pl.BlockSpec((1, tm, D_MODEL), lambda b, s: (b, s, 0)),
        const(1, D_MODEL),
        const(D_MODEL, IN_PROJ_WIDTH),
        const(256, 256),
        const(256, tm), const(1, 256), const(256, tm), const(1, 128),
        pl.BlockSpec((tm, AUG), lambda b, s: (s, 0)),
        pl.BlockSpec((1, tm, 256), lambda b, s: (jnp.minimum(s, 1), 0, 0)),
        const(256, 256), const(1, 256), const(CONV_WIDTH, 256),
    ]
    return pl.pallas_call(
        _in_proj_kernel, grid=(B, ns), in_specs=in_specs, out_specs=out_specs, out_shape=out_shape,
        scratch_shapes=[pltpu.VMEM((2, HALO, 256), F32)],
        compiler_params=pltpu.CompilerParams(dimension_semantics=("parallel", "arbitrary"),
                                             vmem_limit_bytes=VMEM_LIMIT),
        name="in_proj",
    )(x, g, w, bd, gq, gk, gcq, gck, kca, invcnt, pw, pscale, cw)


def _attn_kernel(qT_ref, kaug_ref, vT_ref, avg_ref, crow_ref, causal_ref, gate_ref,
                 cqT_ref, ck_ref, cvT_ref, band0_ref, band_ref, sink_ref, cgate_ref,
                 out_ref, cout_ref,
                 kmean_ref, qaug_ref, acc_ref, s_ref, p_ref, mx_ref, cs_ref, cp_ref, cmx_ref):
    i = pl.program_id(1)
    swa_blocks = Q_TILE // SWA_BLOCK
    S = kaug_ref.shape[2]
    nb = S // MOBA_BLOCK
    chunk = 1024

    @pl.when(i == 0)
    def _():
        for hh in range(N_HEADS):
            acc = jnp.zeros((nb, AUG), F32)
            for c in range(S // chunk):
                acc = acc + _dot(avg_ref[:, c * chunk:(c + 1) * chunk],
                                 kaug_ref[0, hh, c * chunk:(c + 1) * chunk, :])
            kmean_ref[hh] = acc

    row = lax.broadcasted_iota(jnp.int32, (nb, Q_TILE), 0).astype(F32)
    i_f = i.astype(F32)
    neg_inf = jnp.float32(-jnp.inf)
    heads = range(N_HEADS)
    vh = lambda hh, j: vT_ref[0, j, hh * HEAD_DIM:(hh + 1) * HEAD_DIM, :]
    kb = lambda hh, j: kaug_ref[0, hh, pl.ds(pl.multiple_of(j * MOBA_BLOCK, MOBA_BLOCK), MOBA_BLOCK), :]

    for hh in heads:
        qh = qT_ref[0, 0, hh * HEAD_DIM:(hh + 1) * HEAD_DIM, :]
        km = kmean_ref[hh][:, 0:HEAD_DIM]
        km_hi = km.astype(BF16)
        km_lo = (km - km_hi.astype(F32)).astype(BF16)
        bs = _dot(km_hi, qh) + _dot(km_lo, qh)
        xs = jnp.where(row < i_f, bs, neg_inf)
        sel = row == i_f
        for _ in range(MOBA_TOPK):
            mx = jnp.max(xs, axis=0, keepdims=True)
            first = jnp.min(jnp.where(xs == mx, row, jnp.float32(nb)), axis=0, keepdims=True)
            pick = row == first
            sel = sel | (pick & (mx > neg_inf))
            xs = jnp.where(pick, neg_inf, xs)
        negmask = jnp.where(sel, 0.0, NEG_BIG).astype(BF16)
        pad = jnp.zeros((COL_POS - COL_ONEHOT - nb, Q_TILE), BF16) if nb < 32 else None
        parts = [qh, negmask] + ([pad] if pad is not None else []) + [crow_ref[hh]]
        qaug_ref[hh] = jnp.concatenate(parts, axis=0)

    _swa_front(i, swa_blocks, cqT_ref, ck_ref, band0_ref, band_ref, sink_ref, cs_ref, cp_ref, cmx_ref)

    def blk(pos):
        past = jnp.minimum(i + 1, nb - 1)
        return jnp.where(pos <= 0, i, jnp.where(pos <= i, pos - 1, past))

    kblk = MOBA_BLOCK
    sub = 8
    rows = 64

    def scores(hh, j0, j1, mask=None):
        s = _dot(jnp.concatenate([kb(hh, j0), kb(hh, j1)], axis=0), qaug_ref[hh])
        if mask is not None:
            s = s + mask
        s_ref[hh] = s
        mx_ref[hh] = _fold8(s, jnp.maximum)

    def scores_step(t):
        for hh in heads:
            scores(hh, blk(2 * t), blk(2 * t + 1))

    def softmax_step(carry):
        ms, ls, _ = carry
        ms_new, ls_new, alphas_new = [], [], []
        for hh in heads:
            m_new = jnp.maximum(ms[hh], jnp.max(mx_ref[hh], axis=0, keepdims=True))
            alpha = jnp.exp2(ms[hh] - m_new)
            l8 = alpha * ls[hh]
            for c in range(2 * kblk // rows):
                p = jnp.exp2(s_ref[hh, c * rows:(c + 1) * rows] - m_new)
                p_ref[hh, c * rows:(c + 1) * rows] = p.astype(BF16)
                l8 = l8 + _fold8(p, jnp.add)
            ms_new.append(m_new)
            ls_new.append(l8)
            alphas_new.append(alpha)
        return tuple(ms_new), tuple(ls_new), tuple(alphas_new)

    def values_step(t, alphas):
        for hh in heads:
            acc_ref[hh] = (alphas[hh] * acc_ref[hh]
                           + _dot(vh(hh, blk(2 * t)), p_ref[hh, 0:kblk])
                           + _dot(vh(hh, blk(2 * t + 1)), p_ref[hh, kblk:2 * kblk]))

    def body(t, carry):
        values_step(t - 1, carry[2])
        carry = softmax_step(carry)
        scores_step(t + 1)
        return carry

    def drain(carry):
        values_step(steps - 2, carry[2])
        return softmax_step(carry)

    steps = (i + 2) // 2
    neg = jnp.full((1, Q_TILE), NEG_BIG, F32)
    zero8 = jnp.zeros((sub, Q_TILE), F32)
    acc_ref[...] = jnp.zeros(acc_ref.shape, F32)
    for hh in heads:
        scores(hh, i, blk(1), causal_ref[...])
    carry = softmax_step(((neg,) * N_HEADS, (zero8,) * N_HEADS, None))
    scores_step(1)
    carry = lax.fori_loop(1, steps - 1, body, carry)
    carry = lax.cond(steps >= 2, drain, lambda c: c, carry)
    ms, ls, alphas = carry
    values_step(steps - 1, alphas)
    _swa_back(i, swa_blocks, cvT_ref, cp_ref, cmx_ref, cgate_ref, cout_ref)
    outs = [acc_ref[hh] / jnp.sum(ls[hh], axis=0, keepdims=True) for hh in heads]
    o = jnp.concatenate(outs, axis=0).T
    out_ref[0] = (o * gate_ref[0].astype(F32)).astype(BF16)


def _attention(qT, kaug, vT, avg, crow, causal, cqT, ck, cvT, band, sink, gates):
    B, nq = qT.shape[0], qT.shape[1]
    S = kaug.shape[2]
    nb = S // MOBA_BLOCK
    nkb = S // SWA_BLOCK
    nchain = (Q_TILE // SWA_BLOCK) * C_KV_HEADS
    win = 2 * SWA_BLOCK
    tile = lambda col: pl.BlockSpec((1, Q_TILE, 256), lambda b, i: (b, i, col))
    in_specs = [
        pl.BlockSpec((1, 1, 256, Q_TILE), lambda b, i: (b, i, 0, 0)),
        pl.BlockSpec((1, N_HEADS, S, AUG), lambda b, i: (b, 0, 0, 0)),
        pl.BlockSpec((1, nq, 256, Q_TILE), lambda b, i: (b, 0, 0, 0)),
        pl.BlockSpec((nb, S), lambda b, i: (0, 0)),
        pl.BlockSpec((N_HEADS, AUG - COL_POS, Q_TILE), lambda b, i: (0, 0, 0)),
        pl.BlockSpec((2 * MOBA_BLOCK, Q_TILE), lambda b, i: (0, 0)),
        tile(0),
        pl.BlockSpec((1, 1, 256, Q_TILE), lambda b, i: (b, i, 0, 0)),
        pl.BlockSpec((1, S, 128), lambda b, i: (b, 0, 0)),
        pl.BlockSpec((1, nkb, 128, SWA_BLOCK), lambda b, i: (b, 0, 0, 0)),
        pl.BlockSpec((C_KV_HEADS, 1, win, win), lambda b, i: (0, jnp.minimum(i, 1), 0, 0)),
        pl.BlockSpec((C_KV_HEADS, 1, win, win), lambda b, i: (0, 1, 0, 0)),
        pl.BlockSpec((C_KV_HEADS, 1, win), lambda b, i: (0, 0, 0)),
        tile(1),
    ]
    out = jax.ShapeDtypeStruct((B, S, 256), BF16)
    return pl.pallas_call(
        _attn_kernel, grid=(B, nq), in_specs=in_specs,
        out_specs=(tile(0), tile(0)), out_shape=(out, out),
        scratch_shapes=[pltpu.VMEM((N_HEADS, nb, AUG), F32),
                        pltpu.VMEM((N_HEADS, AUG, Q_TILE), BF16),
                        pltpu.VMEM((N_HEADS, HEAD_DIM, Q_TILE), F32),
                        pltpu.VMEM((N_HEADS, 2 * MOBA_BLOCK, Q_TILE), F32),
                        pltpu.VMEM((N_HEADS, 2 * MOBA_BLOCK, Q_TILE), BF16),
                        pltpu.VMEM((N_HEADS, 8, Q_TILE), F32),
                        pltpu.VMEM((nchain, win, win), F32),
                        pltpu.VMEM((nchain, win, win), BF16),
                        pltpu.VMEM((nchain, 8, win), F32)],
        compiler_params=pltpu.CompilerParams(dimension_semantics=("parallel", "arbitrary"),
                                             vmem_limit_bytes=VMEM_LIMIT),
        name="attention",
    )(qT, kaug, vT, avg, crow, causal, gates, cqT, ck, cvT, band, band, sink, gates)


def _swa_front(i, blocks, cqT_ref, ck_ref, band0_ref, band_ref, sink_ref, s_ref, p_ref, mx_ref):
    win = 2 * SWA_BLOCK
    rows = 64
    zrows = jnp.zeros((HEAD_DIM, win), BF16)
    chains = [(t, kv) for t in range(blocks) for kv in range(C_KV_HEADS)]
    for t in range(blocks):
        n = i * blocks + t
        prev = jnp.maximum(n - 1, 0)
        kwin = jnp.concatenate([
            ck_ref[0, pl.ds(pl.multiple_of(prev * SWA_BLOCK, SWA_BLOCK), SWA_BLOCK), :],
            ck_ref[0, pl.ds(pl.multiple_of(n * SWA_BLOCK, SWA_BLOCK), SWA_BLOCK), :]], axis=0)
        for kv in range(C_KV_HEADS):
            c = chains.index((t, kv))
            q2 = jnp.concatenate([
                cqT_ref[0, t // 2, (2 * kv + g) * HEAD_DIM:(2 * kv + g + 1) * HEAD_DIM,
                        (t % 2) * SWA_BLOCK:(t % 2 + 1) * SWA_BLOCK] for g in range(2)], axis=1)
            qz = jnp.concatenate([q2, zrows] if kv == 0 else [zrows, q2], axis=0)
            band = band0_ref[kv, 0] if t == 0 else band_ref[kv, 0]
            s = _dot(kwin, qz) + band
            s_ref[c] = s
            mx_ref[c] = _fold8(s, jnp.maximum)
    for c, (t, kv) in enumerate(chains):
        m = jnp.max(mx_ref[c], axis=0, keepdims=True)
        l8 = jnp.zeros((8, win), F32)
        for r in range(win // rows):
            p = jnp.exp2(s_ref[c, r * rows:(r + 1) * rows] - m)
            p_ref[c, r * rows:(r + 1) * rows] = p.astype(BF16)
            l8 = l8 + _fold8(p, jnp.add)
        mx_ref[c, 0:1] = 1.0 / (jnp.sum(l8, axis=0, keepdims=True) + jnp.exp2(sink_ref[kv] - m))


def _swa_back(i, blocks, cvT_ref, p_ref, mx_ref, gate_ref, out_ref):
    chains = [(t, kv) for t in range(blocks) for kv in range(C_KV_HEADS)]
    outs = {}
    for c, (t, kv) in enumerate(chains):
        n = i * blocks + t
        prev = jnp.maximum(n - 1, 0)
        vwin = jnp.concatenate([cvT_ref[0, prev, kv * HEAD_DIM:(kv + 1) * HEAD_DIM, :],
                                cvT_ref[0, n, kv * HEAD_DIM:(kv + 1) * HEAD_DIM, :]], axis=1)
        outs[t, kv] = _dot(vwin, p_ref[c]) * mx_ref[c, 0:1]
    for u in range(blocks * SWA_BLOCK // Q_TILE):
        head_rows = []
        for kv in range(C_KV_HEADS):
            for g in range(2):
                head_rows.append(jnp.concatenate(
                    [outs[t, kv][:, g * SWA_BLOCK:(g + 1) * SWA_BLOCK] for t in (2 * u, 2 * u + 1)], axis=1))
        o = jnp.concatenate(head_rows, axis=0).T
        gate = gate_ref[0, u * Q_TILE:(u + 1) * Q_TILE, :].astype(F32)
        out_ref[0, u * Q_TILE:(u + 1) * Q_TILE, :] = (o * gate).astype(BF16)


def _out_proj_kernel(x_ref, ya_ref, yc_ref, ybd_ref, wo_ref, out_ref):
    y = (_dot(ya_ref[0], wo_ref[0:256, :]) + _dot(ybd_ref[0, :, 0:256], wo_ref[256:512, :])
         + _dot(yc_ref[0], wo_ref[512:768, :]) + _dot(ybd_ref[0, :, 256:512], wo_ref[768:1024, :]))
    out_ref[0] = x_ref[0] + y


def _out_proj(x, ya, yc, ybd, wo):
    B, S, _ = x.shape
    tm = ROW_TILE
    rows = lambda width: pl.BlockSpec((1, tm, width), lambda b, s: (b, s, 0))
    in_specs = [rows(D_MODEL), rows(256), rows(256), rows(512),
                pl.BlockSpec((D_MODEL, D_MODEL), lambda b, s: (0, 0))]
    return pl.pallas_call(
        _out_proj_kernel, grid=(B, S // tm), in_specs=in_specs,
        out_specs=rows(D_MODEL),
        out_shape=jax.ShapeDtypeStruct((B, S, D_MODEL), F32),
        compiler_params=pltpu.CompilerParams(dimension_semantics=("parallel", "parallel"),
                                             vmem_limit_bytes=VMEM_LIMIT),
        name="out_proj",
    )(x, ya, yc, ybd, wo)


def _tables(S):
    tm = IN_TILE
    slopes_c, slopes_a = SLOPES_C, SLOPES_A
    pos = np.arange(S)

    kca = np.zeros((S, AUG), np.float32)
    blk, r = pos // MOBA_BLOCK, pos % MOBA_BLOCK
    kca[pos, COL_ONEHOT + blk] = 1.0
    kca[:, COL_POS] = r
    kca[:, COL_POS + 1] = r
    kca[:, COL_POS + 2] = blk
    kca[:, COL_POS + 3] = blk
    kca = jnp.asarray(kca, BF16)

    c = slopes_a * np.float32(LOG2E)
    hi = c.astype(BF16).astype(np.float32)
    lo = c - hi
    rows = np.stack([hi, lo, hi * MOBA_BLOCK, lo * MOBA_BLOCK], axis=-1)
    rows = np.pad(rows, ((0, 0), (0, AUG - COL_POS - 4)))
    crow_a = jnp.asarray(np.broadcast_to(rows[:, :, None], rows.shape + (Q_TILE,)), BF16)

    nb = S // MOBA_BLOCK
    avg = jnp.asarray((pos[None, :] // MOBA_BLOCK == np.arange(nb)[:, None]) / MOBA_BLOCK, BF16)
    kq = np.arange(MOBA_BLOCK)
    causal = np.where(kq[:, None] <= kq[None, :], 0.0, NEG_BIG)
    causal = jnp.asarray(np.concatenate([causal, np.zeros_like(causal)], axis=0), F32)
    u = np.arange(2 * SWA_BLOCK)[:, None]
    t = np.arange(SWA_BLOCK)[None, :]
    ok = (u > t) & (u <= t + SWA_BLOCK)
    dist = (SWA_BLOCK + t - u).astype(np.float32)
    sc = np.asarray(slopes_c, np.float32) * np.float32(LOG2E)
    band = []
    for kv in range(C_KV_HEADS):
        bias = np.concatenate([-sc[2 * kv] * dist, -sc[2 * kv + 1] * dist], axis=1)
        ok2 = np.concatenate([ok, ok], axis=1)
        band.append(np.stack([np.where(ok2 & (u >= SWA_BLOCK), bias, NEG_BIG),
                              np.where(ok2, bias, NEG_BIG)]))
    band = jnp.asarray(np.stack(band), F32)
    w = np.repeat(np.asarray(POOL_WINDOWS, np.float32), 64)[None, :]
    first = 1.0 / np.minimum(np.arange(tm, dtype=np.float32)[:, None] + 1.0, w)
    invcnt = jnp.asarray(np.stack([first, np.broadcast_to(1.0 / w, (tm, 256))]), F32)
    return dict(kca=kca, crow_a=crow_a, avg=avg, causal=causal, band=band, invcnt=invcnt)


def kernel(x, norm_g, w_in, w_out, a_q_norm, a_k_norm, pool_w, pool_scale, c_q_norm, c_k_norm, c_sinks, conv_w):
    B, S, _ = x.shape
    depth = norm_g.shape[0]
    assert S % max(ROW_TILE, IN_TILE) == 0 and S // MOBA_BLOCK <= 32
    tb = _tables(S)
    tm = IN_TILE
    d = np.arange(256)
    bd = jnp.asarray(d[:, None] // HEAD_DIM == d[None, :] // HEAD_DIM, BF16)
    qscale = QK_SCALE * LOG2E
    for l in range(depth):
        gq = jnp.broadcast_to(jnp.tile(a_q_norm[l] * qscale, N_HEADS)[:, None], (256, tm))
        gcq = jnp.broadcast_to(jnp.tile(c_q_norm[l] * qscale, N_HEADS)[:, None], (256, tm))
        gk = jnp.tile(a_k_norm[l], N_HEADS)[None, :]
        gck = jnp.tile(c_k_norm[l], C_KV_HEADS)[None, :]
        pw = jnp.zeros((256, 256), F32)
        for g in range(4):
            pw = pw.at[g * 64:(g + 1) * 64, g * 64:(g + 1) * 64].set(pool_w[l, g])
        qT, kaug, vT, cqT, ck, cvT, gates, ybd = _in_proj(
            x, norm_g[l][None, :], w_in[l].astype(BF16), bd, gq, gk, gcq, gck, tb['kca'],
            tb['invcnt'], pw.astype(BF16), pool_scale[l][None, :], conv_w[l])
        sink = jnp.repeat((c_sinks[l] * LOG2E).reshape(C_KV_HEADS, 2), SWA_BLOCK, axis=1)[:, None, :]
        ya, yc = _attention(qT, kaug, vT, tb['avg'], tb['crow_a'], tb['causal'],
                            cqT, ck, cvT, tb['band'], sink, gates)
        x = _out_proj(x, ya, yc, ybd, w_out[l].astype(BF16))
    return x
```

```python
import math

import jax
import jax.numpy as jnp
import numpy as np
from jax import lax
from jax.experimental import pallas as pl
from jax.experimental.pallas import tpu as pltpu

F32 = jnp.float32
BF16 = jnp.bfloat16

D_MODEL = 1024
HEAD_DIM = 64
N_HEADS = 4
C_KV_HEADS = 2
NORM_EPS = 1e-6
MOBA_BLOCK = 256
MOBA_TOPK = 3
SWA_BLOCK = 128
POOL_WINDOWS = (2, 4, 8, 16)
CONV_WIDTH = 3
HALO = 16

AUG = 128
COL_ONEHOT = 64
COL_POS = 96
NEG_BIG = -(2.0 ** 100)
LOG2E = math.log2(math.e)
QK_SCALE = HEAD_DIM ** -0.5
SLOPES = np.exp2(-(8.0 / (2 * N_HEADS)) * np.arange(1, 2 * N_HEADS + 1)).astype(np.float32)
SLOPES_C, SLOPES_A = SLOPES[:N_HEADS], SLOPES[N_HEADS:]

IN_TILE = 1024
Q_TILE = 256
VMEM_LIMIT = 56 * 1024 * 1024

SEG = dict(aq=0, ak=256, av=512, ag=768, bu=1024, bg=1280, cq=1536, ck=1792, cv=1920,
           cg=2048, dh=2304, db=2560, dc=2816, dg=3072)
IN_PROJ_WIDTH = 3328


def _dot(a, b):
    return jnp.dot(a, b, preferred_element_type=F32)


def _silu(v):
    h = 0.5 * v
    return h + h * jnp.tanh(h)


def _fold8(x, op):
    out = x[0:8]
    for g in range(1, x.shape[0] // 8):
        out = op(out, x[g * 8:(g + 1) * 8])
    return out


def _in_proj_kernel(x_ref, g_ref, w_ref, bd_ref, gq_ref, gk_ref, gcq_ref, gck_ref, kca_ref,
                    invcnt_ref, pw_ref, pscale_ref, cw_ref,
                    qT_ref, kaug_ref, vT_ref, cqT_ref, ck_ref, cvT_ref, gates_ref, ybd_ref,
                    hist_ref):
    tm = x_ref.shape[1]
    s_idx = pl.program_id(1)
    x = x_ref[0]
    ms = jnp.mean(x * x, axis=-1, keepdims=True)
    h = (x * lax.rsqrt(ms + NORM_EPS) * g_ref[...]).astype(BF16)

    def proj(lo, width):
        return _dot(h, w_ref[:, lo:lo + width])

    def norm_t(p, gain_ref):
        pt = p.T
        outs = []
        for hh in range(N_HEADS):
            ph = pt[hh * HEAD_DIM:(hh + 1) * HEAD_DIM]
            ss = jnp.sum(ph * ph, axis=0, keepdims=True) * (1.0 / HEAD_DIM)
            outs.append(ph * lax.rsqrt(ss + NORM_EPS))
        return (jnp.concatenate(outs, axis=0) * gain_ref[...]).astype(BF16)

    def norm_rows(p, gain_ref, width):
        ss = _dot((p * p).astype(BF16), bd_ref[0:width, 0:width]) * (1.0 / HEAD_DIM)
        return p * lax.rsqrt(ss + NORM_EPS) * gain_ref[...]

    def build_kaug(kn, kconst):
        lane = lax.broadcasted_iota(jnp.int32, (tm, AUG), 1)
        blocks = []
        for hh in range(N_HEADS):
            col = kn[:, (hh // 2) * AUG:(hh // 2 + 1) * AUG]
            if hh % 2 == 1:
                col = pltpu.roll(col, HEAD_DIM, axis=1)
            blocks.append(jnp.where(lane < HEAD_DIM, col, kconst).astype(BF16))
        return blocks


    @pl.when(s_idx == 0)
    def _():
        hist_ref[...] = jnp.zeros(hist_ref.shape, F32)

    back = lambda v, k: pltpu.roll(v, k, axis=0)

    pd = proj(SEG['dh'], 1024)
    dh, db, dc, dg = (pd[:, k * 256:(k + 1) * 256] for k in range(4))
    u = dc * dh
    ue = jnp.concatenate([hist_ref[1], u], axis=0)
    hist_ref[1] = u[tm - HALO:]
    conv = (cw_ref[0:1, :] * back(ue, 2) + cw_ref[1:2, :] * back(ue, 1) + cw_ref[2:3, :] * ue)[HALO:]
    ybd_ref[0, :, 256:512] = (db * conv * _silu(dg)).astype(BF16)

    pb = proj(SEG['bu'], 512)
    bu, bg = pb[:, 0:256], pb[:, 256:512]
    e = jnp.concatenate([hist_ref[0], bu], axis=0)
    hist_ref[0] = bu[tm - HALO:]
    a2 = e + back(e, 1)
    a4 = a2 + back(a2, 2)
    a4r = a4[:, 128:256]
    a8 = a4r + back(a4r, 4)
    a16 = a8 + back(a8, 8)
    lane = lax.broadcasted_iota(jnp.int32, (HALO + tm, 128), 1)
    sums = jnp.concatenate([jnp.where(lane < 64, a2[:, 0:128], a4[:, 0:128]),
                            jnp.where(lane < 64, a8, a16)], axis=1)
    pooled = sums[HALO:] * invcnt_ref[0] - bu
    yb = _dot(pooled.astype(BF16), pw_ref[...]) * pscale_ref[...] * _silu(bg)
    ybd_ref[0, :, 0:256] = yb.astype(BF16)

    nblk = tm // Q_TILE
    qT = norm_t(proj(SEG['aq'], 256), gq_ref)
    for t in range(nblk):
        qT_ref[0, t] = qT[:, t * Q_TILE:(t + 1) * Q_TILE]
    kn = norm_rows(proj(SEG['ak'], 256), gk_ref, 256)
    kaug = build_kaug(kn, kca_ref[...].astype(F32))
    for hh in range(N_HEADS):
        kaug_ref[0, hh] = kaug[hh]
    vT = proj(SEG['av'], 256).T.astype(BF16)
    for t in range(nblk):
        vT_ref[0, t] = vT[:, t * Q_TILE:(t + 1) * Q_TILE]
    cqT = norm_t(proj(SEG['cq'], 256), gcq_ref)
    for t in range(nblk):
        cqT_ref[0, t] = cqT[:, t * Q_TILE:(t + 1) * Q_TILE]
    ckv = proj(SEG['ck'], 256)
    ck_ref[0] = norm_rows(ckv[:, 0:128], gck_ref, 128).astype(BF16)
    cvT = ckv[:, 128:256].T.astype(BF16)
    for t in range(tm // SWA_BLOCK):
        cvT_ref[0, t] = cvT[:, t * SWA_BLOCK:(t + 1) * SWA_BLOCK]
    gates_ref[0, :, 0:256] = _silu(proj(SEG['ag'], 256)).astype(BF16)
    gates_ref[0, :, 256:512] = _silu(proj(SEG['cg'], 256)).astype(BF16)


def _in_proj(x, g, w, bd, gq, gk, gcq, gck, kca, invcnt, pw, pscale, cw):
    B, S, _ = x.shape
    tm = IN_TILE
    ns = S // tm
    nq = S // Q_TILE
    nkb = S // SWA_BLOCK
    const = lambda *shape: pl.BlockSpec(shape, lambda b, s: (0,) * len(shape))
    out_shape = (
        jax.ShapeDtypeStruct((B, nq, 256, Q_TILE), BF16),
        jax.ShapeDtypeStruct((B, N_HEADS, S, AUG), BF16),
        jax.ShapeDtypeStruct((B, nq, 256, Q_TILE), BF16),
        jax.ShapeDtypeStruct((B, nq, 256, Q_TILE), BF16),
        jax.ShapeDtypeStruct((B, S, 128), BF16),
        jax.ShapeDtypeStruct((B, nkb, 128, SWA_BLOCK), BF16),
        jax.ShapeDtypeStruct((B, S, 512), BF16),
        jax.ShapeDtypeStruct((B, S, 512), BF16),
    )
    r = tm // Q_TILE
    out_specs = (
        pl.BlockSpec((1, r, 256, Q_TILE), lambda b, s: (b, s, 0, 0)),
        pl.BlockSpec((1, N_HEADS, tm, AUG), lambda b, s: (b, 0, s, 0)),
        pl.BlockSpec((1, r, 256, Q_TILE), lambda b, s: (b, s, 0, 0)),
        pl.BlockSpec((1, r, 256, Q_TILE), lambda b, s: (b, s, 0, 0)),
        pl.BlockSpec((1, tm, 128), lambda b, s: (b, s, 0)),
        pl.BlockSpec((1, tm // SWA_BLOCK, 128, SWA_BLOCK), lambda b, s: (b, s, 0, 0)),
        pl.BlockSpec((1, tm, 512), lambda b, s: (b, s, 0)),
        pl.BlockSpec((1, tm, 512), lambda b, s: (b, s, 0)),
    )
    in_specs = [
        pl.BlockSpec((1, tm, D_MODEL), lambda b, s: (b, s, 0)),
        const(1, D_MODEL),
        const(D_MODEL, IN_PROJ_WIDTH),
        const(256, 256),
        const(256, tm), const(1, 256), const(256, tm), const(1, 128),
        pl.BlockSpec((tm, AUG), lambda b, s: (s, 0)),
        pl.BlockSpec((1, tm, 256), lambda b, s: (jnp.minimum(s, 1), 0, 0)),
        const(256, 256), const(1, 256), const(CONV_WIDTH, 256),
    ]
    return pl.pallas_call(
        _in_proj_kernel, grid=(B, ns), in_specs=in_specs, out_specs=out_specs, out_shape=out_shape,
        scratch_shapes=[pltpu.VMEM((2, HALO, 256), F32)],
        compiler_params=pltpu.CompilerParams(dimension_semantics=("parallel", "arbitrary"),
                                             vmem_limit_bytes=VMEM_LIMIT),
        name="in_proj",
    )(x, g, w, bd, gq, gk, gcq, gck, kca, invcnt, pw, pscale, cw)


def _attn_kernel(qT_ref, kaug_ref, vT_ref, avg_ref, crow_ref, causal_ref, gate_ref,
                 cqT_ref, ck_ref, cvT_ref, band0_ref, band_ref, sink_ref, cgate_ref,
                 x_ref, ybd_ref, wo_ref,
                 out_ref,
                 kmean_ref, qaug_ref, acc_ref, s_ref, p_ref, mx_ref, cs_ref, cp_ref, cmx_ref):
    i = pl.program_id(1)
    swa_blocks = Q_TILE // SWA_BLOCK
    S = kaug_ref.shape[2]
    nb = S // MOBA_BLOCK
    chunk = 1024

    @pl.when(i == 0)
    def _():
        for hh in range(N_HEADS):
            acc = jnp.zeros((nb, AUG), F32)
            for c in range(S // chunk):
                acc = acc + _dot(avg_ref[:, c * chunk:(c + 1) * chunk],
                                 kaug_ref[0, hh, c * chunk:(c + 1) * chunk, :])
            kmean_ref[hh] = acc

    row = lax.broadcasted_iota(jnp.int32, (nb, Q_TILE), 0).astype(F32)
    i_f = i.astype(F32)
    neg_inf = jnp.float32(-jnp.inf)
    heads = range(N_HEADS)
    vh = lambda hh, j: vT_ref[0, j, hh * HEAD_DIM:(hh + 1) * HEAD_DIM, :]
    kb = lambda hh, j: kaug_ref[0, hh, pl.ds(pl.multiple_of(j * MOBA_BLOCK, MOBA_BLOCK), MOBA_BLOCK), :]

    for hh in heads:
        qh = qT_ref[0, 0, hh * HEAD_DIM:(hh + 1) * HEAD_DIM, :]
        km = kmean_ref[hh][:, 0:HEAD_DIM]
        km_hi = km.astype(BF16)
        km_lo = (km - km_hi.astype(F32)).astype(BF16)
        bs = _dot(km_hi, qh) + _dot(km_lo, qh)
        xs = jnp.where(row < i_f, bs, neg_inf)
        sel = row == i_f
        for _ in range(MOBA_TOPK):
            mx = jnp.max(xs, axis=0, keepdims=True)
            first = jnp.min(jnp.where(xs == mx, row, jnp.float32(nb)), axis=0, keepdims=True)
            pick = row == first
            sel = sel | (pick & (mx > neg_inf))
            xs = jnp.where(pick, neg_inf, xs)
        negmask = jnp.where(sel, 0.0, NEG_BIG).astype(BF16)
        pad = jnp.zeros((COL_POS - COL_ONEHOT - nb, Q_TILE), BF16) if nb < 32 else None
        parts = [qh, negmask] + ([pad] if pad is not None else []) + [crow_ref[hh]]
        qaug_ref[hh] = jnp.concatenate(parts, axis=0)

    _swa_front(i, swa_blocks, cqT_ref, ck_ref, band0_ref, band_ref, sink_ref, cs_ref, cp_ref, cmx_ref)

    def blk(pos):
        past = jnp.minimum(i + 1, nb - 1)
        return jnp.where(pos <= 0, i, jnp.where(pos <= i, pos - 1, past))

    kblk = MOBA_BLOCK
    sub = 8
    rows = 64

    def scores(hh, j0, j1, mask=None):
        s = _dot(jnp.concatenate([kb(hh, j0), kb(hh, j1)], axis=0), qaug_ref[hh])
        if mask is not None:
            s = s + mask
        s_ref[hh] = s
        mx_ref[hh] = _fold8(s, jnp.maximum)

    def scores_step(t):
        for hh in heads:
            scores(hh, blk(2 * t), blk(2 * t + 1))

    def softmax_step(carry):
        ms, ls, _ = carry
        ms_new, ls_new, alphas_new = [], [], []
        for hh in heads:
            m_new = jnp.maximum(ms[hh], jnp.max(mx_ref[hh], axis=0, keepdims=True))
            alpha = jnp.exp2(ms[hh] - m_new)
            l8 = alpha * ls[hh]
            for c in range(2 * kblk // rows):
                p = jnp.exp2(s_ref[hh, c * rows:(c + 1) * rows] - m_new)
                p_ref[hh, c * rows:(c + 1) * rows] = p.astype(BF16)
                l8 = l8 + _fold8(p, jnp.add)
            ms_new.append(m_new)
            ls_new.append(l8)
            alphas_new.append(alpha)
        return tuple(ms_new), tuple(ls_new), tuple(alphas_new)

    def values_step(t, alphas):
        for hh in heads:
            acc_ref[hh] = (alphas[hh] * acc_ref[hh]
                           + _dot(vh(hh, blk(2 * t)), p_ref[hh, 0:kblk])
                           + _dot(vh(hh, blk(2 * t + 1)), p_ref[hh, kblk:2 * kblk]))

    def body(t, carry):
        values_step(t - 1, carry[2])
        carry = softmax_step(carry)
        scores_step(t + 1)
        return carry

    def drain(carry):
        values_step(steps - 2, carry[2])
        return softmax_step(carry)

    steps = (i + 2) // 2
    neg = jnp.full((1, Q_TILE), NEG_BIG, F32)
    zero8 = jnp.zeros((sub, Q_TILE), F32)
    acc_ref[...] = jnp.zeros(acc_ref.shape, F32)
    for hh in heads:
        scores(hh, i, blk(1), causal_ref[...])
    carry = softmax_step(((neg,) * N_HEADS, (zero8,) * N_HEADS, None))
    scores_step(1)
    carry = lax.fori_loop(1, steps - 1, body, carry)
    carry = lax.cond(steps >= 2, drain, lambda c: c, carry)
    ms, ls, alphas = carry
    values_step(steps - 1, alphas)
    yc = _swa_back(i, swa_blocks, cvT_ref, cp_ref, cmx_ref, cgate_ref)[0]
    part = (x_ref[0] + _dot(ybd_ref[0, :, 0:256], wo_ref[256:512, :]) + _dot(yc, wo_ref[512:768, :])
            + _dot(ybd_ref[0, :, 256:512], wo_ref[768:1024, :]))
    outs = [acc_ref[hh] / jnp.sum(ls[hh], axis=0, keepdims=True) for hh in heads]
    o = jnp.concatenate(outs, axis=0).T
    ya = (o * gate_ref[0].astype(F32)).astype(BF16)
    out_ref[0] = part + _dot(ya, wo_ref[0:256, :])


def _attention(qT, kaug, vT, avg, crow, causal, cqT, ck, cvT, band, sink, gates, x, ybd, wo):
    B, nq = qT.shape[0], qT.shape[1]
    S = kaug.shape[2]
    nb = S // MOBA_BLOCK
    nkb = S // SWA_BLOCK
    nchain = (Q_TILE // SWA_BLOCK) * C_KV_HEADS
    win = 2 * SWA_BLOCK
    tile = lambda col: pl.BlockSpec((1, Q_TILE, 256), lambda b, i: (b, i, col))
    in_specs = [
        pl.BlockSpec((1, 1, 256, Q_TILE), lambda b, i: (b, i, 0, 0)),
        pl.BlockSpec((1, N_HEADS, S, AUG), lambda b, i: (b, 0, 0, 0)),
        pl.BlockSpec((1, nq, 256, Q_TILE), lambda b, i: (b, 0, 0, 0)),
        pl.BlockSpec((nb, S), lambda b, i: (0, 0)),
        pl.BlockSpec((N_HEADS, AUG - COL_POS, Q_TILE), lambda b, i: (0, 0, 0)),
        pl.BlockSpec((2 * MOBA_BLOCK, Q_TILE), lambda b, i: (0, 0)),
        tile(0),
        pl.BlockSpec((1, 1, 256, Q_TILE), lambda b, i: (b, i, 0, 0)),
        pl.BlockSpec((1, S, 128), lambda b, i: (b, 0, 0)),
        pl.BlockSpec((1, nkb, 128, SWA_BLOCK), lambda b, i: (b, 0, 0, 0)),
        pl.BlockSpec((C_KV_HEADS, 1, win, win), lambda b, i: (0, jnp.minimum(i, 1), 0, 0)),
        pl.BlockSpec((C_KV_HEADS, 1, win, win), lambda b, i: (0, 1, 0, 0)),
        pl.BlockSpec((C_KV_HEADS, 1, win), lambda b, i: (0, 0, 0)),
        tile(1),
        pl.BlockSpec((1, Q_TILE, D_MODEL), lambda b, i: (b, i, 0)),
        pl.BlockSpec((1, Q_TILE, 512), lambda b, i: (b, i, 0)),
        pl.BlockSpec((D_MODEL, D_MODEL), lambda b, i: (0, 0)),
    ]
    return pl.pallas_call(
        _attn_kernel, grid=(B, nq), in_specs=in_specs,
        out_specs=pl.BlockSpec((1, Q_TILE, D_MODEL), lambda b, i: (b, i, 0)),
        out_shape=jax.ShapeDtypeStruct((B, S, D_MODEL), F32),
        scratch_shapes=[pltpu.VMEM((N_HEADS, nb, AUG), F32),
                        pltpu.VMEM((N_HEADS, AUG, Q_TILE), BF16),
                        pltpu.VMEM((N_HEADS, HEAD_DIM, Q_TILE), F32),
                        pltpu.VMEM((N_HEADS, 2 * MOBA_BLOCK, Q_TILE), F32),
                        pltpu.VMEM((N_HEADS, 2 * MOBA_BLOCK, Q_TILE), BF16),
                        pltpu.VMEM((N_HEADS, 8, Q_TILE), F32),
                        pltpu.VMEM((nchain, win, win), F32),
                        pltpu.VMEM((nchain, win, win), BF16),
                        pltpu.VMEM((nchain, 8, win), F32)],
        compiler_params=pltpu.CompilerParams(dimension_semantics=("parallel", "arbitrary"),
                                             vmem_limit_bytes=VMEM_LIMIT),
        name="attention",
    )(qT, kaug, vT, avg, crow, causal, gates, cqT, ck, cvT, band, band, sink, gates, x, ybd, wo)


def _swa_front(i, blocks, cqT_ref, ck_ref, band0_ref, band_ref, sink_ref, s_ref, p_ref, mx_ref):
    win = 2 * SWA_BLOCK
    rows = 64
    zrows = jnp.zeros((HEAD_DIM, win), BF16)
    chains = [(t, kv) for t in range(blocks) for kv in range(C_KV_HEADS)]
    for t in range(blocks):
        n = i * blocks + t
        prev = jnp.maximum(n - 1, 0)
        kwin = jnp.concatenate([
            ck_ref[0, pl.ds(pl.multiple_of(prev * SWA_BLOCK, SWA_BLOCK), SWA_BLOCK), :],
            ck_ref[0, pl.ds(pl.multiple_of(n * SWA_BLOCK, SWA_BLOCK), SWA_BLOCK), :]], axis=0)
        for kv in range(C_KV_HEADS):
            c = chains.index((t, kv))
            q2 = jnp.concatenate([
                cqT_ref[0, t // 2, (2 * kv + g) * HEAD_DIM:(2 * kv + g + 1) * HEAD_DIM,
                        (t % 2) * SWA_BLOCK:(t % 2 + 1) * SWA_BLOCK] for g in range(2)], axis=1)
            qz = jnp.concatenate([q2, zrows] if kv == 0 else [zrows, q2], axis=0)
            band = band0_ref[kv, 0] if t == 0 else band_ref[kv, 0]
            s = _dot(kwin, qz) + band
            s_ref[c] = s
            mx_ref[c] = _fold8(s, jnp.maximum)
    for c, (t, kv) in enumerate(chains):
        m = jnp.max(mx_ref[c], axis=0, keepdims=True)
        l8 = jnp.zeros((8, win), F32)
        for r in range(win // rows):
            p = jnp.exp2(s_ref[c, r * rows:(r + 1) * rows] - m)
            p_ref[c, r * rows:(r + 1) * rows] = p.astype(BF16)
            l8 = l8 + _fold8(p, jnp.add)
        mx_ref[c, 0:1] = 1.0 / (jnp.sum(l8, axis=0, keepdims=True) + jnp.exp2(sink_ref[kv] - m))


def _swa_back(i, blocks, cvT_ref, p_ref, mx_ref, gate_ref):
    chains = [(t, kv) for t in range(blocks) for kv in range(C_KV_HEADS)]
    outs = {}
    for c, (t, kv) in enumerate(chains):
        n = i * blocks + t
        prev = jnp.maximum(n - 1, 0)
        vwin = jnp.concatenate([cvT_ref[0, prev, kv * HEAD_DIM:(kv + 1) * HEAD_DIM, :],
                                cvT_ref[0, n, kv * HEAD_DIM:(kv + 1) * HEAD_DIM, :]], axis=1)
        outs[t, kv] = _dot(vwin, p_ref[c]) * mx_ref[c, 0:1]
    tiles = []
    for u in range(blocks * SWA_BLOCK // Q_TILE):
        head_rows = []
        for kv in range(C_KV_HEADS):
            for g in range(2):
                head_rows.append(jnp.concatenate(
                    [outs[t, kv][:, g * SWA_BLOCK:(g + 1) * SWA_BLOCK] for t in (2 * u, 2 * u + 1)], axis=1))
        o = jnp.concatenate(head_rows, axis=0).T
        gate = gate_ref[0, u * Q_TILE:(u + 1) * Q_TILE, :].astype(F32)
        tiles.append((o * gate).astype(BF16))
    return tiles


def _tables(S):
    tm = IN_TILE
    slopes_c, slopes_a = SLOPES_C, SLOPES_A
    pos = np.arange(S)

    kca = np.zeros((S, AUG), np.float32)
    blk, r = pos // MOBA_BLOCK, pos % MOBA_BLOCK
    kca[pos, COL_ONEHOT + blk] = 1.0
    kca[:, COL_POS] = r
    kca[:, COL_POS + 1] = r
    kca[:, COL_POS + 2] = blk
    kca[:, COL_POS + 3] = blk
    kca = jnp.asarray(kca, BF16)

    c = slopes_a * np.float32(LOG2E)
    hi = c.astype(BF16).astype(np.float32)
    lo = c - hi
    rows = np.stack([hi, lo, hi * MOBA_BLOCK, lo * MOBA_BLOCK], axis=-1)
    rows = np.pad(rows, ((0, 0), (0, AUG - COL_POS - 4)))
    crow_a = jnp.asarray(np.broadcast_to(rows[:, :, None], rows.shape + (Q_TILE,)), BF16)

    nb = S // MOBA_BLOCK
    avg = jnp.asarray((pos[None, :] // MOBA_BLOCK == np.arange(nb)[:, None]) / MOBA_BLOCK, BF16)
    kq = np.arange(MOBA_BLOCK)
    causal = np.where(kq[:, None] <= kq[None, :], 0.0, NEG_BIG)
    causal = jnp.asarray(np.concatenate([causal, np.zeros_like(causal)], axis=0), F32)
    u = np.arange(2 * SWA_BLOCK)[:, None]
    t = np.arange(SWA_BLOCK)[None, :]
    ok = (u > t) & (u <= t + SWA_BLOCK)
    dist = (SWA_BLOCK + t - u).astype(np.float32)
    sc = np.asarray(slopes_c, np.float32) * np.float32(LOG2E)
    band = []
    for kv in range(C_KV_HEADS):
        bias = np.concatenate([-sc[2 * kv] * dist, -sc[2 * kv + 1] * dist], axis=1)
        ok2 = np.concatenate([ok, ok], axis=1)
        band.append(np.stack([np.where(ok2 & (u >= SWA_BLOCK), bias, NEG_BIG),
                              np.where(ok2, bias, NEG_BIG)]))
    band = jnp.asarray(np.stack(band), F32)
    w = np.repeat(np.asarray(POOL_WINDOWS, np.float32), 64)[None, :]
    first = 1.0 / np.minimum(np.arange(tm, dtype=np.float32)[:, None] + 1.0, w)
    invcnt = jnp.asarray(np.stack([first, np.broadcast_to(1.0 / w, (tm, 256))]), F32)
    return dict(kca=kca, crow_a=crow_a, avg=avg, causal=causal, band=band, invcnt=invcnt)


def kernel(x, norm_g, w_in, w_out, a_q_norm, a_k_norm, pool_w, pool_scale, c_q_norm, c_k_norm, c_sinks, conv_w):
    B, S, _ = x.shape
    depth = norm_g.shape[0]
    assert S % IN_TILE == 0 and S // MOBA_BLOCK <= 32
    tb = _tables(S)
    tm = IN_TILE
    d = np.arange(256)
    bd = jnp.asarray(d[:, None] // HEAD_DIM == d[None, :] // HEAD_DIM, BF16)
    qscale = QK_SCALE * LOG2E
    for l in range(depth):
        gq = jnp.broadcast_to(jnp.tile(a_q_norm[l] * qscale, N_HEADS)[:, None], (256, tm))
        gcq = jnp.broadcast_to(jnp.tile(c_q_norm[l] * qscale, N_HEADS)[:, None], (256, tm))
        gk = jnp.tile(a_k_norm[l], N_HEADS)[None, :]
        gck = jnp.tile(c_k_norm[l], C_KV_HEADS)[None, :]
        pw = jnp.zeros((256, 256), F32)
        for g in range(4):
            pw = pw.at[g * 64:(g + 1) * 64, g * 64:(g + 1) * 64].set(pool_w[l, g])
        qT, kaug, vT, cqT, ck, cvT, gates, ybd = _in_proj(
            x, norm_g[l][None, :], w_in[l].astype(BF16), bd, gq, gk, gcq, gck, tb['kca'],
            tb['invcnt'], pw.astype(BF16), pool_scale[l][None, :], conv_w[l])
        sink = jnp.repeat((c_sinks[l] * LOG2E).reshape(C_KV_HEADS, 2), SWA_BLOCK, axis=1)[:, None, :]
        x = _attention(qT, kaug, vT, tb['avg'], tb['crow_a'], tb['causal'],
                       cqT, ck, cvT, tb['band'], sink, gates,
                       x, ybd, w_out[l].astype(BF16))
    return x
```

```python
import math

import jax
import jax.numpy as jnp
import numpy as np
from jax import lax
from jax.experimental import pallas as pl
from jax.experimental.pallas import tpu as pltpu

F32 = jnp.float32
BF16 = jnp.bfloat16

D_MODEL = 1024
HEAD_DIM = 64
N_HEADS = 4
C_KV_HEADS = 2
NORM_EPS = 1e-6
MOBA_BLOCK = 256
MOBA_TOPK = 3
SWA_BLOCK = 128
POOL_WINDOWS = (2, 4, 8, 16)
CONV_WIDTH = 3
HALO = 16

AUG = 128
COL_ONEHOT = 64
COL_POS = 96
NEG_BIG = -(2.0 ** 100)
LOG2E = math.log2(math.e)
QK_SCALE = HEAD_DIM ** -0.5
SLOPES = np.exp2(-(8.0 / (2 * N_HEADS)) * np.arange(1, 2 * N_HEADS + 1)).astype(np.float32)
SLOPES_C, SLOPES_A = SLOPES[:N_HEADS], SLOPES[N_HEADS:]

IN_TILE = 1024
Q_TILE = 256
VMEM_LIMIT = 56 * 1024 * 1024

SEG = dict(aq=0, ak=256, av=512, ag=768, bu=1024, bg=1280, cq=1536, ck=1792, cv=1920,
           cg=2048, dh=2304, db=2560, dc=2816, dg=3072)
IN_PROJ_WIDTH = 3328


def _dot(a, b):
    return jnp.dot(a, b, preferred_element_type=F32)


def _silu(v):
    h = 0.5 * v
    return h + h * jnp.tanh(h)


def _fold8(x, op):
    out = x[0:8]
    for g in range(1, x.shape[0] // 8):
        out = op(out, x[g * 8:(g + 1) * 8])
    return out


def _in_proj_kernel(x_ref, g_ref, w_ref, bd_ref, gq_ref, gk_ref, gcq_ref, gck_ref, kca_ref,
                    invcnt_ref, pw_ref, pscale_ref, cw_ref,
                    qT_ref, kaug_ref, vT_ref, cqT_ref, ck_ref, cvT_ref, gates_ref, ybd_ref,
                    hist_ref):
    tm = x_ref.shape[1]
    s_idx = pl.program_id(1)
    x = x_ref[0]
    ms = jnp.mean(x * x, axis=-1, keepdims=True)
    h = (x * lax.rsqrt(ms + NORM_EPS) * g_ref[...]).astype(BF16)

    def proj(lo, width):
        return _dot(h, w_ref[:, lo:lo + width])

    def norm_t(p, gain_ref):
        pt = p.T
        outs = []
        for hh in range(N_HEADS):
            ph = pt[hh * HEAD_DIM:(hh + 1) * HEAD_DIM]
            ss = jnp.sum(ph * ph, axis=0, keepdims=True) * (1.0 / HEAD_DIM)
            outs.append(ph * lax.rsqrt(ss + NORM_EPS))
        return (jnp.concatenate(outs, axis=0) * gain_ref[...]).astype(BF16)

    def norm_rows(p, gain_ref, width):
        ss = _dot((p * p).astype(BF16), bd_ref[0:width, 0:width]) * (1.0 / HEAD_DIM)
        return p * lax.rsqrt(ss + NORM_EPS) * gain_ref[...]

    def build_kaug(kn, kconst):
        lane = lax.broadcasted_iota(jnp.int32, (tm, AUG), 1)
        blocks = []
        for hh in range(N_HEADS):
            col = kn[:, (hh // 2) * AUG:(hh // 2 + 1) * AUG]
            if hh % 2 == 1:
                col = pltpu.roll(col, HEAD_DIM, axis=1)
            blocks.append(jnp.where(lane < HEAD_DIM, col, kconst).astype(BF16))
        return blocks


    @pl.when(s_idx == 0)
    def _():
        hist_ref[...] = jnp.zeros(hist_ref.shape, F32)

    back = lambda v, k: pltpu.roll(v, k, axis=0)

    pd = proj(SEG['dh'], 1024)
    dh, db, dc, dg = (pd[:, k * 256:(k + 1) * 256] for k in range(4))
    u = dc * dh
    ue = jnp.concatenate([hist_ref[1], u], axis=0)
    hist_ref[1] = u[tm - HALO:]
    conv = (cw_ref[0:1, :] * back(ue, 2) + cw_ref[1:2, :] * back(ue, 1) + cw_ref[2:3, :] * ue)[HALO:]
    ybd_ref[0, :, 256:512] = (db * conv * _silu(dg)).astype(BF16)

    pb = proj(SEG['bu'], 512)
    bu, bg = pb[:, 0:256], pb[:, 256:512]
    e = jnp.concatenate([hist_ref[0], bu], axis=0)
    hist_ref[0] = bu[tm - HALO:]
    a2 = e + back(e, 1)
    a4 = a2 + back(a2, 2)
    a4r = a4[:, 128:256]
    a8 = a4r + back(a4r, 4)
    a16 = a8 + back(a8, 8)
    lane = lax.broadcasted_iota(jnp.int32, (HALO + tm, 128), 1)
    sums = jnp.concatenate([jnp.where(lane < 64, a2[:, 0:128], a4[:, 0:128]),
                            jnp.where(lane < 64, a8, a16)], axis=1)
    pooled = sums[HALO:] * invcnt_ref[0] - bu
    yb = _dot(pooled.astype(BF16), pw_ref[...]) * pscale_ref[...] * _silu(bg)
    ybd_ref[0, :, 0:256] = yb.astype(BF16)

    gates_ref[0, :, 0:256] = _silu(proj(SEG['ag'], 256)).astype(BF16)
    gates_ref[0, :, 256:512] = _silu(proj(SEG['cg'], 256)).astype(BF16)

    nblk = tm // Q_TILE
    qT = norm_t(proj(SEG['aq'], 256), gq_ref)
    for t in range(nblk):
        qT_ref[0, t] = qT[:, t * Q_TILE:(t + 1) * Q_TILE]
    kn = norm_rows(proj(SEG['ak'], 256), gk_ref, 256)
    kaug = build_kaug(kn, kca_ref[...].astype(F32))
    for hh in range(N_HEADS):
        kaug_ref[0, hh] = kaug[hh]
    vT = proj(SEG['av'], 256).T.astype(BF16)
    for t in range(nblk):
        vT_ref[0, t] = vT[:, t * Q_TILE:(t + 1) * Q_TILE]
    cqT = norm_t(proj(SEG['cq'], 256), gcq_ref)
    for t in range(nblk):
        cqT_ref[0, t] = cqT[:, t * Q_TILE:(t + 1) * Q_TILE]
    ckv = proj(SEG['ck'], 256)
    ck_ref[0] = norm_rows(ckv[:, 0:128], gck_ref, 128).astype(BF16)
    cvT = ckv[:, 128:256].T.astype(BF16)
    for t in range(tm // SWA_BLOCK):
        cvT_ref[0, t] = cvT[:, t * SWA_BLOCK:(t + 1) * SWA_BLOCK]


def _in_proj(x, g, w, bd, gq, gk, gcq, gck, kca, invcnt, pw, pscale, cw):
    B, S, _ = x.shape
    tm = IN_TILE
    ns = S // tm
    nq = S // Q_TILE
    nkb = S // SWA_BLOCK
    const = lambda *shape: pl.BlockSpec(shape, lambda b, s: (0,) * len(shape))
    out_shape = (
        jax.ShapeDtypeStruct((B, nq, 256, Q_TILE), BF16),
        jax.ShapeDtypeStruct((B, N_HEADS, S, AUG), BF16),
        jax.ShapeDtypeStruct((B, nq, 256, Q_TILE), BF16),
        jax.ShapeDtypeStruct((B, nq, 256, Q_TILE), BF16),
        jax.ShapeDtypeStruct((B, S, 128), BF16),
        jax.ShapeDtypeStruct((B, nkb, 128, SWA_BLOCK), BF16),
        jax.ShapeDtypeStruct((B, S, 512), BF16),
        jax.ShapeDtypeStruct((B, S, 512), BF16),
    )
    r = tm // Q_TILE
    out_specs = (
        pl.BlockSpec((1, r, 256, Q_TILE), lambda b, s: (b, s, 0, 0)),
        pl.BlockSpec((1, N_HEADS, tm, AUG), lambda b, s: (b, 0, s, 0)),
        pl.BlockSpec((1, r, 256, Q_TILE), lambda b, s: (b, s, 0, 0)),
        pl.BlockSpec((1, r, 256, Q_TILE), lambda b, s: (b, s, 0, 0)),
        pl.BlockSpec((1, tm, 128), lambda b, s: (b, s, 0)),
        pl.BlockSpec((1, tm // SWA_BLOCK, 128, SWA_BLOCK), lambda b, s: (b, s, 0, 0)),
        pl.BlockSpec((1, tm, 512), lambda b, s: (b, s, 0)),
        pl.BlockSpec((1, tm, 512), lambda b, s: (b, s, 0)),
    )
    in_specs = [
        pl.BlockSpec((1, tm, D_MODEL), lambda b, s: (b, s, 0)),
        const(1, D_MODEL),
        const(D_MODEL, IN_PROJ_WIDTH),
        const(256, 256),
        const(256, tm), const(1, 256), const(256, tm), const(1, 128),
        pl.BlockSpec((tm, AUG), lambda b, s: (s, 0)),
        pl.BlockSpec((1, tm, 256), lambda b, s: (jnp.minimum(s, 1), 0, 0)),
        const(256, 256), const(1, 256), const(CONV_WIDTH, 256),
    ]
    return pl.pallas_call(
        _in_proj_kernel, grid=(B, ns), in_specs=in_specs, out_specs=out_specs, out_shape=out_shape,
        scratch_shapes=[pltpu.VMEM((2, HALO, 256), F32)],
        compiler_params=pltpu.CompilerParams(dimension_semantics=("parallel", "arbitrary"),
                                             vmem_limit_bytes=VMEM_LIMIT),
        name="in_proj",
    )(x, g, w, bd, gq, gk, gcq, gck, kca, invcnt, pw, pscale, cw)


def _attn_kernel(qT_ref, kaug_ref, vT_ref, avg_ref, crow_ref, causal_ref, gate_ref,
                 cqT_ref, ck_ref, cvT_ref, band0_ref, band_ref, sink_ref, cgate_ref,
                 x_ref, ybd_ref, wo_ref,
                 out_ref,
                 kmean_ref, qaug_ref, acc_ref, s_ref, p_ref, mx_ref, cs_ref, cp_ref, cmx_ref, st_ref):
    g = pl.program_id(1)
    n_tiles = pl.num_programs(1) - 1
    i = jnp.minimum(g, n_tiles - 1)
    j = jnp.maximum(g - 1, 0)
    swa_blocks = Q_TILE // SWA_BLOCK
    S = kaug_ref.shape[2]
    nb = S // MOBA_BLOCK
    chunk = 1024
    kblk = MOBA_BLOCK
    sub = 8
    rows = 64
    ALPHA, LSUM = 0, 1

    @pl.when(g == 0)
    def _():
        for hh in range(N_HEADS):
            acc = jnp.zeros((nb, AUG), F32)
            for c in range(S // chunk):
                acc = acc + _dot(avg_ref[:, c * chunk:(c + 1) * chunk],
                                 kaug_ref[0, hh, c * chunk:(c + 1) * chunk, :])
            kmean_ref[hh] = acc
        acc_ref[...] = jnp.zeros(acc_ref.shape, F32)
        p_ref[...] = jnp.zeros(p_ref.shape, BF16)
        cp_ref[...] = jnp.zeros(cp_ref.shape, BF16)
        cmx_ref[...] = jnp.ones(cmx_ref.shape, F32)
        st_ref[...] = jnp.ones(st_ref.shape, F32)

    row = lax.broadcasted_iota(jnp.int32, (nb, Q_TILE), 0).astype(F32)
    i_f = i.astype(F32)
    neg_inf = jnp.float32(-jnp.inf)
    heads = range(N_HEADS)
    vh = lambda hh, j: vT_ref[0, j, hh * HEAD_DIM:(hh + 1) * HEAD_DIM, :]
    kb = lambda hh, j: kaug_ref[0, hh, pl.ds(pl.multiple_of(j * MOBA_BLOCK, MOBA_BLOCK), MOBA_BLOCK), :]

    def blk(tile, pos):
        past = jnp.minimum(tile + 1, nb - 1)
        return jnp.where(pos <= 0, tile, jnp.where(pos <= tile, pos - 1, past))

    def values_step(tile, t, alphas):
        for hh in heads:
            acc_ref[hh] = (alphas[hh] * acc_ref[hh]
                           + _dot(vh(hh, blk(tile, 2 * t)), p_ref[hh, 0:kblk])
                           + _dot(vh(hh, blk(tile, 2 * t + 1)), p_ref[hh, kblk:2 * kblk]))

    values_step(j, (j + 2) // 2 - 1, [st_ref[hh, ALPHA, 0:1] for hh in heads])
    yc = _swa_back(j, swa_blocks, cvT_ref, cp_ref, cmx_ref, cgate_ref)[0]
    part = (x_ref[0] + _dot(ybd_ref[0, :, 0:256], wo_ref[256:512, :]) + _dot(yc, wo_ref[512:768, :])
            + _dot(ybd_ref[0, :, 256:512], wo_ref[768:1024, :]))
    outs = [acc_ref[hh] / jnp.sum(st_ref[hh, LSUM], axis=0, keepdims=True) for hh in heads]
    o = jnp.concatenate(outs, axis=0).T
    ya = (o * gate_ref[0].astype(F32)).astype(BF16)
    out_ref[0] = part + _dot(ya, wo_ref[0:256, :])


    for hh in heads:
        qh = qT_ref[0, 0, hh * HEAD_DIM:(hh + 1) * HEAD_DIM, :]
        km = kmean_ref[hh][:, 0:HEAD_DIM]
        km_hi = km.astype(BF16)
        km_lo = (km - km_hi.astype(F32)).astype(BF16)
        bs = _dot(km_hi, qh) + _dot(km_lo, qh)
        xs = jnp.where(row < i_f, bs, neg_inf)
        sel = row == i_f
        for _ in range(MOBA_TOPK):
            mx = jnp.max(xs, axis=0, keepdims=True)
            first = jnp.min(jnp.where(xs == mx, row, jnp.float32(nb)), axis=0, keepdims=True)
            pick = row == first
            sel = sel | (pick & (mx > neg_inf))
            xs = jnp.where(pick, neg_inf, xs)
        negmask = jnp.where(sel, 0.0, NEG_BIG).astype(BF16)
        pad = jnp.zeros((COL_POS - COL_ONEHOT - nb, Q_TILE), BF16) if nb < 32 else None
        parts = [qh, negmask] + ([pad] if pad is not None else []) + [crow_ref[hh]]
        qaug_ref[hh] = jnp.concatenate(parts, axis=0)

    _swa_front(i, swa_blocks, cqT_ref, ck_ref, band0_ref, band_ref, sink_ref, cs_ref, cp_ref, cmx_ref)

    def scores(hh, j0, j1, mask=None):
        s = _dot(jnp.concatenate([kb(hh, j0), kb(hh, j1)], axis=0), qaug_ref[hh])
        if mask is not None:
            s = s + mask
        s_ref[hh] = s
        mx_ref[hh] = _fold8(s, jnp.maximum)

    def scores_step(t):
        for hh in heads:
            scores(hh, blk(i, 2 * t), blk(i, 2 * t + 1))

    def softmax_step(carry):
        ms, ls, _ = carry
        ms_new, ls_new, alphas_new = [], [], []
        for hh in heads:
            m_new = jnp.maximum(ms[hh], jnp.max(mx_ref[hh], axis=0, keepdims=True))
            alpha = jnp.exp2(ms[hh] - m_new)
            l8 = alpha * ls[hh]
            for c in range(2 * kblk // rows):
                p = jnp.exp2(s_ref[hh, c * rows:(c + 1) * rows] - m_new)
                p_ref[hh, c * rows:(c + 1) * rows] = p.astype(BF16)
                l8 = l8 + _fold8(p, jnp.add)
            ms_new.append(m_new)
            ls_new.append(l8)
            alphas_new.append(alpha)
        return tuple(ms_new), tuple(ls_new), tuple(alphas_new)

    def body(t, carry):
        values_step(i, t - 1, carry[2])
        carry = softmax_step(carry)
        scores_step(t + 1)
        return carry

    def drain(carry):
        values_step(i, steps - 2, carry[2])
        return softmax_step(carry)

    steps = jnp.where(g < n_tiles, (i + 2) // 2, 0)
    neg = jnp.full((1, Q_TILE), NEG_BIG, F32)
    zero8 = jnp.zeros((sub, Q_TILE), F32)
    acc_ref[...] = jnp.zeros(acc_ref.shape, F32)
    for hh in heads:
        scores(hh, i, blk(i, 1), causal_ref[...])
    carry = softmax_step(((neg,) * N_HEADS, (zero8,) * N_HEADS, None))
    scores_step(1)
    carry = lax.fori_loop(1, steps - 1, body, carry)
    ms, ls, alphas = lax.cond(steps >= 2, drain, lambda c: c, carry)
    for hh in heads:
        st_ref[hh, ALPHA] = jnp.broadcast_to(alphas[hh], (sub, Q_TILE))
        st_ref[hh, LSUM] = ls[hh]


def _attention(qT, kaug, vT, avg, crow, causal, cqT, ck, cvT, band, sink, gates, x, ybd, wo):
    B, nq = qT.shape[0], qT.shape[1]
    S = kaug.shape[2]
    nb = S // MOBA_BLOCK
    nkb = S // SWA_BLOCK
    nchain = (Q_TILE // SWA_BLOCK) * C_KV_HEADS
    win = 2 * SWA_BLOCK
    cur = lambda g: jnp.minimum(g, nq - 1)
    prev = lambda g: jnp.maximum(g - 1, 0)
    tile = lambda col: pl.BlockSpec((1, Q_TILE, 256), lambda b, g: (b, prev(g), col))
    in_specs = [
        pl.BlockSpec((1, 1, 256, Q_TILE), lambda b, g: (b, cur(g), 0, 0)),
        pl.BlockSpec((1, N_HEADS, S, AUG), lambda b, i: (b, 0, 0, 0)),
        pl.BlockSpec((1, nq, 256, Q_TILE), lambda b, i: (b, 0, 0, 0)),
        pl.BlockSpec((nb, S), lambda b, i: (0, 0)),
        pl.BlockSpec((N_HEADS, AUG - COL_POS, Q_TILE), lambda b, i: (0, 0, 0)),
        pl.BlockSpec((2 * MOBA_BLOCK, Q_TILE), lambda b, i: (0, 0)),
        tile(0),
        pl.BlockSpec((1, 1, 256, Q_TILE), lambda b, g: (b, cur(g), 0, 0)),
        pl.BlockSpec((1, S, 128), lambda b, i: (b, 0, 0)),
        pl.BlockSpec((1, nkb, 128, SWA_BLOCK), lambda b, i: (b, 0, 0, 0)),
        pl.BlockSpec((C_KV_HEADS, 1, win, win), lambda b, i: (0, jnp.minimum(i, 1), 0, 0)),
        pl.BlockSpec((C_KV_HEADS, 1, win, win), lambda b, i: (0, 1, 0, 0)),
        pl.BlockSpec((C_KV_HEADS, 1, win), lambda b, i: (0, 0, 0)),
        tile(1),
        pl.BlockSpec((1, Q_TILE, D_MODEL), lambda b, g: (b, prev(g), 0)),
        pl.BlockSpec((1, Q_TILE, 512), lambda b, g: (b, prev(g), 0)),
        pl.BlockSpec((D_MODEL, D_MODEL), lambda b, i: (0, 0)),
    ]
    return pl.pallas_call(
        _attn_kernel, grid=(B, nq + 1), in_specs=in_specs,
        out_specs=pl.BlockSpec((1, Q_TILE, D_MODEL), lambda b, g: (b, prev(g), 0)),
        out_shape=jax.ShapeDtypeStruct((B, S, D_MODEL), F32),
        scratch_shapes=[pltpu.VMEM((N_HEADS, nb, AUG), F32),
                        pltpu.VMEM((N_HEADS, AUG, Q_TILE), BF16),
                        pltpu.VMEM((N_HEADS, HEAD_DIM, Q_TILE), F32),
                        pltpu.VMEM((N_HEADS, 2 * MOBA_BLOCK, Q_TILE), F32),
                        pltpu.VMEM((N_HEADS, 2 * MOBA_BLOCK, Q_TILE), BF16),
                        pltpu.VMEM((N_HEADS, 8, Q_TILE), F32),
                        pltpu.VMEM((nchain, win, win), F32),
                        pltpu.VMEM((nchain, win, win), BF16),
                        pltpu.VMEM((nchain, 8, win), F32),
                        pltpu.VMEM((N_HEADS, 2, 8, Q_TILE), F32)],
        compiler_params=pltpu.CompilerParams(dimension_semantics=("parallel", "arbitrary"),
                                             vmem_limit_bytes=VMEM_LIMIT),
        name="attention",
    )(qT, kaug, vT, avg, crow, causal, gates, cqT, ck, cvT, band, band, sink, gates, x, ybd, wo)


def _swa_front(i, blocks, cqT_ref, ck_ref, band0_ref, band_ref, sink_ref, s_ref, p_ref, mx_ref):
    win = 2 * SWA_BLOCK
    rows = 64
    zrows = jnp.zeros((HEAD_DIM, win), BF16)
    chains = [(t, kv) for t in range(blocks) for kv in range(C_KV_HEADS)]
    for t in range(blocks):
        n = i * blocks + t
        prev = jnp.maximum(n - 1, 0)
        kwin = jnp.concatenate([
            ck_ref[0, pl.ds(pl.multiple_of(prev * SWA_BLOCK, SWA_BLOCK), SWA_BLOCK), :],
            ck_ref[0, pl.ds(pl.multiple_of(n * SWA_BLOCK, SWA_BLOCK), SWA_BLOCK), :]], axis=0)
        for kv in range(C_KV_HEADS):
            c = chains.index((t, kv))
            q2 = jnp.concatenate([
                cqT_ref[0, t // 2, (2 * kv + g) * HEAD_DIM:(2 * kv + g + 1) * HEAD_DIM,
                        (t % 2) * SWA_BLOCK:(t % 2 + 1) * SWA_BLOCK] for g in range(2)], axis=1)
            qz = jnp.concatenate([q2, zrows] if kv == 0 else [zrows, q2], axis=0)
            band = band0_ref[kv, 0] if t == 0 else band_ref[kv, 0]
            s = _dot(kwin, qz) + band
            s_ref[c] = s
            mx_ref[c] = _fold8(s, jnp.maximum)
    for c, (t, kv) in enumerate(chains):
        m = jnp.max(mx_ref[c], axis=0, keepdims=True)
        l8 = jnp.zeros((8, win), F32)
        for r in range(win // rows):
            p = jnp.exp2(s_ref[c, r * rows:(r + 1) * rows] - m)
            p_ref[c, r * rows:(r + 1) * rows] = p.astype(BF16)
            l8 = l8 + _fold8(p, jnp.add)
        mx_ref[c, 0:1] = 1.0 / (jnp.sum(l8, axis=0, keepdims=True) + jnp.exp2(sink_ref[kv] - m))


def _swa_back(i, blocks, cvT_ref, p_ref, mx_ref, gate_ref):
    chains = [(t, kv) for t in range(blocks) for kv in range(C_KV_HEADS)]
    outs = {}
    for c, (t, kv) in enumerate(chains):
        n = i * blocks + t
        prev = jnp.maximum(n - 1, 0)
        vwin = jnp.concatenate([cvT_ref[0, prev, kv * HEAD_DIM:(kv + 1) * HEAD_DIM, :],
                                cvT_ref[0, n, kv * HEAD_DIM:(kv + 1) * HEAD_DIM, :]], axis=1)
        outs[t, kv] = _dot(vwin, p_ref[c]) * mx_ref[c, 0:1]
    tiles = []
    for u in range(blocks * SWA_BLOCK // Q_TILE):
        head_rows = []
        for kv in range(C_KV_HEADS):
            for g in range(2):
                head_rows.append(jnp.concatenate(
                    [outs[t, kv][:, g * SWA_BLOCK:(g + 1) * SWA_BLOCK] for t in (2 * u, 2 * u + 1)], axis=1))
        o = jnp.concatenate(head_rows, axis=0).T
        gate = gate_ref[0, u * Q_TILE:(u + 1) * Q_TILE, :].astype(F32)
        tiles.append((o * gate).astype(BF16))
    return tiles


def _tables(S):
    tm = IN_TILE
    slopes_c, slopes_a = SLOPES_C, SLOPES_A
    pos = np.arange(S)

    kca = np.zeros((S, AUG), np.float32)
    blk, r = pos // MOBA_BLOCK, pos % MOBA_BLOCK
    kca[pos, COL_ONEHOT + blk] = 1.0
    kca[:, COL_POS] = r
    kca[:, COL_POS + 1] = r
    kca[:, COL_POS + 2] = blk
    kca[:, COL_POS + 3] = blk
    kca = jnp.asarray(kca, BF16)

    c = slopes_a * np.float32(LOG2E)
    hi = c.astype(BF16).astype(np.float32)
    lo = c - hi
    rows = np.stack([hi, lo, hi * MOBA_BLOCK, lo * MOBA_BLOCK], axis=-1)
    rows = np.pad(rows, ((0, 0), (0, AUG - COL_POS - 4)))
    crow_a = jnp.asarray(np.broadcast_to(rows[:, :, None], rows.shape + (Q_TILE,)), BF16)

    nb = S // MOBA_BLOCK
    avg = jnp.asarray((pos[None, :] // MOBA_BLOCK == np.arange(nb)[:, None]) / MOBA_BLOCK, BF16)
    kq = np.arange(MOBA_BLOCK)
    causal = np.where(kq[:, None] <= kq[None, :], 0.0, NEG_BIG)
    causal = jnp.asarray(np.concatenate([causal, np.zeros_like(causal)], axis=0), F32)
    u = np.arange(2 * SWA_BLOCK)[:, None]
    t = np.arange(SWA_BLOCK)[None, :]
    ok = (u > t) & (u <= t + SWA_BLOCK)
    dist = (SWA_BLOCK + t - u).astype(np.float32)
    sc = np.asarray(slopes_c, np.float32) * np.float32(LOG2E)
    band = []
    for kv in range(C_KV_HEADS):
        bias = np.concatenate([-sc[2 * kv] * dist, -sc[2 * kv + 1] * dist], axis=1)
        ok2 = np.concatenate([ok, ok], axis=1)
        band.append(np.stack([np.where(ok2 & (u >= SWA_BLOCK), bias, NEG_BIG),
                              np.where(ok2, bias, NEG_BIG)]))
    band = jnp.asarray(np.stack(band), F32)
    w = np.repeat(np.asarray(POOL_WINDOWS, np.float32), 64)[None, :]
    first = 1.0 / np.minimum(np.arange(tm, dtype=np.float32)[:, None] + 1.0, w)
    invcnt = jnp.asarray(np.stack([first, np.broadcast_to(1.0 / w, (tm, 256))]), F32)
    return dict(kca=kca, crow_a=crow_a, avg=avg, causal=causal, band=band, invcnt=invcnt)


def kernel(x, norm_g, w_in, w_out, a_q_norm, a_k_norm, pool_w, pool_scale, c_q_norm, c_k_norm, c_sinks, conv_w):
    B, S, _ = x.shape
    depth = norm_g.shape[0]
    assert S % IN_TILE == 0 and S // MOBA_BLOCK <= 32
    tb = _tables(S)
    tm = IN_TILE
    d = np.arange(256)
    bd = jnp.asarray(d[:, None] // HEAD_DIM == d[None, :] // HEAD_DIM, BF16)
    qscale = QK_SCALE * LOG2E
    for l in range(depth):
        gq = jnp.broadcast_to(jnp.tile(a_q_norm[l] * qscale, N_HEADS)[:, None], (256, tm))
        gcq = jnp.broadcast_to(jnp.tile(c_q_norm[l] * qscale, N_HEADS)[:, None], (256, tm))
        gk = jnp.tile(a_k_norm[l], N_HEADS)[None, :]
        gck = jnp.tile(c_k_norm[l], C_KV_HEADS)[None, :]
        pw = jnp.zeros((256, 256), F32)
        for g in range(4):
            pw = pw.at[g * 64:(g + 1) * 64, g * 64:(g + 1) * 64].set(pool_w[l, g])
        qT, kaug, vT, cqT, ck, cvT, gates, ybd = _in_proj(
            x, norm_g[l][None, :], w_in[l].astype(BF16), bd, gq, gk, gcq, gck, tb['kca'],
            tb['invcnt'], pw.astype(BF16), pool_scale[l][None, :], conv_w[l])
        sink = jnp.repeat((c_sinks[l] * LOG2E).reshape(C_KV_HEADS, 2), SWA_BLOCK, axis=1)[:, None, :]
        x = _attention(qT, kaug, vT, tb['avg'], tb['crow_a'], tb['causal'],
                       cqT, ck, cvT, tb['band'], sink, gates,
                       x, ybd, w_out[l].astype(BF16))
    return x
```

```python
import math

import jax
import jax.numpy as jnp
import numpy as np
from jax import lax
from jax.experimental import pallas as pl
from jax.experimental.pallas import tpu as pltpu

F32 = jnp.float32
BF16 = jnp.bfloat16

D_MODEL = 1024
HEAD_DIM = 64
N_HEADS = 4
C_KV_HEADS = 2
NORM_EPS = 1e-6
MOBA_BLOCK = 256
MOBA_TOPK = 3
SWA_BLOCK = 128
POOL_WINDOWS = (2, 4, 8, 16)
CONV_WIDTH = 3
HALO = 16

AUG = 128
COL_ONEHOT = 64
COL_POS = 96
NEG_BIG = -(2.0 ** 100)
LOG2E = math.log2(math.e)
QK_SCALE = HEAD_DIM ** -0.5
SLOPES = np.exp2(-(8.0 / (2 * N_HEADS)) * np.arange(1, 2 * N_HEADS + 1)).astype(np.float32)
SLOPES_C, SLOPES_A = SLOPES[:N_HEADS], SLOPES[N_HEADS:]

IN_TILE = 1024
Q_TILE = 256
VMEM_LIMIT = 56 * 1024 * 1024

SEG = dict(aq=0, ak=256, av=512, ag=768, bu=1024, bg=1280, cq=1536, ck=1792, cv=1920,
           cg=2048, dh=2304, db=2560, dc=2816, dg=3072)
IN_PROJ_WIDTH = 3328


def _dot(a, b):
    return jnp.dot(a, b, preferred_element_type=F32)


def _silu(v):
    h = 0.5 * v
    return h + h * jnp.tanh(h)


def _fold8(x, op):
    out = x[0:8]
    for g in range(1, x.shape[0] // 8):
        out = op(out, x[g * 8:(g + 1) * 8])
    return out


def _in_proj_kernel(x_ref, g_ref, w_ref, bd_ref, gq_ref, gk_ref, gcq_ref, gck_ref, kca_ref,
                    invcnt_ref, pw_ref, pscale_ref, cw_ref,
                    qT_ref, kaug_ref, vT_ref, cqT_ref, ck_ref, cvT_ref, gates_ref, ybd_ref,
                    hist_ref):
    tm = x_ref.shape[1]
    s_idx = pl.program_id(1)
    x = x_ref[0]
    ms = jnp.mean(x * x, axis=-1, keepdims=True)
    h = (x * lax.rsqrt(ms + NORM_EPS) * g_ref[...]).astype(BF16)

    def proj(lo, width):
        return _dot(h, w_ref[:, lo:lo + width])

    def norm_t(p, gain_ref):
        pt = p.T
        outs = []
        for hh in range(N_HEADS):
            ph = pt[hh * HEAD_DIM:(hh + 1) * HEAD_DIM]
            ss = jnp.sum(ph * ph, axis=0, keepdims=True) * (1.0 / HEAD_DIM)
            outs.append(ph * lax.rsqrt(ss + NORM_EPS))
        return (jnp.concatenate(outs, axis=0) * gain_ref[...]).astype(BF16)

    def norm_rows(p, gain_ref, width):
        ss = _dot((p * p).astype(BF16), bd_ref[0:width, 0:width]) * (1.0 / HEAD_DIM)
        return p * lax.rsqrt(ss + NORM_EPS) * gain_ref[...]

    def build_kaug(kn, kconst):
        lane = lax.broadcasted_iota(jnp.int32, (tm, AUG), 1)
        blocks = []
        for hh in range(N_HEADS):
            col = kn[:, (hh // 2) * AUG:(hh // 2 + 1) * AUG]
            if hh % 2 == 1:
                col = pltpu.roll(col, HEAD_DIM, axis=1)
            blocks.append(jnp.where(lane < HEAD_DIM, col, kconst).astype(BF16))
        return blocks


    @pl.when(s_idx == 0)
    def _():
        hist_ref[...] = jnp.zeros(hist_ref.shape, F32)

    back = lambda v, k: pltpu.roll(v, k, axis=0)

    pd = proj(SEG['dh'], 1024)
    dh, db, dc, dg = (pd[:, k * 256:(k + 1) * 256] for k in range(4))
    u = dc * dh
    ue = jnp.concatenate([hist_ref[1], u], axis=0)
    hist_ref[1] = u[tm - HALO:]
    conv = (cw_ref[0:1, :] * back(ue, 2) + cw_ref[1:2, :] * back(ue, 1) + cw_ref[2:3, :] * ue)[HALO:]
    ybd_ref[0, :, 256:512] = (db * conv * _silu(dg)).astype(BF16)

    pb = proj(SEG['bu'], 512)
    bu, bg = pb[:, 0:256], pb[:, 256:512]
    e = jnp.concatenate([hist_ref[0], bu], axis=0)
    hist_ref[0] = bu[tm - HALO:]
    a2 = e + back(e, 1)
    a4 = a2 + back(a2, 2)
    a4r = a4[:, 128:256]
    a8 = a4r + back(a4r, 4)
    a16 = a8 + back(a8, 8)
    lane = lax.broadcasted_iota(jnp.int32, (HALO + tm, 128), 1)
    sums = jnp.concatenate([jnp.where(lane < 64, a2[:, 0:128], a4[:, 0:128]),
                            jnp.where(lane < 64, a8, a16)], axis=1)
    pooled = sums[HALO:] * invcnt_ref[0] - bu
    yb = _dot(pooled.astype(BF16), pw_ref[...]) * pscale_ref[...] * _silu(bg)
    ybd_ref[0, :, 0:256] = yb.astype(BF16)

    nblk = tm // Q_TILE
    qT = norm_t(proj(SEG['aq'], 256), gq_ref)
    for t in range(nblk):
        qT_ref[0, t] = qT[:, t * Q_TILE:(t + 1) * Q_TILE]
    kn = norm_rows(proj(SEG['ak'], 256), gk_ref, 256)
    kaug = build_kaug(kn, kca_ref[...].astype(F32))
    for hh in range(N_HEADS):
        kaug_ref[0, hh] = kaug[hh]
    vT = proj(SEG['av'], 256).T.astype(BF16)
    for t in range(nblk):
        vT_ref[0, t] = vT[:, t * Q_TILE:(t + 1) * Q_TILE]
    cqT = norm_t(proj(SEG['cq'], 256), gcq_ref)
    for t in range(nblk):
        cqT_ref[0, t] = cqT[:, t * Q_TILE:(t + 1) * Q_TILE]
    ckv = proj(SEG['ck'], 256)
    ck_ref[0] = norm_rows(ckv[:, 0:128], gck_ref, 128).astype(BF16)
    cvT = ckv[:, 128:256].T.astype(BF16)
    for t in range(tm // SWA_BLOCK):
        cvT_ref[0, t] = cvT[:, t * SWA_BLOCK:(t + 1) * SWA_BLOCK]
    gates_ref[0, :, 0:256] = _silu(proj(SEG['ag'], 256)).astype(BF16)
    gates_ref[0, :, 256:512] = _silu(proj(SEG['cg'], 256)).astype(BF16)


def _in_proj(x, g, w, bd, gq, gk, gcq, gck, kca, invcnt, pw, pscale, cw):
    B, S, _ = x.shape
    tm = IN_TILE
    ns = S // tm
    nq = S // Q_TILE
    nkb = S // SWA_BLOCK
    const = lambda *shape: pl.BlockSpec(shape, lambda b, s: (0,) * len(shape))
    out_shape = (
        jax.ShapeDtypeStruct((B, nq, 256, Q_TILE), BF16),
        jax.ShapeDtypeStruct((B, N_HEADS, S, AUG), BF16),
        jax.ShapeDtypeStruct((B, nq, 256, Q_TILE), BF16),
        jax.ShapeDtypeStruct((B, nq, 256, Q_TILE), BF16),
        jax.ShapeDtypeStruct((B, S, 128), BF16),
        jax.ShapeDtypeStruct((B, nkb, 128, SWA_BLOCK), BF16),
        jax.ShapeDtypeStruct((B, S, 512), BF16),
        jax.ShapeDtypeStruct((B, S, 512), BF16),
    )
    r = tm // Q_TILE
    out_specs = (
        pl.BlockSpec((1, r, 256, Q_TILE), lambda b, s: (b, s, 0, 0)),
        pl.BlockSpec((1, N_HEADS, tm, AUG), lambda b, s: (b, 0, s, 0)),
        pl.BlockSpec((1, r, 256, Q_TILE), lambda b, s: (b, s, 0, 0)),
        pl.BlockSpec((1, r, 256, Q_TILE), lambda b, s: (b, s, 0, 0)),
        pl.BlockSpec((1, tm, 128), lambda b, s: (b, s, 0)),
        pl.BlockSpec((1, tm // SWA_BLOCK, 128, SWA_BLOCK), lambda b, s: (b, s, 0, 0)),
        pl.BlockSpec((1, tm, 512), lambda b, s: (b, s, 0)),
        pl.BlockSpec((1, tm, 512), lambda b, s: (b, s, 0)),
    )
    in_specs = [
        pl.BlockSpec((1, tm, D_MODEL), lambda b, s: (b, s, 0)),
        const(1, D_MODEL),
        const(D_MODEL, IN_PROJ_WIDTH),
        const(256, 256),
        const(256, tm), const(1, 256), const(256, tm), const(1, 128),
        pl.BlockSpec((tm, AUG), lambda b, s: (s, 0)),
        pl.BlockSpec((1, tm, 256), lambda b, s: (jnp.minimum(s, 1), 0, 0)),
        const(256, 256), const(1, 256), const(CONV_WIDTH, 256),
    ]
    return pl.pallas_call(
        _in_proj_kernel, grid=(B, ns), in_specs=in_specs, out_specs=out_specs, out_shape=out_shape,
        scratch_shapes=[pltpu.VMEM((2, HALO, 256), F32)],
        compiler_params=pltpu.CompilerParams(dimension_semantics=("parallel", "arbitrary"),
                                             vmem_limit_bytes=VMEM_LIMIT),
        name="in_proj",
    )(x, g, w, bd, gq, gk, gcq, gck, kca, invcnt, pw, pscale, cw)


def _attn_kernel(qT_ref, kaug_ref, vT_ref, avg_ref, crow_ref, causal_ref, gate_ref,
                 cqT_ref, ck_ref, cvT_ref, band0_ref, band_ref, sink_ref, cgate_ref,
                 x_ref, ybd_ref, wo_ref,
                 out_ref,
                 kmean_ref, qaug_ref, acc_ref, s_ref, p_ref, mx_ref, cs_ref, cp_ref, cmx_ref):
    i = pl.program_id(1)
    swa_blocks = Q_TILE // SWA_BLOCK
    S = kaug_ref.shape[2]
    nb = S // MOBA_BLOCK
    chunk = 1024

    @pl.when(i == 0)
    def _():
        for hh in range(N_HEADS):
            acc = jnp.zeros((nb, AUG), F32)
            for c in range(S // chunk):
                acc = acc + _dot(avg_ref[:, c * chunk:(c + 1) * chunk],
                                 kaug_ref[0, hh, c * chunk:(c + 1) * chunk, :])
            kmean_ref[hh] = acc

    row = lax.broadcasted_iota(jnp.int32, (nb, Q_TILE), 0).astype(F32)
    i_f = i.astype(F32)
    neg_inf = jnp.float32(-jnp.inf)
    heads = range(N_HEADS)
    vh = lambda hh, j: vT_ref[0, j, hh * HEAD_DIM:(hh + 1) * HEAD_DIM, :]
    kb = lambda hh, j: kaug_ref[0, hh, pl.ds(pl.multiple_of(j * MOBA_BLOCK, MOBA_BLOCK), MOBA_BLOCK), :]

    for hh in heads:
        qh = qT_ref[0, 0, hh * HEAD_DIM:(hh + 1) * HEAD_DIM, :]
        km = kmean_ref[hh][:, 0:HEAD_DIM]
        km_hi = km.astype(BF16)
        km_lo = (km - km_hi.astype(F32)).astype(BF16)
        bs = _dot(km_hi, qh) + _dot(km_lo, qh)
        xs = jnp.where(row < i_f, bs, neg_inf)
        sel = row == i_f
        for _ in range(MOBA_TOPK):
            mx = jnp.max(xs, axis=0, keepdims=True)
            first = jnp.min(jnp.where(xs == mx, row, jnp.float32(nb)), axis=0, keepdims=True)
            pick = row == first
            sel = sel | (pick & (mx > neg_inf))
            xs = jnp.where(pick, neg_inf, xs)
        negmask = jnp.where(sel, 0.0, NEG_BIG).astype(BF16)
        pad = jnp.zeros((COL_POS - COL_ONEHOT - nb, Q_TILE), BF16) if nb < 32 else None
        parts = [qh, negmask] + ([pad] if pad is not None else []) + [crow_ref[hh]]
        qaug_ref[hh] = jnp.concatenate(parts, axis=0)

    _swa_front(i, swa_blocks, cqT_ref, ck_ref, band0_ref, band_ref, sink_ref, cs_ref, cp_ref, cmx_ref)

    def blk(pos):
        past = jnp.minimum(i + 1, nb - 1)
        return jnp.where(pos <= 0, i, jnp.where(pos <= i, pos - 1, past))

    kblk = MOBA_BLOCK
    sub = 8
    rows = 64

    def scores(hh, j0, j1, mask=None):
        s = _dot(jnp.concatenate([kb(hh, j0), kb(hh, j1)], axis=0), qaug_ref[hh])
        if mask is not None:
            s = s + mask
        s_ref[hh] = s
        mx_ref[hh] = _fold8(s, jnp.maximum)

    def scores_head(hh, t):
        scores(hh, blk(2 * t), blk(2 * t + 1))

    def scores_step(t):
        for hh in heads:
            scores_head(hh, t)

    def softmax_head(hh, m_old, l_old):
        m_new = jnp.maximum(m_old, jnp.max(mx_ref[hh], axis=0, keepdims=True))
        alpha = jnp.exp2(m_old - m_new)
        l8 = alpha * l_old
        for c in range(2 * kblk // rows):
            p = jnp.exp2(s_ref[hh, c * rows:(c + 1) * rows] - m_new)
            p_ref[hh, c * rows:(c + 1) * rows] = p.astype(BF16)
            l8 = l8 + _fold8(p, jnp.add)
        return m_new, l8, alpha

    def softmax_step(carry):
        ms, ls, _ = carry
        new = [softmax_head(hh, ms[hh], ls[hh]) for hh in heads]
        return tuple(zip(*new))

    def values_head(hh, t, alpha):
        acc_ref[hh] = (alpha * acc_ref[hh]
                       + _dot(vh(hh, blk(2 * t)), p_ref[hh, 0:kblk])
                       + _dot(vh(hh, blk(2 * t + 1)), p_ref[hh, kblk:2 * kblk]))

    def values_step(t, alphas):
        for hh in heads:
            values_head(hh, t, alphas[hh])

    def body(t, carry):
        ms, ls, alphas = carry
        new = []
        for hh in heads:
            values_head(hh, t - 1, alphas[hh])
            new.append(softmax_head(hh, ms[hh], ls[hh]))
            scores_head(hh, t + 1)
        return tuple(zip(*new))

    def drain(carry):
        values_step(steps - 2, carry[2])
        return softmax_step(carry)

    steps = (i + 2) // 2
    neg = jnp.full((1, Q_TILE), NEG_BIG, F32)
    zero8 = jnp.zeros((sub, Q_TILE), F32)
    acc_ref[...] = jnp.zeros(acc_ref.shape, F32)
    for hh in heads:
        scores(hh, i, blk(1), causal_ref[...])
    carry = softmax_step(((neg,) * N_HEADS, (zero8,) * N_HEADS, None))
    scores_step(1)
    carry = lax.fori_loop(1, steps - 1, body, carry)
    carry = lax.cond(steps >= 2, drain, lambda c: c, carry)
    ms, ls, alphas = carry
    values_step(steps - 1, alphas)
    yc = _swa_back(i, swa_blocks, cvT_ref, cp_ref, cmx_ref, cgate_ref)[0]
    part = (x_ref[0] + _dot(ybd_ref[0, :, 0:256], wo_ref[256:512, :]) + _dot(yc, wo_ref[512:768, :])
            + _dot(ybd_ref[0, :, 256:512], wo_ref[768:1024, :]))
    outs = [acc_ref[hh] / jnp.sum(ls[hh], axis=0, keepdims=True) for hh in heads]
    o = jnp.concatenate(outs, axis=0).T
    ya = (o * gate_ref[0].astype(F32)).astype(BF16)
    out_ref[0] = part + _dot(ya, wo_ref[0:256, :])


def _attention(qT, kaug, vT, avg, crow, causal, cqT, ck, cvT, band, sink, gates, x, ybd, wo):
    B, nq = qT.shape[0], qT.shape[1]
    S = kaug.shape[2]
    nb = S // MOBA_BLOCK
    nkb = S // SWA_BLOCK
    nchain = (Q_TILE // SWA_BLOCK) * C_KV_HEADS
    win = 2 * SWA_BLOCK
    tile = lambda col: pl.BlockSpec((1, Q_TILE, 256), lambda b, i: (b, i, col))
    in_specs = [
        pl.BlockSpec((1, 1, 256, Q_TILE), lambda b, i: (b, i, 0, 0)),
        pl.BlockSpec((1, N_HEADS, S, AUG), lambda b, i: (b, 0, 0, 0)),
        pl.BlockSpec((1, nq, 256, Q_TILE), lambda b, i: (b, 0, 0, 0)),
        pl.BlockSpec((nb, S), lambda b, i: (0, 0)),
        pl.BlockSpec((N_HEADS, AUG - COL_POS, Q_TILE), lambda b, i: (0, 0, 0)),
        pl.BlockSpec((2 * MOBA_BLOCK, Q_TILE), lambda b, i: (0, 0)),
        tile(0),
        pl.BlockSpec((1, 1, 256, Q_TILE), lambda b, i: (b, i, 0, 0)),
        pl.BlockSpec((1, S, 128), lambda b, i: (b, 0, 0)),
        pl.BlockSpec((1, nkb, 128, SWA_BLOCK), lambda b, i: (b, 0, 0, 0)),
        pl.BlockSpec((C_KV_HEADS, 1, win, win), lambda b, i: (0, jnp.minimum(i, 1), 0, 0)),
        pl.BlockSpec((C_KV_HEADS, 1, win, win), lambda b, i: (0, 1, 0, 0)),
        pl.BlockSpec((C_KV_HEADS, 1, win), lambda b, i: (0, 0, 0)),
        tile(1),
        pl.BlockSpec((1, Q_TILE, D_MODEL), lambda b, i: (b, i, 0)),
        pl.BlockSpec((1, Q_TILE, 512), lambda b, i: (b, i, 0)),
        pl.BlockSpec((D_MODEL, D_MODEL), lambda b, i: (0, 0)),
    ]
    return pl.pallas_call(
        _attn_kernel, grid=(B, nq), in_specs=in_specs,
        out_specs=pl.BlockSpec((1, Q_TILE, D_MODEL), lambda b, i: (b, i, 0)),
        out_shape=jax.ShapeDtypeStruct((B, S, D_MODEL), F32),
        scratch_shapes=[pltpu.VMEM((N_HEADS, nb, AUG), F32),
                        pltpu.VMEM((N_HEADS, AUG, Q_TILE), BF16),
                        pltpu.VMEM((N_HEADS, HEAD_DIM, Q_TILE), F32),
                        pltpu.VMEM((N_HEADS, 2 * MOBA_BLOCK, Q_TILE), F32),
                        pltpu.VMEM((N_HEADS, 2 * MOBA_BLOCK, Q_TILE), BF16),
                        pltpu.VMEM((N_HEADS, 8, Q_TILE), F32),
                        pltpu.VMEM((nchain, win, win), F32),
                        pltpu.VMEM((nchain, win, win), BF16),
                        pltpu.VMEM((nchain, 8, win), F32)],
        compiler_params=pltpu.CompilerParams(dimension_semantics=("parallel", "arbitrary"),
                                             vmem_limit_bytes=VMEM_LIMIT),
        name="attention",
    )(qT, kaug, vT, avg, crow, causal, gates, cqT, ck, cvT, band, band, sink, gates, x, ybd, wo)


def _swa_front(i, blocks, cqT_ref, ck_ref, band0_ref, band_ref, sink_ref, s_ref, p_ref, mx_ref):
    win = 2 * SWA_BLOCK
    rows = 64
    zrows = jnp.zeros((HEAD_DIM, win), BF16)
    chains = [(t, kv) for t in range(blocks) for kv in range(C_KV_HEADS)]
    for t in range(blocks):
        n = i * blocks + t
        prev = jnp.maximum(n - 1, 0)
        kwin = jnp.concatenate([
            ck_ref[0, pl.ds(pl.multiple_of(prev * SWA_BLOCK, SWA_BLOCK), SWA_BLOCK), :],
            ck_ref[0, pl.ds(pl.multiple_of(n * SWA_BLOCK, SWA_BLOCK), SWA_BLOCK), :]], axis=0)
        for kv in range(C_KV_HEADS):
            c = chains.index((t, kv))
            q2 = jnp.concatenate([
                cqT_ref[0, t // 2, (2 * kv + g) * HEAD_DIM:(2 * kv + g + 1) * HEAD_DIM,
                        (t % 2) * SWA_BLOCK:(t % 2 + 1) * SWA_BLOCK] for g in range(2)], axis=1)
            qz = jnp.concatenate([q2, zrows] if kv == 0 else [zrows, q2], axis=0)
            band = band0_ref[kv, 0] if t == 0 else band_ref[kv, 0]
            s = _dot(kwin, qz) + band
            s_ref[c] = s
            mx_ref[c] = _fold8(s, jnp.maximum)
    for c, (t, kv) in enumerate(chains):
        m = jnp.max(mx_ref[c], axis=0, keepdims=True)
        l8 = jnp.zeros((8, win), F32)
        for r in range(win // rows):
            p = jnp.exp2(s_ref[c, r * rows:(r + 1) * rows] - m)
            p_ref[c, r * rows:(r + 1) * rows] = p.astype(BF16)
            l8 = l8 + _fold8(p, jnp.add)
        mx_ref[c, 0:1] = 1.0 / (jnp.sum(l8, axis=0, keepdims=True) + jnp.exp2(sink_ref[kv] - m))


def _swa_back(i, blocks, cvT_ref, p_ref, mx_ref, gate_ref):
    chains = [(t, kv) for t in range(blocks) for kv in range(C_KV_HEADS)]
    outs = {}
    for c, (t, kv) in enumerate(chains):
        n = i * blocks + t
        prev = jnp.maximum(n - 1, 0)
        vwin = jnp.concatenate([cvT_ref[0, prev, kv * HEAD_DIM:(kv + 1) * HEAD_DIM, :],
                                cvT_ref[0, n, kv * HEAD_DIM:(kv + 1) * HEAD_DIM, :]], axis=1)
        outs[t, kv] = _dot(vwin, p_ref[c]) * mx_ref[c, 0:1]
    tiles = []
    for u in range(blocks * SWA_BLOCK // Q_TILE):
        head_rows = []
        for kv in range(C_KV_HEADS):
            for g in range(2):
                head_rows.append(jnp.concatenate(
                    [outs[t, kv][:, g * SWA_BLOCK:(g + 1) * SWA_BLOCK] for t in (2 * u, 2 * u + 1)], axis=1))
        o = jnp.concatenate(head_rows, axis=0).T
        gate = gate_ref[0, u * Q_TILE:(u + 1) * Q_TILE, :].astype(F32)
        tiles.append((o * gate).astype(BF16))
    return tiles


def _tables(S):
    tm = IN_TILE
    slopes_c, slopes_a = SLOPES_C, SLOPES_A
    pos = np.arange(S)

    kca = np.zeros((S, AUG), np.float32)
    blk, r = pos // MOBA_BLOCK, pos % MOBA_BLOCK
    kca[pos, COL_ONEHOT + blk] = 1.0
    kca[:, COL_POS] = r
    kca[:, COL_POS + 1] = r
    kca[:, COL_POS + 2] = blk
    kca[:, COL_POS + 3] = blk
    kca = jnp.asarray(kca, BF16)

    c = slopes_a * np.float32(LOG2E)
    hi = c.astype(BF16).astype(np.float32)
    lo = c - hi
    rows = np.stack([hi, lo, hi * MOBA_BLOCK, lo * MOBA_BLOCK], axis=-1)
    rows = np.pad(rows, ((0, 0), (0, AUG - COL_POS - 4)))
    crow_a = jnp.asarray(np.broadcast_to(rows[:, :, None], rows.shape + (Q_TILE,)), BF16)

    nb = S // MOBA_BLOCK
    avg = jnp.asarray((pos[None, :] // MOBA_BLOCK == np.arange(nb)[:, None]) / MOBA_BLOCK, BF16)
    kq = np.arange(MOBA_BLOCK)
    causal = np.where(kq[:, None] <= kq[None, :], 0.0, NEG_BIG)
    causal = jnp.asarray(np.concatenate([causal, np.zeros_like(causal)], axis=0), F32)
    u = np.arange(2 * SWA_BLOCK)[:, None]
    t = np.arange(SWA_BLOCK)[None, :]
    ok = (u > t) & (u <= t + SWA_BLOCK)
    dist = (SWA_BLOCK + t - u).astype(np.float32)
    sc = np.asarray(slopes_c, np.float32) * np.float32(LOG2E)
    band = []
    for kv in range(C_KV_HEADS):
        bias = np.concatenate([-sc[2 * kv] * dist, -sc[2 * kv + 1] * dist], axis=1)
        ok2 = np.concatenate([ok, ok], axis=1)
        band.append(np.stack([np.where(ok2 & (u >= SWA_BLOCK), bias, NEG_BIG),
                              np.where(ok2, bias, NEG_BIG)]))
    band = jnp.asarray(np.stack(band), F32)
    w = np.repeat(np.asarray(POOL_WINDOWS, np.float32), 64)[None, :]
    first = 1.0 / np.minimum(np.arange(tm, dtype=np.float32)[:, None] + 1.0, w)
    invcnt = jnp.asarray(np.stack([first, np.broadcast_to(1.0 / w, (tm, 256))]), F32)
    return dict(kca=kca, crow_a=crow_a, avg=avg, causal=causal, band=band, invcnt=invcnt)


def kernel(x, norm_g, w_in, w_out, a_q_norm, a_k_norm, pool_w, pool_scale, c_q_norm, c_k_norm, c_sinks, conv_w):
    B, S, _ = x.shape
    depth = norm_g.shape[0]
    assert S % IN_TILE == 0 and S // MOBA_BLOCK <= 32
    tb = _tables(S)
    tm = IN_TILE
    d = np.arange(256)
    bd = jnp.asarray(d[:, None] // HEAD_DIM == d[None, :] // HEAD_DIM, BF16)
    qscale = QK_SCALE * LOG2E
    for l in range(depth):
        gq = jnp.broadcast_to(jnp.tile(a_q_norm[l] * qscale, N_HEADS)[:, None], (256, tm))
        gcq = jnp.broadcast_to(jnp.tile(c_q_norm[l] * qscale, N_HEADS)[:, None], (256, tm))
        gk = jnp.tile(a_k_norm[l], N_HEADS)[None, :]
        gck = jnp.tile(c_k_norm[l], C_KV_HEADS)[None, :]
        pw = jnp.zeros((256, 256), F32)
        for g in range(4):
            pw = pw.at[g * 64:(g + 1) * 64, g * 64:(g + 1) * 64].set(pool_w[l, g])
        qT, kaug, vT, cqT, ck, cvT, gates, ybd = _in_proj(
            x, norm_g[l][None, :], w_in[l].astype(BF16), bd, gq, gk, gcq, gck, tb['kca'],
            tb['invcnt'], pw.astype(BF16), pool_scale[l][None, :], conv_w[l])
        sink = jnp.repeat((c_sinks[l] * LOG2E).reshape(C_KV_HEADS, 2), SWA_BLOCK, axis=1)[:, None, :]
        x = _attention(qT, kaug, vT, tb['avg'], tb['crow_a'], tb['causal'],
                       cqT, ck, cvT, tb['band'], sink, gates,
                       x, ybd, w_out[l].astype(BF16))
    return x
```

```python
import math

import jax
import jax.numpy as jnp
import numpy as np
from jax import lax
from jax.experimental import pallas as pl
from jax.experimental.pallas import tpu as pltpu

F32 = jnp.float32
BF16 = jnp.bfloat16

D_MODEL = 1024
HEAD_DIM = 64
N_HEADS = 4
C_KV_HEADS = 2
NORM_EPS = 1e-6
MOBA_BLOCK = 256
MOBA_TOPK = 3
SWA_BLOCK = 128
POOL_WINDOWS = (2, 4, 8, 16)
CONV_WIDTH = 3
HALO = 16

AUG = 128
COL_ONEHOT = 64
COL_POS = 96
NEG_BIG = -(2.0 ** 100)
LOG2E = math.log2(math.e)
QK_SCALE = HEAD_DIM ** -0.5
SLOPES = np.exp2(-(8.0 / (2 * N_HEADS)) * np.arange(1, 2 * N_HEADS + 1)).astype(np.float32)
SLOPES_C, SLOPES_A = SLOPES[:N_HEADS], SLOPES[N_HEADS:]

IN_TILE = 1024
Q_TILE = 256
VMEM_LIMIT = 56 * 1024 * 1024

SEG = dict(aq=0, ak=256, av=512, ag=768, bu=1024, bg=1280, cq=1536, ck=1792, cv=1920,
           cg=2048, dh=2304, db=2560, dc=2816, dg=3072)
IN_PROJ_WIDTH = 3328


def _dot(a, b):
    return jnp.dot(a, b, preferred_element_type=F32)


def _silu(v):
    h = 0.5 * v
    return h + h * jnp.tanh(h)


def _fold8(x, op):
    out = x[0:8]
    for g in range(1, x.shape[0] // 8):
        out = op(out, x[g * 8:(g + 1) * 8])
    return out


def _in_proj_kernel(x_ref, g_ref, w_ref, bd_ref, gq_ref, gk_ref, gcq_ref, gck_ref, kca_ref,
                    invcnt_ref, pw_ref, pscale_ref, cw_ref,
                    qT_ref, kaug_ref, vT_ref, cqT_ref, ck_ref, cvT_ref, gates_ref, ybd_ref,
                    hist_ref):
    tm = x_ref.shape[1]
    s_idx = pl.program_id(1)
    x = x_ref[0]
    ms = jnp.mean(x * x, axis=-1, keepdims=True)
    h = (x * lax.rsqrt(ms + NORM_EPS) * g_ref[...]).astype(BF16)

    def proj(lo, width):
        return _dot(h, w_ref[:, lo:lo + width])

    def norm_t(p, gain_ref):
        pt = p.T
        outs = []
        for hh in range(N_HEADS):
            ph = pt[hh * HEAD_DIM:(hh + 1) * HEAD_DIM]
            ss = jnp.sum(ph * ph, axis=0, keepdims=True) * (1.0 / HEAD_DIM)
            outs.append(ph * lax.rsqrt(ss + NORM_EPS))
        return (jnp.concatenate(outs, axis=0) * gain_ref[...]).astype(BF16)

    def norm_rows(p, gain_ref, width):
        ss = _dot((p * p).astype(BF16), bd_ref[0:width, 0:width]) * (1.0 / HEAD_DIM)
        return p * lax.rsqrt(ss + NORM_EPS) * gain_ref[...]

    def build_kaug(kn, kconst):
        lane = lax.broadcasted_iota(jnp.int32, (tm, AUG), 1)
        blocks = []
        for hh in range(N_HEADS):
            col = kn[:, (hh // 2) * AUG:(hh // 2 + 1) * AUG]
            if hh % 2 == 1:
                col = pltpu.roll(col, HEAD_DIM, axis=1)
            blocks.append(jnp.where(lane < HEAD_DIM, col, kconst).astype(BF16))
        return blocks


    @pl.when(s_idx == 0)
    def _():
        hist_ref[...] = jnp.zeros(hist_ref.shape, F32)

    back = lambda v, k: pltpu.roll(v, k, axis=0)

    pd = proj(SEG['dh'], 1024)
    dh, db, dc, dg = (pd[:, k * 256:(k + 1) * 256] for k in range(4))
    u = dc * dh
    ue = jnp.concatenate([hist_ref[1], u], axis=0)
    hist_ref[1] = u[tm - HALO:]
    conv = (cw_ref[0:1, :] * back(ue, 2) + cw_ref[1:2, :] * back(ue, 1) + cw_ref[2:3, :] * ue)[HALO:]
    ybd_ref[0, :, 256:512] = (db * conv * _silu(dg)).astype(BF16)

    pb = proj(SEG['bu'], 512)
    bu, bg = pb[:, 0:256], pb[:, 256:512]
    e = jnp.concatenate([hist_ref[0], bu], axis=0)
    hist_ref[0] = bu[tm - HALO:]
    a2 = e + back(e, 1)
    a4 = a2 + back(a2, 2)
    a4r = a4[:, 128:256]
    a8 = a4r + back(a4r, 4)
    a16 = a8 + back(a8, 8)
    lane = lax.broadcasted_iota(jnp.int32, (HALO + tm, 128), 1)
    sums = jnp.concatenate([jnp.where(lane < 64, a2[:, 0:128], a4[:, 0:128]),
                            jnp.where(lane < 64, a8, a16)], axis=1)
    pooled = sums[HALO:] * invcnt_ref[0] - bu
    yb = _dot(pooled.astype(BF16), pw_ref[...]) * pscale_ref[...] * _silu(bg)
    ybd_ref[0, :, 0:256] = yb.astype(BF16)

    nblk = tm // Q_TILE
    qT = norm_t(proj(SEG['aq'], 256), gq_ref)
    for t in range(nblk):
        qT_ref[0, t] = qT[:, t * Q_TILE:(t + 1) * Q_TILE]
    kn = norm_rows(proj(SEG['ak'], 256), gk_ref, 256)
    kaug = build_kaug(kn, kca_ref[...].astype(F32))
    for hh in range(N_HEADS):
        kaug_ref[0, hh] = kaug[hh]
    vT = proj(SEG['av'], 256).T.astype(BF16)
    for t in range(nblk):
        vT_ref[0, t] = vT[:, t * Q_TILE:(t + 1) * Q_TILE]
    cqT = norm_t(proj(SEG['cq'], 256), gcq_ref)
    for t in range(nblk):
        cqT_ref[0, t] = cqT[:, t * Q_TILE:(t + 1) * Q_TILE]
    ckv = proj(SEG['ck'], 256)
    ck_ref[0] = norm_rows(ckv[:, 0:128], gck_ref, 128).astype(BF16)
    cvT = ckv[:, 128:256].T.astype(BF16)
    for t in range(tm // SWA_BLOCK):
        cvT_ref[0, t] = cvT[:, t * SWA_BLOCK:(t + 1) * SWA_BLOCK]
    gates_ref[0, :, 0:256] = _silu(proj(SEG['ag'], 256)).astype(BF16)
    gates_ref[0, :, 256:512] = _silu(proj(SEG['cg'], 256)).astype(BF16)


def _in_proj(x, g, w, bd, gq, gk, gcq, gck, kca, invcnt, pw, pscale, cw):
    B, S, _ = x.shape
    tm = IN_TILE
    ns = S // tm
    nq = S // Q_TILE
    nkb = S // SWA_BLOCK
    const = lambda *shape: pl.BlockSpec(shape, lambda b, s: (0,) * len(shape))
    out_shape = (
        jax.ShapeDtypeStruct((B, nq, 256, Q_TILE), BF16),
        jax.ShapeDtypeStruct((B, N_HEADS, S, AUG), BF16),
        jax.ShapeDtypeStruct((B, nq, 256, Q_TILE), BF16),
        jax.ShapeDtypeStruct((B, nq, 256, Q_TILE), BF16),
        jax.ShapeDtypeStruct((B, S, 128), BF16),
        jax.ShapeDtypeStruct((B, nkb, 128, SWA_BLOCK), BF16),
        jax.ShapeDtypeStruct((B, S, 512), BF16),
        jax.ShapeDtypeStruct((B, S, 512), BF16),
    )
    r = tm // Q_TILE
    out_specs = (
        pl.BlockSpec((1, r, 256, Q_TILE), lambda b, s: (b, s, 0, 0)),
        pl.BlockSpec((1, N_HEADS, tm, AUG), lambda b, s: (b, 0, s, 0)),
        pl.BlockSpec((1, r, 256, Q_TILE), lambda b, s: (b, s, 0, 0)),
        pl.BlockSpec((1, r, 256, Q_TILE), lambda b, s: (b, s, 0, 0)),
        pl.BlockSpec((1, tm, 128), lambda b, s: (b, s, 0)),
        pl.BlockSpec((1, tm // SWA_BLOCK, 128, SWA_BLOCK), lambda b, s: (b, s, 0, 0)),
        pl.BlockSpec((1, tm, 512), lambda b, s: (b, s, 0)),
        pl.BlockSpec((1, tm, 512), lambda b, s: (b, s, 0)),
    )
    in_specs = [
        pl.BlockSpec((1, tm, D_MODEL), lambda b, s: (b, s, 0)),
        const(1, D_MODEL),
        const(D_MODEL, IN_PROJ_WIDTH),
        const(256, 256),
        const(256, tm), const(1, 256), const(256, tm), const(1, 128),
        pl.BlockSpec((tm, AUG), lambda b, s: (s, 0)),
        pl.BlockSpec((1, tm, 256), lambda b, s: (jnp.minimum(s, 1), 0, 0)),
        const(256, 256), const(1, 256), const(CONV_WIDTH, 256),
    ]
    return pl.pallas_call(
        _in_proj_kernel, grid=(B, ns), in_specs=in_specs, out_specs=out_specs, out_shape=out_shape,
        scratch_shapes=[pltpu.VMEM((2, HALO, 256), F32)],
        compiler_params=pltpu.CompilerParams(dimension_semantics=("parallel", "arbitrary"),
                                             vmem_limit_bytes=VMEM_LIMIT),
        name="in_proj",
    )(x, g, w, bd, gq, gk, gcq, gck, kca, invcnt, pw, pscale, cw)


def _attn_kernel(qT_ref, kaug_ref, vT_ref, avg_ref, crow_ref, causal_ref, gate_ref,
                 cqT_ref, ck_ref, cvT_ref, band0_ref, band_ref, sink_ref, cgate_ref,
                 x_ref, ybd_ref, wo_ref,
                 out_ref,
                 kmean_ref, qaug_ref, acc_ref, s_ref, p_ref, mx_ref, cs_ref, cp_ref, cmx_ref):
    i = pl.program_id(1)
    swa_blocks = Q_TILE // SWA_BLOCK
    S = kaug_ref.shape[2]
    nb = S // MOBA_BLOCK
    chunk = 1024

    @pl.when(i == 0)
    def _():
        for hh in range(N_HEADS):
            acc = jnp.zeros((nb, AUG), F32)
            for c in range(S // chunk):
                acc = acc + _dot(avg_ref[:, c * chunk:(c + 1) * chunk],
                                 kaug_ref[0, hh, c * chunk:(c + 1) * chunk, :])
            kmean_ref[hh] = acc

    row = lax.broadcasted_iota(jnp.int32, (nb, Q_TILE), 0).astype(F32)
    i_f = i.astype(F32)
    neg_inf = jnp.float32(-jnp.inf)
    heads = range(N_HEADS)
    vh = lambda hh, j: vT_ref[0, j, hh * HEAD_DIM:(hh + 1) * HEAD_DIM, :]
    kb = lambda hh, j: kaug_ref[0, hh, pl.ds(pl.multiple_of(j * MOBA_BLOCK, MOBA_BLOCK), MOBA_BLOCK), :]

    for hh in heads:
        qh = qT_ref[0, 0, hh * HEAD_DIM:(hh + 1) * HEAD_DIM, :]
        km = kmean_ref[hh][:, 0:HEAD_DIM]
        km_hi = km.astype(BF16)
        km_lo = (km - km_hi.astype(F32)).astype(BF16)
        bs = _dot(km_hi, qh) + _dot(km_lo, qh)
        xs = jnp.where(row < i_f, bs, neg_inf)
        sel = row == i_f
        for _ in range(MOBA_TOPK):
            mx = jnp.max(xs, axis=0, keepdims=True)
            first = jnp.min(jnp.where(xs == mx, row, jnp.float32(nb)), axis=0, keepdims=True)
            pick = row == first
            sel = sel | (pick & (mx > neg_inf))
            xs = jnp.where(pick, neg_inf, xs)
        negmask = jnp.where(sel, 0.0, NEG_BIG).astype(BF16)
        pad = jnp.zeros((COL_POS - COL_ONEHOT - nb, Q_TILE), BF16) if nb < 32 else None
        parts = [qh, negmask] + ([pad] if pad is not None else []) + [crow_ref[hh]]
        qaug_ref[hh] = jnp.concatenate(parts, axis=0)

    _swa_front(i, swa_blocks, cqT_ref, ck_ref, band0_ref, band_ref, sink_ref, cs_ref, cp_ref, cmx_ref)

    def blk(pos):
        past = jnp.minimum(i + 1, nb - 1)
        return jnp.where(pos <= 0, i, jnp.where(pos <= i, pos - 1, past))

    kblk = MOBA_BLOCK
    sub = 8
    rows = 64

    def scores(hh, j0, j1, mask=None):
        s = _dot(jnp.concatenate([kb(hh, j0), kb(hh, j1)], axis=0), qaug_ref[hh])
        if mask is not None:
            s = s + mask
        s_ref[hh] = s
        mx_ref[hh] = _fold8(s, jnp.maximum)

    def scores_head(hh, t):
        scores(hh, blk(2 * t), blk(2 * t + 1))

    def scores_step(t):
        for hh in heads:
            scores_head(hh, t)

    def softmax_head(hh, m_old, l_old):
        m_new = jnp.maximum(m_old, jnp.max(mx_ref[hh], axis=0, keepdims=True))
        alpha = jnp.exp2(m_old - m_new)
        l8 = alpha * l_old
        for c in range(2 * kblk // rows):
            p = jnp.exp2(s_ref[hh, c * rows:(c + 1) * rows] - m_new)
            p_ref[hh, c * rows:(c + 1) * rows] = p.astype(BF16)
            l8 = l8 + _fold8(p, jnp.add)
        return m_new, l8, alpha

    def softmax_step(carry):
        ms, ls, _ = carry
        new = [softmax_head(hh, ms[hh], ls[hh]) for hh in heads]
        return tuple(zip(*new))

    def values_head(hh, t, alpha):
        acc_ref[hh] = (alpha * acc_ref[hh]
                       + _dot(vh(hh, blk(2 * t)), p_ref[hh, 0:kblk])
                       + _dot(vh(hh, blk(2 * t + 1)), p_ref[hh, kblk:2 * kblk]))

    def values_step(t, alphas):
        for hh in heads:
            values_head(hh, t, alphas[hh])

    def body(t, carry):
        ms, ls, alphas = carry
        new = []
        for hh in heads:
            values_head(hh, t - 1, alphas[hh])
            new.append(softmax_head(hh, ms[hh], ls[hh]))
            scores_head(hh, t + 1)
        return tuple(zip(*new))

    def drain(carry):
        values_step(steps - 2, carry[2])
        return softmax_step(carry)

    steps = (i + 2) // 2
    neg = jnp.full((1, Q_TILE), NEG_BIG, F32)
    zero8 = jnp.zeros((sub, Q_TILE), F32)
    acc_ref[...] = jnp.zeros(acc_ref.shape, F32)
    for hh in heads:
        scores(hh, i, blk(1), causal_ref[...])
    carry = softmax_step(((neg,) * N_HEADS, (zero8,) * N_HEADS, None))
    scores_step(1)
    trips = jnp.maximum(steps - 2, 0)
    carry = lax.fori_loop(0, trips // 2, lambda k, c: body(2 * k + 2, body(2 * k + 1, c)), carry)
    carry = lax.cond(trips % 2 == 1, lambda c: body(steps - 2, c), lambda c: c, carry)
    carry = lax.cond(steps >= 2, drain, lambda c: c, carry)
    ms, ls, alphas = carry
    values_step(steps - 1, alphas)
    yc = _swa_back(i, swa_blocks, cvT_ref, cp_ref, cmx_ref, cgate_ref)[0]
    part = (x_ref[0] + _dot(ybd_ref[0, :, 0:256], wo_ref[256:512, :]) + _dot(yc, wo_ref[512:768, :])
            + _dot(ybd_ref[0, :, 256:512], wo_ref[768:1024, :]))
    outs = [acc_ref[hh] / jnp.sum(ls[hh], axis=0, keepdims=True) for hh in heads]
    o = jnp.concatenate(outs, axis=0).T
    ya = (o * gate_ref[0].astype(F32)).astype(BF16)
    out_ref[0] = part + _dot(ya, wo_ref[0:256, :])


def _attention(qT, kaug, vT, avg, crow, causal, cqT, ck, cvT, band, sink, gates, x, ybd, wo):
    B, nq = qT.shape[0], qT.shape[1]
    S = kaug.shape[2]
    nb = S // MOBA_BLOCK
    nkb = S // SWA_BLOCK
    nchain = (Q_TILE // SWA_BLOCK) * C_KV_HEADS
    win = 2 * SWA_BLOCK
    tile = lambda col: pl.BlockSpec((1, Q_TILE, 256), lambda b, i: (b, i, col))
    in_specs = [
        pl.BlockSpec((1, 1, 256, Q_TILE), lambda b, i: (b, i, 0, 0)),
        pl.BlockSpec((1, N_HEADS, S, AUG), lambda b, i: (b, 0, 0, 0)),
        pl.BlockSpec((1, nq, 256, Q_TILE), lambda b, i: (b, 0, 0, 0)),
        pl.BlockSpec((nb, S), lambda b, i: (0, 0)),
        pl.BlockSpec((N_HEADS, AUG - COL_POS, Q_TILE), lambda b, i: (0, 0, 0)),
        pl.BlockSpec((2 * MOBA_BLOCK, Q_TILE), lambda b, i: (0, 0)),
        tile(0),
        pl.BlockSpec((1, 1, 256, Q_TILE), lambda b, i: (b, i, 0, 0)),
        pl.BlockSpec((1, S, 128), lambda b, i: (b, 0, 0)),
        pl.BlockSpec((1, nkb, 128, SWA_BLOCK), lambda b, i: (b, 0, 0, 0)),
        pl.BlockSpec((C_KV_HEADS, 1, win, win), lambda b, i: (0, jnp.minimum(i, 1), 0, 0)),
        pl.BlockSpec((C_KV_HEADS, 1, win, win), lambda b, i: (0, 1, 0, 0)),
        pl.BlockSpec((C_KV_HEADS, 1, win), lambda b, i: (0, 0, 0)),
        tile(1),
        pl.BlockSpec((1, Q_TILE, D_MODEL), lambda b, i: (b, i, 0)),
        pl.BlockSpec((1, Q_TILE, 512), lambda b, i: (b, i, 0)),
        pl.BlockSpec((D_MODEL, D_MODEL), lambda b, i: (0, 0)),
    ]
    return pl.pallas_call(
        _attn_kernel, grid=(B, nq), in_specs=in_specs,
        out_specs=pl.BlockSpec((1, Q_TILE, D_MODEL), lambda b, i: (b, i, 0)),
        out_shape=jax.ShapeDtypeStruct((B, S, D_MODEL), F32),
        scratch_shapes=[pltpu.VMEM((N_HEADS, nb, AUG), F32),
                        pltpu.VMEM((N_HEADS, AUG, Q_TILE), BF16),
                        pltpu.VMEM((N_HEADS, HEAD_DIM, Q_TILE), F32),
                        pltpu.VMEM((N_HEADS, 2 * MOBA_BLOCK, Q_TILE), F32),
                        pltpu.VMEM((N_HEADS, 2 * MOBA_BLOCK, Q_TILE), BF16),
                        pltpu.VMEM((N_HEADS, 8, Q_TILE), F32),
                        pltpu.VMEM((nchain, win, win), F32),
                        pltpu.VMEM((nchain, win, win), BF16),
                        pltpu.VMEM((nchain, 8, win), F32)],
        compiler_params=pltpu.CompilerParams(dimension_semantics=("parallel", "arbitrary"),
                                             vmem_limit_bytes=VMEM_LIMIT),
        name="attention",
    )(qT, kaug, vT, avg, crow, causal, gates, cqT, ck, cvT, band, band, sink, gates, x, ybd, wo)


def _swa_front(i, blocks, cqT_ref, ck_ref, band0_ref, band_ref, sink_ref, s_ref, p_ref, mx_ref):
    win = 2 * SWA_BLOCK
    rows = 64
    zrows = jnp.zeros((HEAD_DIM, win), BF16)
    chains = [(t, kv) for t in range(blocks) for kv in range(C_KV_HEADS)]
    for t in range(blocks):
        n = i * blocks + t
        prev = jnp.maximum(n - 1, 0)
        kwin = jnp.concatenate([
            ck_ref[0, pl.ds(pl.multiple_of(prev * SWA_BLOCK, SWA_BLOCK), SWA_BLOCK), :],
            ck_ref[0, pl.ds(pl.multiple_of(n * SWA_BLOCK, SWA_BLOCK), SWA_BLOCK), :]], axis=0)
        for kv in range(C_KV_HEADS):
            c = chains.index((t, kv))
            q2 = jnp.concatenate([
                cqT_ref[0, t // 2, (2 * kv + g) * HEAD_DIM:(2 * kv + g + 1) * HEAD_DIM,
                        (t % 2) * SWA_BLOCK:(t % 2 + 1) * SWA_BLOCK] for g in range(2)], axis=1)
            qz = jnp.concatenate([q2, zrows] if kv == 0 else [zrows, q2], axis=0)
            band = band0_ref[kv, 0] if t == 0 else band_ref[kv, 0]
            s = _dot(kwin, qz) + band
            s_ref[c] = s
            mx_ref[c] = _fold8(s, jnp.maximum)
    for c, (t, kv) in enumerate(chains):
        m = jnp.max(mx_ref[c], axis=0, keepdims=True)
        l8 = jnp.zeros((8, win), F32)
        for r in range(win // rows):
            p = jnp.exp2(s_ref[c, r * rows:(r + 1) * rows] - m)
            p_ref[c, r * rows:(r + 1) * rows] = p.astype(BF16)
            l8 = l8 + _fold8(p, jnp.add)
        mx_ref[c, 0:1] = 1.0 / (jnp.sum(l8, axis=0, keepdims=True) + jnp.exp2(sink_ref[kv] - m))


def _swa_back(i, blocks, cvT_ref, p_ref, mx_ref, gate_ref):
    chains = [(t, kv) for t in range(blocks) for kv in range(C_KV_HEADS)]
    outs = {}
    for c, (t, kv) in enumerate(chains):
        n = i * blocks + t
        prev = jnp.maximum(n - 1, 0)
        vwin = jnp.concatenate([cvT_ref[0, prev, kv * HEAD_DIM:(kv + 1) * HEAD_DIM, :],
                                cvT_ref[0, n, kv * HEAD_DIM:(kv + 1) * HEAD_DIM, :]], axis=1)
        outs[t, kv] = _dot(vwin, p_ref[c]) * mx_ref[c, 0:1]
    tiles = []
    for u in range(blocks * SWA_BLOCK // Q_TILE):
        head_rows = []
        for kv in range(C_KV_HEADS):
            for g in range(2):
                head_rows.append(jnp.concatenate(
                    [outs[t, kv][:, g * SWA_BLOCK:(g + 1) * SWA_BLOCK] for t in (2 * u, 2 * u + 1)], axis=1))
        o = jnp.concatenate(head_rows, axis=0).T
        gate = gate_ref[0, u * Q_TILE:(u + 1) * Q_TILE, :].astype(F32)
        tiles.append((o * gate).astype(BF16))
    return tiles


def _tables(S):
    tm = IN_TILE
    slopes_c, slopes_a = SLOPES_C, SLOPES_A
    pos = np.arange(S)

    kca = np.zeros((S, AUG), np.float32)
    blk, r = pos // MOBA_BLOCK, pos % MOBA_BLOCK
    kca[pos, COL_ONEHOT + blk] = 1.0
    kca[:, COL_POS] = r
    kca[:, COL_POS + 1] = r
    kca[:, COL_POS + 2] = blk
    kca[:, COL_POS + 3] = blk
    kca = jnp.asarray(kca, BF16)

    c = slopes_a * np.float32(LOG2E)
    hi = c.astype(BF16).astype(np.float32)
    lo = c - hi
    rows = np.stack([hi, lo, hi * MOBA_BLOCK, lo * MOBA_BLOCK], axis=-1)
    rows = np.pad(rows, ((0, 0), (0, AUG - COL_POS - 4)))
    crow_a = jnp.asarray(np.broadcast_to(rows[:, :, None], rows.shape + (Q_TILE,)), BF16)

    nb = S // MOBA_BLOCK
    avg = jnp.asarray((pos[None, :] // MOBA_BLOCK == np.arange(nb)[:, None]) / MOBA_BLOCK, BF16)
    kq = np.arange(MOBA_BLOCK)
    causal = np.where(kq[:, None] <= kq[None, :], 0.0, NEG_BIG)
    causal = jnp.asarray(np.concatenate([causal, np.zeros_like(causal)], axis=0), F32)
    u = np.arange(2 * SWA_BLOCK)[:, None]
    t = np.arange(SWA_BLOCK)[None, :]
    ok = (u > t) & (u <= t + SWA_BLOCK)
    dist = (SWA_BLOCK + t - u).astype(np.float32)
    sc = np.asarray(slopes_c, np.float32) * np.float32(LOG2E)
    band = []
    for kv in range(C_KV_HEADS):
        bias = np.concatenate([-sc[2 * kv] * dist, -sc[2 * kv + 1] * dist], axis=1)
        ok2 = np.concatenate([ok, ok], axis=1)
        band.append(np.stack([np.where(ok2 & (u >= SWA_BLOCK), bias, NEG_BIG),
                              np.where(ok2, bias, NEG_BIG)]))
    band = jnp.asarray(np.stack(band), F32)
    w = np.repeat(np.asarray(POOL_WINDOWS, np.float32), 64)[None, :]
    first = 1.0 / np.minimum(np.arange(tm, dtype=np.float32)[:, None] + 1.0, w)
    invcnt = jnp.asarray(np.stack([first, np.broadcast_to(1.0 / w, (tm, 256))]), F32)
    return dict(kca=kca, crow_a=crow_a, avg=avg, causal=causal, band=band, invcnt=invcnt)


def kernel(x, norm_g, w_in, w_out, a_q_norm, a_k_norm, pool_w, pool_scale, c_q_norm, c_k_norm, c_sinks, conv_w):
    B, S, _ = x.shape
    depth = norm_g.shape[0]
    assert S % IN_TILE == 0 and S // MOBA_BLOCK <= 32
    tb = _tables(S)
    tm = IN_TILE
    d = np.arange(256)
    bd = jnp.asarray(d[:, None] // HEAD_DIM == d[None, :] // HEAD_DIM, BF16)
    qscale = QK_SCALE * LOG2E
    for l in range(depth):
        gq = jnp.broadcast_to(jnp.tile(a_q_norm[l] * qscale, N_HEADS)[:, None], (256, tm))
        gcq = jnp.broadcast_to(jnp.tile(c_q_norm[l] * qscale, N_HEADS)[:, None], (256, tm))
        gk = jnp.tile(a_k_norm[l], N_HEADS)[None, :]
        gck = jnp.tile(c_k_norm[l], C_KV_HEADS)[None, :]
        pw = jnp.zeros((256, 256), F32)
        for g in range(4):
            pw = pw.at[g * 64:(g + 1) * 64, g * 64:(g + 1) * 64].set(pool_w[l, g])
        qT, kaug, vT, cqT, ck, cvT, gates, ybd = _in_proj(
            x, norm_g[l][None, :], w_in[l].astype(BF16), bd, gq, gk, gcq, gck, tb['kca'],
            tb['invcnt'], pw.astype(BF16), pool_scale[l][None, :], conv_w[l])
        sink = jnp.repeat((c_sinks[l] * LOG2E).reshape(C_KV_HEADS, 2), SWA_BLOCK, axis=1)[:, None, :]
        x = _attention(qT, kaug, vT, tb['avg'], tb['crow_a'], tb['causal'],
                       cqT, ck, cvT, tb['band'], sink, gates,
                       x, ybd, w_out[l].astype(BF16))
    return x
```

```python
import math

import jax
import jax.numpy as jnp
import numpy as np
from jax import lax
from jax.experimental import pallas as pl
from jax.experimental.pallas import tpu as pltpu

F32 = jnp.float32
BF16 = jnp.bfloat16

D_MODEL = 1024
HEAD_DIM = 64
N_HEADS = 4
C_KV_HEADS = 2
NORM_EPS = 1e-6
MOBA_BLOCK = 256
MOBA_TOPK = 3
SWA_BLOCK = 128
POOL_WINDOWS = (2, 4, 8, 16)
CONV_WIDTH = 3
HALO = 16

AUG = 128
COL_ONEHOT = 64
COL_POS = 96
NEG_BIG = -(2.0 ** 100)
LOG2E = math.log2(math.e)
QK_SCALE = HEAD_DIM ** -0.5
SLOPES = np.exp2(-(8.0 / (2 * N_HEADS)) * np.arange(1, 2 * N_HEADS + 1)).astype(np.float32)
SLOPES_C, SLOPES_A = SLOPES[:N_HEADS], SLOPES[N_HEADS:]

IN_TILE = 1024
Q_TILE = 256
VMEM_LIMIT = 56 * 1024 * 1024

SEG = dict(aq=0, ak=256, av=512, ag=768, bu=1024, bg=1280, cq=1536, ck=1792, cv=1920,
           cg=2048, dh=2304, db=2560, dc=2816, dg=3072)
IN_PROJ_WIDTH = 3328


def _dot(a, b):
    return jnp.dot(a, b, preferred_element_type=F32)


def _silu(v):
    h = 0.5 * v
    return h + h * jnp.tanh(h)


def _fold8(x, op):
    out = x[0:8]
    for g in range(1, x.shape[0] // 8):
        out = op(out, x[g * 8:(g + 1) * 8])
    return out


def _in_proj_kernel(x_ref, g_ref, w_ref, bd_ref, gq_ref, gk_ref, gcq_ref, gck_ref, kca_ref,
                    invcnt_ref, pw_ref, pscale_ref, cw_ref,
                    qT_ref, kaug_ref, vT_ref, cqT_ref, ck_ref, cvT_ref, gates_ref, ybd_ref,
                    hist_ref):
    tm = x_ref.shape[1]
    s_idx = pl.program_id(1)
    x = x_ref[0]
    ms = jnp.mean(x * x, axis=-1, keepdims=True)
    h = (x * lax.rsqrt(ms + NORM_EPS) * g_ref[...]).astype(BF16)

    def proj(lo, width):
        return _dot(h, w_ref[:, lo:lo + width])

    def norm_t(p, gain_ref):
        pt = p.T
        outs = []
        for hh in range(N_HEADS):
            ph = pt[hh * HEAD_DIM:(hh + 1) * HEAD_DIM]
            ss = jnp.sum(ph * ph, axis=0, keepdims=True) * (1.0 / HEAD_DIM)
            outs.append(ph * lax.rsqrt(ss + NORM_EPS))
        return (jnp.concatenate(outs, axis=0) * gain_ref[...]).astype(BF16)

    def norm_rows(p, gain_ref, width):
        ss = _dot((p * p).astype(BF16), bd_ref[0:width, 0:width]) * (1.0 / HEAD_DIM)
        return p * lax.rsqrt(ss + NORM_EPS) * gain_ref[...]

    def build_kaug(kn, kconst):
        lane = lax.broadcasted_iota(jnp.int32, (tm, AUG), 1)
        blocks = []
        for hh in range(N_HEADS):
            col = kn[:, (hh // 2) * AUG:(hh // 2 + 1) * AUG]
            if hh % 2 == 1:
                col = pltpu.roll(col, HEAD_DIM, axis=1)
            blocks.append(jnp.where(lane < HEAD_DIM, col, kconst).astype(BF16))
        return blocks


    @pl.when(s_idx == 0)
    def _():
        hist_ref[...] = jnp.zeros(hist_ref.shape, F32)

    back = lambda v, k: pltpu.roll(v, k, axis=0)

    pd = proj(SEG['dh'], 1024)
    dh, db, dc, dg = (pd[:, k * 256:(k + 1) * 256] for k in range(4))
    u = dc * dh
    ue = jnp.concatenate([hist_ref[1], u], axis=0)
    hist_ref[1] = u[tm - HALO:]
    conv = (cw_ref[0:1, :] * back(ue, 2) + cw_ref[1:2, :] * back(ue, 1) + cw_ref[2:3, :] * ue)[HALO:]
    ybd_ref[0, :, 256:512] = (db * conv * _silu(dg)).astype(BF16)

    pb = proj(SEG['bu'], 512)
    bu, bg = pb[:, 0:256], pb[:, 256:512]
    e = jnp.concatenate([hist_ref[0], bu], axis=0)
    hist_ref[0] = bu[tm - HALO:]
    a2 = e + back(e, 1)
    a4 = a2 + back(a2, 2)
    a4r = a4[:, 128:256]
    a8 = a4r + back(a4r, 4)
    a16 = a8 + back(a8, 8)
    lane = lax.broadcasted_iota(jnp.int32, (HALO + tm, 128), 1)
    sums = jnp.concatenate([jnp.where(lane < 64, a2[:, 0:128], a4[:, 0:128]),
                            jnp.where(lane < 64, a8, a16)], axis=1)
    pooled = sums[HALO:] * invcnt_ref[0] - bu
    yb = _dot(pooled.astype(BF16), pw_ref[...]) * pscale_ref[...] * _silu(bg)
    ybd_ref[0, :, 0:256] = yb.astype(BF16)

    nblk = tm // Q_TILE
    qT = norm_t(proj(SEG['aq'], 256), gq_ref)
    for t in range(nblk):
        qT_ref[0, t] = qT[:, t * Q_TILE:(t + 1) * Q_TILE]
    kn = norm_rows(proj(SEG['ak'], 256), gk_ref, 256)
    kaug = build_kaug(kn, kca_ref[...].astype(F32))
    for hh in range(N_HEADS):
        kaug_ref[0, hh] = kaug[hh]
    vT = proj(SEG['av'], 256).T.astype(BF16)
    for t in range(nblk):
        vT_ref[0, t] = vT[:, t * Q_TILE:(t + 1) * Q_TILE]
    cqT = norm_t(proj(SEG['cq'], 256), gcq_ref)
    for t in range(nblk):
        cqT_ref[0, t] = cqT[:, t * Q_TILE:(t + 1) * Q_TILE]
    ckv = proj(SEG['ck'], 256)
    ck_ref[0] = norm_rows(ckv[:, 0:128], gck_ref, 128).astype(BF16)
    cvT = ckv[:, 128:256].T.astype(BF16)
    for t in range(tm // SWA_BLOCK):
        cvT_ref[0, t] = cvT[:, t * SWA_BLOCK:(t + 1) * SWA_BLOCK]
    gates_ref[0, :, 0:256] = _silu(proj(SEG['ag'], 256)).astype(BF16)
    gates_ref[0, :, 256:512] = _silu(proj(SEG['cg'], 256)).astype(BF16)


def _in_proj(x, g, w, bd, gq, gk, gcq, gck, kca, invcnt, pw, pscale, cw):
    B, S, _ = x.shape
    tm = IN_TILE
    ns = S // tm
    nq = S // Q_TILE
    nkb = S // SWA_BLOCK
    const = lambda *shape: pl.BlockSpec(shape, lambda b, s: (0,) * len(shape))
    out_shape = (
        jax.ShapeDtypeStruct((B, nq, 256, Q_TILE), BF16),
        jax.ShapeDtypeStruct((B, N_HEADS, S, AUG), BF16),
        jax.ShapeDtypeStruct((B, nq, 256, Q_TILE), BF16),
        jax.ShapeDtypeStruct((B, nq, 256, Q_TILE), BF16),
        jax.ShapeDtypeStruct((B, S, 128), BF16),
        jax.ShapeDtypeStruct((B, nkb, 128, SWA_BLOCK), BF16),
        jax.ShapeDtypeStruct((B, S, 512), BF16),
        jax.ShapeDtypeStruct((B, S, 512), BF16),
    )
    r = tm // Q_TILE
    out_specs = (
        pl.BlockSpec((1, r, 256, Q_TILE), lambda b, s: (b, s, 0, 0)),
        pl.BlockSpec((1, N_HEADS, tm, AUG), lambda b, s: (b, 0, s, 0)),
        pl.BlockSpec((1, r, 256, Q_TILE), lambda b, s: (b, s, 0, 0)),
        pl.BlockSpec((1, r, 256, Q_TILE), lambda b, s: (b, s, 0, 0)),
        pl.BlockSpec((1, tm, 128), lambda b, s: (b, s, 0)),
        pl.BlockSpec((1, tm // SWA_BLOCK, 128, SWA_BLOCK), lambda b, s: (b, s, 0, 0)),
        pl.BlockSpec((1, tm, 512), lambda b, s: (b, s, 0)),
        pl.BlockSpec((1, tm, 512), lambda b, s: (b, s, 0)),
    )
    in_specs = [
        pl.BlockSpec((1, tm, D_MODEL), lambda b, s: (b, s, 0)),
        const(1, D_MODEL),
        const(D_MODEL, IN_PROJ_WIDTH),
        const(256, 256),
        const(256, tm), const(1, 256), const(256, tm), const(1, 128),
        pl.BlockSpec((tm, AUG), lambda b, s: (s, 0)),
        pl.BlockSpec((1, tm, 256), lambda b, s: (jnp.minimum(s, 1), 0, 0)),
        const(256, 256), const(1, 256), const(CONV_WIDTH, 256),
    ]
    return pl.pallas_call(
        _in_proj_kernel, grid=(B, ns), in_specs=in_specs, out_specs=out_specs, out_shape=out_shape,
        scratch_shapes=[pltpu.VMEM((2, HALO, 256), F32)],
        compiler_params=pltpu.CompilerParams(dimension_semantics=("parallel", "arbitrary"),
                                             vmem_limit_bytes=VMEM_LIMIT),
        name="in_proj",
    )(x, g, w, bd, gq, gk, gcq, gck, kca, invcnt, pw, pscale, cw)


def _attn_kernel(qT_ref, kaug_ref, vT_ref, avg_ref, crow_ref, causal_ref, gate_ref,
                 cqT_ref, ck_ref, cvT_ref, band0_ref, band_ref, sink_ref, cgate_ref,
                 x_ref, ybd_ref, wo_ref,
                 out_ref,
                 kmean_ref, qaug_ref, acc_ref, s_ref, p_ref, mx_ref, cs_ref, cp_ref, cmx_ref):
    i = pl.program_id(1)
    swa_blocks = Q_TILE // SWA_BLOCK
    S = kaug_ref.shape[2]
    nb = S // MOBA_BLOCK
    chunk = 1024

    @pl.when(i == 0)
    def _():
        for hh in range(N_HEADS):
            acc = jnp.zeros((nb, AUG), F32)
            for c in range(S // chunk):
                acc = acc + _dot(avg_ref[:, c * chunk:(c + 1) * chunk],
                                 kaug_ref[0, hh, c * chunk:(c + 1) * chunk, :])
            kmean_ref[hh] = acc

    row = lax.broadcasted_iota(jnp.int32, (nb, Q_TILE), 0).astype(F32)
    i_f = i.astype(F32)
    neg_inf = jnp.float32(-jnp.inf)
    heads = range(N_HEADS)
    vh = lambda hh, j: vT_ref[0, j, hh * HEAD_DIM:(hh + 1) * HEAD_DIM, :]
    kb = lambda hh, j: kaug_ref[0, hh, pl.ds(pl.multiple_of(j * MOBA_BLOCK, MOBA_BLOCK), MOBA_BLOCK), :]

    for hh in heads:
        qh = qT_ref[0, 0, hh * HEAD_DIM:(hh + 1) * HEAD_DIM, :]
        km = kmean_ref[hh][:, 0:HEAD_DIM]
        km_hi = km.astype(BF16)
        km_lo = (km - km_hi.astype(F32)).astype(BF16)
        bs = _dot(km_hi, qh) + _dot(km_lo, qh)
        xs = jnp.where(row < i_f, bs, neg_inf)
        sel = row == i_f
        for _ in range(MOBA_TOPK):
            mx = jnp.max(xs, axis=0, keepdims=True)
            first = jnp.min(jnp.where(xs == mx, row, jnp.float32(nb)), axis=0, keepdims=True)
            pick = row == first
            sel = sel | (pick & (mx > neg_inf))
            xs = jnp.where(pick, neg_inf, xs)
        negmask = jnp.where(sel, 0.0, NEG_BIG).astype(BF16)
        pad = jnp.zeros((COL_POS - COL_ONEHOT - nb, Q_TILE), BF16) if nb < 32 else None
        parts = [qh, negmask] + ([pad] if pad is not None else []) + [crow_ref[hh]]
        qaug_ref[hh] = jnp.concatenate(parts, axis=0)

    _swa_front(i, swa_blocks, cqT_ref, ck_ref, band0_ref, band_ref, sink_ref, cs_ref, cp_ref, cmx_ref)

    def blk(pos):
        past = jnp.minimum(i + 1, nb - 1)
        return jnp.where(pos <= 0, i, jnp.where(pos <= i, pos - 1, past))

    kblk = MOBA_BLOCK
    sub = 8
    rows = 64

    def scores(hh, j0, j1, mask=None):
        s = _dot(jnp.concatenate([kb(hh, j0), kb(hh, j1)], axis=0), qaug_ref[hh])
        if mask is not None:
            s = s + mask
        s_ref[hh] = s
        mx_ref[hh] = _fold8(s, jnp.maximum)

    def scores_head(hh, t):
        scores(hh, blk(2 * t), blk(2 * t + 1))

    def scores_step(t):
        for hh in heads:
            scores_head(hh, t)

    def softmax_head(hh, m_old, l_old):
        m_new = jnp.maximum(m_old, jnp.max(mx_ref[hh], axis=0, keepdims=True))
        alpha = jnp.exp2(m_old - m_new)
        l8 = alpha * l_old
        for c in range(2 * kblk // rows):
            p = jnp.exp2(s_ref[hh, c * rows:(c + 1) * rows] - m_new)
            p_ref[hh, c * rows:(c + 1) * rows] = p.astype(BF16)
            l8 = l8 + _fold8(p, jnp.add)
        return m_new, l8, alpha

    def softmax_step(carry):
        ms, ls, _ = carry
        new = [softmax_head(hh, ms[hh], ls[hh]) for hh in heads]
        return tuple(zip(*new))

    def values_head(hh, t, alpha):
        acc_ref[hh] = (alpha * acc_ref[hh]
                       + _dot(vh(hh, blk(2 * t)), p_ref[hh, 0:kblk])
                       + _dot(vh(hh, blk(2 * t + 1)), p_ref[hh, kblk:2 * kblk]))

    def values_step(t, alphas):
        for hh in heads:
            values_head(hh, t, alphas[hh])

    def body(t, carry):
        ms, ls, alphas = carry
        new = []
        for hh in heads:
            values_head(hh, t - 1, alphas[hh])
            new.append(softmax_head(hh, ms[hh], ls[hh]))
            scores_head(hh, t + 1)
        return tuple(zip(*new))

    def drain(carry):
        values_step(steps - 2, carry[2])
        return softmax_step(carry)

    steps = (i + 2) // 2
    neg = jnp.full((1, Q_TILE), NEG_BIG, F32)
    zero8 = jnp.zeros((sub, Q_TILE), F32)
    acc_ref[...] = jnp.zeros(acc_ref.shape, F32)
    for hh in heads:
        scores(hh, i, blk(1), causal_ref[...])
    carry = softmax_step(((neg,) * N_HEADS, (zero8,) * N_HEADS, None))
    scores_step(1)
    UNROLL = 4

    def run(first, count):
        def go(c):
            for t in range(count):
                c = body(first + t, c)
            return c
        return go

    trips = jnp.maximum(steps - 2, 0)
    groups = trips // UNROLL
    carry = lax.fori_loop(0, groups, lambda k, c: run(UNROLL * k + 1, UNROLL)(c), carry)
    done = UNROLL * groups
    carry = lax.cond((trips - done) >= 2, run(done + 1, 2), lambda c: c, carry)
    carry = lax.cond((trips - done) % 2 == 1, run(steps - 2, 1), lambda c: c, carry)
    carry = lax.cond(steps >= 2, drain, lambda c: c, carry)
    ms, ls, alphas = carry
    values_step(steps - 1, alphas)
    yc = _swa_back(i, swa_blocks, cvT_ref, cp_ref, cmx_ref, cgate_ref)[0]
    part = (x_ref[0] + _dot(ybd_ref[0, :, 0:256], wo_ref[256:512, :]) + _dot(yc, wo_ref[512:768, :])
            + _dot(ybd_ref[0, :, 256:512], wo_ref[768:1024, :]))
    outs = [acc_ref[hh] / jnp.sum(ls[hh], axis=0, keepdims=True) for hh in heads]
    o = jnp.concatenate(outs, axis=0).T
    ya = (o * gate_ref[0].astype(F32)).astype(BF16)
    out_ref[0] = part + _dot(ya, wo_ref[0:256, :])


def _attention(qT, kaug, vT, avg, crow, causal, cqT, ck, cvT, band, sink, gates, x, ybd, wo):
    B, nq = qT.shape[0], qT.shape[1]
    S = kaug.shape[2]
    nb = S // MOBA_BLOCK
    nkb = S // SWA_BLOCK
    nchain = (Q_TILE // SWA_BLOCK) * C_KV_HEADS
    win = 2 * SWA_BLOCK
    tile = lambda col: pl.BlockSpec((1, Q_TILE, 256), lambda b, i: (b, i, col))
    in_specs = [
        pl.BlockSpec((1, 1, 256, Q_TILE), lambda b, i: (b, i, 0, 0)),
        pl.BlockSpec((1, N_HEADS, S, AUG), lambda b, i: (b, 0, 0, 0)),
        pl.BlockSpec((1, nq, 256, Q_TILE), lambda b, i: (b, 0, 0, 0)),
        pl.BlockSpec((nb, S), lambda b, i: (0, 0)),
        pl.BlockSpec((N_HEADS, AUG - COL_POS, Q_TILE), lambda b, i: (0, 0, 0)),
        pl.BlockSpec((2 * MOBA_BLOCK, Q_TILE), lambda b, i: (0, 0)),
        tile(0),
        pl.BlockSpec((1, 1, 256, Q_TILE), lambda b, i: (b, i, 0, 0)),
        pl.BlockSpec((1, S, 128), lambda b, i: (b, 0, 0)),
        pl.BlockSpec((1, nkb, 128, SWA_BLOCK), lambda b, i: (b, 0, 0, 0)),
        pl.BlockSpec((C_KV_HEADS, 1, win, win), lambda b, i: (0, jnp.minimum(i, 1), 0, 0)),
        pl.BlockSpec((C_KV_HEADS, 1, win, win), lambda b, i: (0, 1, 0, 0)),
        pl.BlockSpec((C_KV_HEADS, 1, win), lambda b, i: (0, 0, 0)),
        tile(1),
        pl.BlockSpec((1, Q_TILE, D_MODEL), lambda b, i: (b, i, 0)),
        pl.BlockSpec((1, Q_TILE, 512), lambda b, i: (b, i, 0)),
        pl.BlockSpec((D_MODEL, D_MODEL), lambda b, i: (0, 0)),
    ]
    return pl.pallas_call(
        _attn_kernel, grid=(B, nq), in_specs=in_specs,
        out_specs=pl.BlockSpec((1, Q_TILE, D_MODEL), lambda b, i: (b, i, 0)),
        out_shape=jax.ShapeDtypeStruct((B, S, D_MODEL), F32),
        scratch_shapes=[pltpu.VMEM((N_HEADS, nb, AUG), F32),
                        pltpu.VMEM((N_HEADS, AUG, Q_TILE), BF16),
                        pltpu.VMEM((N_HEADS, HEAD_DIM, Q_TILE), F32),
                        pltpu.VMEM((N_HEADS, 2 * MOBA_BLOCK, Q_TILE), F32),
                        pltpu.VMEM((N_HEADS, 2 * MOBA_BLOCK, Q_TILE), BF16),
                        pltpu.VMEM((N_HEADS, 8, Q_TILE), F32),
                        pltpu.VMEM((nchain, win, win), F32),
                        pltpu.VMEM((nchain, win, win), BF16),
                        pltpu.VMEM((nchain, 8, win), F32)],
        compiler_params=pltpu.CompilerParams(dimension_semantics=("parallel", "arbitrary"),
                                             vmem_limit_bytes=VMEM_LIMIT),
        name="attention",
    )(qT, kaug, vT, avg, crow, causal, gates, cqT, ck, cvT, band, band, sink, gates, x, ybd, wo)


def _swa_front(i, blocks, cqT_ref, ck_ref, band0_ref, band_ref, sink_ref, s_ref, p_ref, mx_ref):
    win = 2 * SWA_BLOCK
    rows = 64
    zrows = jnp.zeros((HEAD_DIM, win), BF16)
    chains = [(t, kv) for t in range(blocks) for kv in range(C_KV_HEADS)]
    for t in range(blocks):
        n = i * blocks + t
        prev = jnp.maximum(n - 1, 0)
        kwin = jnp.concatenate([
            ck_ref[0, pl.ds(pl.multiple_of(prev * SWA_BLOCK, SWA_BLOCK), SWA_BLOCK), :],
            ck_ref[0, pl.ds(pl.multiple_of(n * SWA_BLOCK, SWA_BLOCK), SWA_BLOCK), :]], axis=0)
        for kv in range(C_KV_HEADS):
            c = chains.index((t, kv))
            q2 = jnp.concatenate([
                cqT_ref[0, t // 2, (2 * kv + g) * HEAD_DIM:(2 * kv + g + 1) * HEAD_DIM,
                        (t % 2) * SWA_BLOCK:(t % 2 + 1) * SWA_BLOCK] for g in range(2)], axis=1)
            qz = jnp.concatenate([q2, zrows] if kv == 0 else [zrows, q2], axis=0)
            band = band0_ref[kv, 0] if t == 0 else band_ref[kv, 0]
            s = _dot(kwin, qz) + band
            s_ref[c] = s
            mx_ref[c] = _fold8(s, jnp.maximum)
    for c, (t, kv) in enumerate(chains):
        m = jnp.max(mx_ref[c], axis=0, keepdims=True)
        l8 = jnp.zeros((8, win), F32)
        for r in range(win // rows):
            p = jnp.exp2(s_ref[c, r * rows:(r + 1) * rows] - m)
            p_ref[c, r * rows:(r + 1) * rows] = p.astype(BF16)
            l8 = l8 + _fold8(p, jnp.add)
        mx_ref[c, 0:1] = 1.0 / (jnp.sum(l8, axis=0, keepdims=True) + jnp.exp2(sink_ref[kv] - m))


def _swa_back(i, blocks, cvT_ref, p_ref, mx_ref, gate_ref):
    chains = [(t, kv) for t in range(blocks) for kv in range(C_KV_HEADS)]
    outs = {}
    for c, (t, kv) in enumerate(chains):
        n = i * blocks + t
        prev = jnp.maximum(n - 1, 0)
        vwin = jnp.concatenate([cvT_ref[0, prev, kv * HEAD_DIM:(kv + 1) * HEAD_DIM, :],
                                cvT_ref[0, n, kv * HEAD_DIM:(kv + 1) * HEAD_DIM, :]], axis=1)
        outs[t, kv] = _dot(vwin, p_ref[c]) * mx_ref[c, 0:1]
    tiles = []
    for u in range(blocks * SWA_BLOCK // Q_TILE):
        head_rows = []
        for kv in range(C_KV_HEADS):
            for g in range(2):
                head_rows.append(jnp.concatenate(
                    [outs[t, kv][:, g * SWA_BLOCK:(g + 1) * SWA_BLOCK] for t in (2 * u, 2 * u + 1)], axis=1))
        o = jnp.concatenate(head_rows, axis=0).T
        gate = gate_ref[0, u * Q_TILE:(u + 1) * Q_TILE, :].astype(F32)
        tiles.append((o * gate).astype(BF16))
    return tiles


def _tables(S):
    tm = IN_TILE
    slopes_c, slopes_a = SLOPES_C, SLOPES_A
    pos = np.arange(S)

    kca = np.zeros((S, AUG), np.float32)
    blk, r = pos // MOBA_BLOCK, pos % MOBA_BLOCK
    kca[pos, COL_ONEHOT + blk] = 1.0
    kca[:, COL_POS] = r
    kca[:, COL_POS + 1] = r
    kca[:, COL_POS + 2] = blk
    kca[:, COL_POS + 3] = blk
    kca = jnp.asarray(kca, BF16)

    c = slopes_a * np.float32(LOG2E)
    hi = c.astype(BF16).astype(np.float32)
    lo = c - hi
    rows = np.stack([hi, lo, hi * MOBA_BLOCK, lo * MOBA_BLOCK], axis=-1)
    rows = np.pad(rows, ((0, 0), (0, AUG - COL_POS - 4)))
    crow_a = jnp.asarray(np.broadcast_to(rows[:, :, None], rows.shape + (Q_TILE,)), BF16)

    nb = S // MOBA_BLOCK
    avg = jnp.asarray((pos[None, :] // MOBA_BLOCK == np.arange(nb)[:, None]) / MOBA_BLOCK, BF16)
    kq = np.arange(MOBA_BLOCK)
    causal = np.where(kq[:, None] <= kq[None, :], 0.0, NEG_BIG)
    causal = jnp.asarray(np.concatenate([causal, np.zeros_like(causal)], axis=0), F32)
    u = np.arange(2 * SWA_BLOCK)[:, None]
    t = np.arange(SWA_BLOCK)[None, :]
    ok = (u > t) & (u <= t + SWA_BLOCK)
    dist = (SWA_BLOCK + t - u).astype(np.float32)
    sc = np.asarray(slopes_c, np.float32) * np.float32(LOG2E)
    band = []
    for kv in range(C_KV_HEADS):
        bias = np.concatenate([-sc[2 * kv] * dist, -sc[2 * kv + 1] * dist], axis=1)
        ok2 = np.concatenate([ok, ok], axis=1)
        band.append(np.stack([np.where(ok2 & (u >= SWA_BLOCK), bias, NEG_BIG),
                              np.where(ok2, bias, NEG_BIG)]))
    band = jnp.asarray(np.stack(band), F32)
    w = np.repeat(np.asarray(POOL_WINDOWS, np.float32), 64)[None, :]
    first = 1.0 / np.minimum(np.arange(tm, dtype=np.float32)[:, None] + 1.0, w)
    invcnt = jnp.asarray(np.stack([first, np.broadcast_to(1.0 / w, (tm, 256))]), F32)
    return dict(kca=kca, crow_a=crow_a, avg=avg, causal=causal, band=band, invcnt=invcnt)


def kernel(x, norm_g, w_in, w_out, a_q_norm, a_k_norm, pool_w, pool_scale, c_q_norm, c_k_norm, c_sinks, conv_w):
    B, S, _ = x.shape
    depth = norm_g.shape[0]
    assert S % IN_TILE == 0 and S // MOBA_BLOCK <= 32
    tb = _tables(S)
    tm = IN_TILE
    d = np.arange(256)
    bd = jnp.asarray(d[:, None] // HEAD_DIM == d[None, :] // HEAD_DIM, BF16)
    qscale = QK_SCALE * LOG2E
    for l in range(depth):
        gq = jnp.broadcast_to(jnp.tile(a_q_norm[l] * qscale, N_HEADS)[:, None], (256, tm))
        gcq = jnp.broadcast_to(jnp.tile(c_q_norm[l] * qscale, N_HEADS)[:, None], (256, tm))
        gk = jnp.tile(a_k_norm[l], N_HEADS)[None, :]
        gck = jnp.tile(c_k_norm[l], C_KV_HEADS)[None, :]
        pw = jnp.zeros((256, 256), F32)
        for g in range(4):
            pw = pw.at[g * 64:(g + 1) * 64, g * 64:(g + 1) * 64].set(pool_w[l, g])
        qT, kaug, vT, cqT, ck, cvT, gates, ybd = _in_proj(
            x, norm_g[l][None, :], w_in[l].astype(BF16), bd, gq, gk, gcq, gck, tb['kca'],
            tb['invcnt'], pw.astype(BF16), pool_scale[l][None, :], conv_w[l])
        sink = jnp.repeat((c_sinks[l] * LOG2E).reshape(C_KV_HEADS, 2), SWA_BLOCK, axis=1)[:, None, :]
        x = _attention(qT, kaug, vT, tb['avg'], tb['crow_a'], tb['causal'],
                       cqT, ck, cvT, tb['band'], sink, gates,
                       x, ybd, w_out[l].astype(BF16))
    return x
```

```python
import math

import jax
import jax.numpy as jnp
import numpy as np
from jax import lax
from jax.experimental import pallas as pl
from jax.experimental.pallas import tpu as pltpu

F32 = jnp.float32
BF16 = jnp.bfloat16

D_MODEL = 1024
HEAD_DIM = 64
N_HEADS = 4
C_KV_HEADS = 2
NORM_EPS = 1e-6
MOBA_BLOCK = 256
MOBA_TOPK = 3
SWA_BLOCK = 128
POOL_WINDOWS = (2, 4, 8, 16)
CONV_WIDTH = 3
HALO = 16

AUG = 128
COL_ONEHOT = 64
COL_POS = 96
NEG_BIG = -(2.0 ** 100)
LOG2E = math.log2(math.e)
QK_SCALE = HEAD_DIM ** -0.5
SLOPES = np.exp2(-(8.0 / (2 * N_HEADS)) * np.arange(1, 2 * N_HEADS + 1)).astype(np.float32)
SLOPES_C, SLOPES_A = SLOPES[:N_HEADS], SLOPES[N_HEADS:]

IN_TILE = 1024
Q_TILE = 256
VMEM_LIMIT = 56 * 1024 * 1024

SEG = dict(aq=0, ak=256, av=512, ag=768, bu=1024, bg=1280, cq=1536, ck=1792, cv=1920,
           cg=2048, dh=2304, db=2560, dc=2816, dg=3072)
IN_PROJ_WIDTH = 3328


def _dot(a, b):
    return jnp.dot(a, b, preferred_element_type=F32)


def _silu(v):
    h = 0.5 * v
    return h + h * jnp.tanh(h)


def _fold8(x, op):
    out = x[0:8]
    for g in range(1, x.shape[0] // 8):
        out = op(out, x[g * 8:(g + 1) * 8])
    return out


def _in_proj_kernel(x_ref, g_ref, w_ref, bd_ref, gq_ref, gk_ref, gcq_ref, gck_ref, kca_ref,
                    invcnt_ref, pw_ref, pscale_ref, cw_ref,
                    qT_ref, kaug_ref, vT_ref, cqT_ref, ck_ref, cvT_ref, gates_ref, ybd_ref,
                    hist_ref):
    tm = x_ref.shape[1]
    s_idx = pl.program_id(1)
    x = x_ref[0]
    ms = jnp.mean(x * x, axis=-1, keepdims=True)
    h = (x * lax.rsqrt(ms + NORM_EPS) * g_ref[...]).astype(BF16)

    def proj(lo, width):
        return _dot(h, w_ref[:, lo:lo + width])

    def norm_t(p, gain_ref):
        pt = p.T
        outs = []
        for hh in range(N_HEADS):
            ph = pt[hh * HEAD_DIM:(hh + 1) * HEAD_DIM]
            ss = jnp.sum(ph * ph, axis=0, keepdims=True) * (1.0 / HEAD_DIM)
            outs.append(ph * lax.rsqrt(ss + NORM_EPS))
        return (jnp.concatenate(outs, axis=0) * gain_ref[...]).astype(BF16)

    def norm_rows(p, gain_ref, width):
        ss = _dot((p * p).astype(BF16), bd_ref[0:width, 0:width]) * (1.0 / HEAD_DIM)
        return p * lax.rsqrt(ss + NORM_EPS) * gain_ref[...]

    def build_kaug(kn, kconst):
        lane = lax.broadcasted_iota(jnp.int32, (tm, AUG), 1)
        blocks = []
        for hh in range(N_HEADS):
            col = kn[:, (hh // 2) * AUG:(hh // 2 + 1) * AUG]
            if hh % 2 == 1:
                col = pltpu.roll(col, HEAD_DIM, axis=1)
            blocks.append(jnp.where(lane < HEAD_DIM, col, kconst).astype(BF16))
        return blocks


    @pl.when(s_idx == 0)
    def _():
        hist_ref[...] = jnp.zeros(hist_ref.shape, F32)

    back = lambda v, k: pltpu.roll(v, k, axis=0)

    pd = proj(SEG['dh'], 1024)
    dh, db, dc, dg = (pd[:, k * 256:(k + 1) * 256] for k in range(4))
    u = dc * dh
    ue = jnp.concatenate([hist_ref[1], u], axis=0)
    hist_ref[1] = u[tm - HALO:]
    conv = (cw_ref[0:1, :] * back(ue, 2) + cw_ref[1:2, :] * back(ue, 1) + cw_ref[2:3, :] * ue)[HALO:]
    ybd_ref[0, :, 256:512] = (db * conv * _silu(dg)).astype(BF16)

    pb = proj(SEG['bu'], 512)
    bu, bg = pb[:, 0:256], pb[:, 256:512]
    e = jnp.concatenate([hist_ref[0], bu], axis=0)
    hist_ref[0] = bu[tm - HALO:]
    a2 = e + back(e, 1)
    a4 = a2 + back(a2, 2)
    a4r = a4[:, 128:256]
    a8 = a4r + back(a4r, 4)
    a16 = a8 + back(a8, 8)
    lane = lax.broadcasted_iota(jnp.int32, (HALO + tm, 128), 1)
    sums = jnp.concatenate([jnp.where(lane < 64, a2[:, 0:128], a4[:, 0:128]),
                            jnp.where(lane < 64, a8, a16)], axis=1)
    pooled = sums[HALO:] * invcnt_ref[0] - bu
    yb = _dot(pooled.astype(BF16), pw_ref[...]) * pscale_ref[...] * _silu(bg)
    ybd_ref[0, :, 0:256] = yb.astype(BF16)

    nblk = tm // Q_TILE
    qT = norm_t(proj(SEG['aq'], 256), gq_ref)
    for t in range(nblk):
        qT_ref[0, t] = qT[:, t * Q_TILE:(t + 1) * Q_TILE]
    kn = norm_rows(proj(SEG['ak'], 256), gk_ref, 256)
    kaug = build_kaug(kn, kca_ref[...].astype(F32))
    for hh in range(N_HEADS):
        kaug_ref[0, hh] = kaug[hh]
    vT = proj(SEG['av'], 256).T.astype(BF16)
    for t in range(nblk):
        vT_ref[0, t] = vT[:, t * Q_TILE:(t + 1) * Q_TILE]
    cqT = norm_t(proj(SEG['cq'], 256), gcq_ref)
    for t in range(nblk):
        cqT_ref[0, t] = cqT[:, t * Q_TILE:(t + 1) * Q_TILE]
    ckv = proj(SEG['ck'], 256)
    ck_ref[0] = norm_rows(ckv[:, 0:128], gck_ref, 128).astype(BF16)
    cvT = ckv[:, 128:256].T.astype(BF16)
    for t in range(tm // SWA_BLOCK):
        cvT_ref[0, t] = cvT[:, t * SWA_BLOCK:(t + 1) * SWA_BLOCK]
    gates_ref[0, :, 0:256] = _silu(proj(SEG['ag'], 256)).astype(BF16)
    gates_ref[0, :, 256:512] = _silu(proj(SEG['cg'], 256)).astype(BF16)


def _in_proj(x, g, w, bd, gq, gk, gcq, gck, kca, invcnt, pw, pscale, cw):
    B, S, _ = x.shape
    tm = IN_TILE
    ns = S // tm
    nq = S // Q_TILE
    nkb = S // SWA_BLOCK
    const = lambda *shape: pl.BlockSpec(shape, lambda b, s: (0,) * len(shape))
    out_shape = (
        jax.ShapeDtypeStruct((B, nq, 256, Q_TILE), BF16),
        jax.ShapeDtypeStruct((B, N_HEADS, S, AUG), BF16),
        jax.ShapeDtypeStruct((B, nq, 256, Q_TILE), BF16),
        jax.ShapeDtypeStruct((B, nq, 256, Q_TILE), BF16),
        jax.ShapeDtypeStruct((B, S, 128), BF16),
        jax.ShapeDtypeStruct((B, nkb, 128, SWA_BLOCK), BF16),
        jax.ShapeDtypeStruct((B, S, 512), BF16),
        jax.ShapeDtypeStruct((B, S, 512), BF16),
    )
    r = tm // Q_TILE
    out_specs = (
        pl.BlockSpec((1, r, 256, Q_TILE), lambda b, s: (b, s, 0, 0)),
        pl.BlockSpec((1, N_HEADS, tm, AUG), lambda b, s: (b, 0, s, 0)),
        pl.BlockSpec((1, r, 256, Q_TILE), lambda b, s: (b, s, 0, 0)),
        pl.BlockSpec((1, r, 256, Q_TILE), lambda b, s: (b, s, 0, 0)),
        pl.BlockSpec((1, tm, 128), lambda b, s: (b, s, 0)),
        pl.BlockSpec((1, tm // SWA_BLOCK, 128, SWA_BLOCK), lambda b, s: (b, s, 0, 0)),
        pl.BlockSpec((1, tm, 512), lambda b, s: (b, s, 0)),
        pl.BlockSpec((1, tm, 512), lambda b, s: (b, s, 0)),
    )
    in_specs = [
        pl.BlockSpec((1, tm, D_MODEL), lambda b, s: (b, s, 0)),
        const(1, D_MODEL),
        const(D_MODEL, IN_PROJ_WIDTH),
        const(256, 256),
        const(256, tm), const(1, 256), const(256, tm), const(1, 128),
        pl.BlockSpec((tm, AUG), lambda b, s: (s, 0)),
        pl.BlockSpec((1, tm, 256), lambda b, s: (jnp.minimum(s, 1), 0, 0)),
        const(256, 256), const(1, 256), const(CONV_WIDTH, 256),
    ]
    return pl.pallas_call(
        _in_proj_kernel, grid=(B, ns), in_specs=in_specs, out_specs=out_specs, out_shape=out_shape,
        scratch_shapes=[pltpu.VMEM((2, HALO, 256), F32)],
        compiler_params=pltpu.CompilerParams(dimension_semantics=("parallel", "arbitrary"),
                                             vmem_limit_bytes=VMEM_LIMIT),
        name="in_proj",
    )(x, g, w, bd, gq, gk, gcq, gck, kca, invcnt, pw, pscale, cw)


def _attn_kernel(qT_ref, kaug_ref, vT_ref, avg_ref, crow_ref, causal_ref, gate_ref,
                 cqT_ref, ck_ref, cvT_ref, band0_ref, band_ref, sink_ref, cgate_ref,
                 x_ref, ybd_ref, wo_ref,
                 out_ref,
                 kmean_ref, qaug_ref, acc_ref, s_ref, p_ref, mx_ref, cs_ref, cp_ref, cmx_ref):
    i = pl.program_id(1)
    swa_blocks = Q_TILE // SWA_BLOCK
    S = kaug_ref.shape[2]
    nb = S // MOBA_BLOCK
    chunk = 1024

    @pl.when(i == 0)
    def _():
        for hh in range(N_HEADS):
            acc = jnp.zeros((nb, AUG), F32)
            for c in range(S // chunk):
                acc = acc + _dot(avg_ref[:, c * chunk:(c + 1) * chunk],
                                 kaug_ref[0, hh, c * chunk:(c + 1) * chunk, :])
            kmean_ref[hh] = acc

    row = lax.broadcasted_iota(jnp.int32, (nb, Q_TILE), 0).astype(F32)
    i_f = i.astype(F32)
    neg_inf = jnp.float32(-jnp.inf)
    heads = range(N_HEADS)
    vh = lambda hh, j: vT_ref[0, j, hh * HEAD_DIM:(hh + 1) * HEAD_DIM, :]
    kb = lambda hh, j: kaug_ref[0, hh, pl.ds(pl.multiple_of(j * MOBA_BLOCK, MOBA_BLOCK), MOBA_BLOCK), :]

    for hh in heads:
        qh = qT_ref[0, 0, hh * HEAD_DIM:(hh + 1) * HEAD_DIM, :]
        km = kmean_ref[hh][:, 0:HEAD_DIM]
        km_hi = km.astype(BF16)
        km_lo = (km - km_hi.astype(F32)).astype(BF16)
        bs = _dot(km_hi, qh) + _dot(km_lo, qh)
        xs = jnp.where(row < i_f, bs, neg_inf)
        sel = row == i_f
        for _ in range(MOBA_TOPK):
            mx = jnp.max(xs, axis=0, keepdims=True)
            first = jnp.min(jnp.where(xs == mx, row, jnp.float32(nb)), axis=0, keepdims=True)
            pick = row == first
            sel = sel | (pick & (mx > neg_inf))
            xs = jnp.where(pick, neg_inf, xs)
        negmask = jnp.where(sel, 0.0, NEG_BIG).astype(BF16)
        pad = jnp.zeros((COL_POS - COL_ONEHOT - nb, Q_TILE), BF16) if nb < 32 else None
        parts = [qh, negmask] + ([pad] if pad is not None else []) + [crow_ref[hh]]
        qaug_ref[hh] = jnp.concatenate(parts, axis=0)

    _swa_front(i, swa_blocks, cqT_ref, ck_ref, band0_ref, band_ref, sink_ref, cs_ref, cp_ref, cmx_ref)

    def blk(pos):
        past = jnp.minimum(i + 1, nb - 1)
        return jnp.where(pos <= 0, i, jnp.where(pos <= i, pos - 1, past))

    kblk = MOBA_BLOCK
    sub = 8
    rows = 64

    def scores(hh, j0, j1, mask=None):
        s = _dot(jnp.concatenate([kb(hh, j0), kb(hh, j1)], axis=0), qaug_ref[hh])
        if mask is not None:
            s = s + mask
        s_ref[hh] = s
        mx_ref[hh] = _fold8(s, jnp.maximum)

    def scores_head(hh, t):
        scores(hh, blk(2 * t), blk(2 * t + 1))

    def scores_step(t):
        for hh in heads:
            scores_head(hh, t)

    def softmax_head(hh, m_old, l_old):
        m_new = jnp.maximum(m_old, jnp.max(mx_ref[hh], axis=0, keepdims=True))
        alpha = jnp.exp2(m_old - m_new)
        l8 = alpha * l_old
        for c in range(2 * kblk // rows):
            p = jnp.exp2(s_ref[hh, c * rows:(c + 1) * rows] - m_new)
            p_ref[hh, c * rows:(c + 1) * rows] = p.astype(BF16)
            l8 = l8 + _fold8(p, jnp.add)
        return m_new, l8, alpha

    def softmax_step(carry):
        ms, ls, _ = carry
        new = [softmax_head(hh, ms[hh], ls[hh]) for hh in heads]
        return tuple(zip(*new))

    def values_head(hh, t, alpha):
        acc_ref[hh] = (alpha * acc_ref[hh]
                       + _dot(vh(hh, blk(2 * t)), p_ref[hh, 0:kblk])
                       + _dot(vh(hh, blk(2 * t + 1)), p_ref[hh, kblk:2 * kblk]))

    def values_step(t, alphas):
        for hh in heads:
            values_head(hh, t, alphas[hh])

    def body(t, carry):
        ms, ls, alphas = carry
        new = []
        for hh in heads:
            values_head(hh, t - 1, alphas[hh])
            new.append(softmax_head(hh, ms[hh], ls[hh]))
            scores_head(hh, t + 1)
        return tuple(zip(*new))

    def drain(carry):
        values_step(steps - 2, carry[2])
        return softmax_step(carry)

    steps = (i + 2) // 2
    neg = jnp.full((1, Q_TILE), NEG_BIG, F32)
    zero8 = jnp.zeros((sub, Q_TILE), F32)
    acc_ref[...] = jnp.zeros(acc_ref.shape, F32)
    for hh in heads:
        scores(hh, i, blk(1), causal_ref[...])
    carry = softmax_step(((neg,) * N_HEADS, (zero8,) * N_HEADS, None))
    scores_step(1)
    def run(first, count):
        def go(c):
            for t in range(count):
                c = body(first + t, c)
            return c
        return go

    unroll = 3
    trips = jnp.maximum(steps - 2, 0)
    groups = trips // unroll
    carry = lax.fori_loop(0, groups, lambda k, c: run(unroll * k + 1, unroll)(c), carry)
    first_left = unroll * groups + 1
    carry = lax.cond(trips - unroll * groups == 2, run(first_left, 2), lambda c: c, carry)
    carry = lax.cond(trips - unroll * groups == 1, run(first_left, 1), lambda c: c, carry)
    carry = lax.cond(steps >= 2, drain, lambda c: c, carry)
    ms, ls, alphas = carry
    values_step(steps - 1, alphas)
    yc = _swa_back(i, swa_blocks, cvT_ref, cp_ref, cmx_ref, cgate_ref)[0]
    part = (x_ref[0] + _dot(ybd_ref[0, :, 0:256], wo_ref[256:512, :]) + _dot(yc, wo_ref[512:768, :])
            + _dot(ybd_ref[0, :, 256:512], wo_ref[768:1024, :]))
    outs = [acc_ref[hh] / jnp.sum(ls[hh], axis=0, keepdims=True) for hh in heads]
    o = jnp.concatenate(outs, axis=0).T
    ya = (o * gate_ref[0].astype(F32)).astype(BF16)
    out_ref[0] = part + _dot(ya, wo_ref[0:256, :])


def _attention(qT, kaug, vT, avg, crow, causal, cqT, ck, cvT, band, sink, gates, x, ybd, wo):
    B, nq = qT.shape[0], qT.shape[1]
    S = kaug.shape[2]
    nb = S // MOBA_BLOCK
    nkb = S // SWA_BLOCK
    nchain = (Q_TILE // SWA_BLOCK) * C_KV_HEADS
    win = 2 * SWA_BLOCK
    tile = lambda col: pl.BlockSpec((1, Q_TILE, 256), lambda b, i: (b, i, col))
    in_specs = [
        pl.BlockSpec((1, 1, 256, Q_TILE), lambda b, i: (b, i, 0, 0)),
        pl.BlockSpec((1, N_HEADS, S, AUG), lambda b, i: (b, 0, 0, 0)),
        pl.BlockSpec((1, nq, 256, Q_TILE), lambda b, i: (b, 0, 0, 0)),
        pl.BlockSpec((nb, S), lambda b, i: (0, 0)),
        pl.BlockSpec((N_HEADS, AUG - COL_POS, Q_TILE), lambda b, i: (0, 0, 0)),
        pl.BlockSpec((2 * MOBA_BLOCK, Q_TILE), lambda b, i: (0, 0)),
        tile(0),
        pl.BlockSpec((1, 1, 256, Q_TILE), lambda b, i: (b, i, 0, 0)),
        pl.BlockSpec((1, S, 128), lambda b, i: (b, 0, 0)),
        pl.BlockSpec((1, nkb, 128, SWA_BLOCK), lambda b, i: (b, 0, 0, 0)),
        pl.BlockSpec((C_KV_HEADS, 1, win, win), lambda b, i: (0, jnp.minimum(i, 1), 0, 0)),
        pl.BlockSpec((C_KV_HEADS, 1, win, win), lambda b, i: (0, 1, 0, 0)),
        pl.BlockSpec((C_KV_HEADS, 1, win), lambda b, i: (0, 0, 0)),
        tile(1),
        pl.BlockSpec((1, Q_TILE, D_MODEL), lambda b, i: (b, i, 0)),
        pl.BlockSpec((1, Q_TILE, 512), lambda b, i: (b, i, 0)),
        pl.BlockSpec((D_MODEL, D_MODEL), lambda b, i: (0, 0)),
    ]
    return pl.pallas_call(
        _attn_kernel, grid=(B, nq), in_specs=in_specs,
        out_specs=pl.BlockSpec((1, Q_TILE, D_MODEL), lambda b, i: (b, i, 0)),
        out_shape=jax.ShapeDtypeStruct((B, S, D_MODEL), F32),
        scratch_shapes=[pltpu.VMEM((N_HEADS, nb, AUG), F32),
                        pltpu.VMEM((N_HEADS, AUG, Q_TILE), BF16),
                        pltpu.VMEM((N_HEADS, HEAD_DIM, Q_TILE), F32),
                        pltpu.VMEM((N_HEADS, 2 * MOBA_BLOCK, Q_TILE), F32),
                        pltpu.VMEM((N_HEADS, 2 * MOBA_BLOCK, Q_TILE), BF16),
                        pltpu.VMEM((N_HEADS, 8, Q_TILE), F32),
                        pltpu.VMEM((nchain, win, win), F32),
                        pltpu.VMEM((nchain, win, win), BF16),
                        pltpu.VMEM((nchain, 8, win), F32)],
        compiler_params=pltpu.CompilerParams(dimension_semantics=("parallel", "arbitrary"),
                                             vmem_limit_bytes=VMEM_LIMIT),
        name="attention",
    )(qT, kaug, vT, avg, crow, causal, gates, cqT, ck, cvT, band, band, sink, gates, x, ybd, wo)


def _swa_front(i, blocks, cqT_ref, ck_ref, band0_ref, band_ref, sink_ref, s_ref, p_ref, mx_ref):
    win = 2 * SWA_BLOCK
    rows = 64
    zrows = jnp.zeros((HEAD_DIM, win), BF16)
    chains = [(t, kv) for t in range(blocks) for kv in range(C_KV_HEADS)]
    for t in range(blocks):
        n = i * blocks + t
        prev = jnp.maximum(n - 1, 0)
        kwin = jnp.concatenate([
            ck_ref[0, pl.ds(pl.multiple_of(prev * SWA_BLOCK, SWA_BLOCK), SWA_BLOCK), :],
            ck_ref[0, pl.ds(pl.multiple_of(n * SWA_BLOCK, SWA_BLOCK), SWA_BLOCK), :]], axis=0)
        for kv in range(C_KV_HEADS):
            c = chains.index((t, kv))
            q2 = jnp.concatenate([
                cqT_ref[0, t // 2, (2 * kv + g) * HEAD_DIM:(2 * kv + g + 1) * HEAD_DIM,
                        (t % 2) * SWA_BLOCK:(t % 2 + 1) * SWA_BLOCK] for g in range(2)], axis=1)
            qz = jnp.concatenate([q2, zrows] if kv == 0 else [zrows, q2], axis=0)
            band = band0_ref[kv, 0] if t == 0 else band_ref[kv, 0]
            s = _dot(kwin, qz) + band
            s_ref[c] = s
            mx_ref[c] = _fold8(s, jnp.maximum)
    for c, (t, kv) in enumerate(chains):
        m = jnp.max(mx_ref[c], axis=0, keepdims=True)
        l8 = jnp.zeros((8, win), F32)
        for r in range(win // rows):
            p = jnp.exp2(s_ref[c, r * rows:(r + 1) * rows] - m)
            p_ref[c, r * rows:(r + 1) * rows] = p.astype(BF16)
            l8 = l8 + _fold8(p, jnp.add)
        mx_ref[c, 0:1] = 1.0 / (jnp.sum(l8, axis=0, keepdims=True) + jnp.exp2(sink_ref[kv] - m))


def _swa_back(i, blocks, cvT_ref, p_ref, mx_ref, gate_ref):
    chains = [(t, kv) for t in range(blocks) for kv in range(C_KV_HEADS)]
    outs = {}
    for c, (t, kv) in enumerate(chains):
        n = i * blocks + t
        prev = jnp.maximum(n - 1, 0)
        vwin = jnp.concatenate([cvT_ref[0, prev, kv * HEAD_DIM:(kv + 1) * HEAD_DIM, :],
                                cvT_ref[0, n, kv * HEAD_DIM:(kv + 1) * HEAD_DIM, :]], axis=1)
        outs[t, kv] = _dot(vwin, p_ref[c]) * mx_ref[c, 0:1]
    tiles = []
    for u in range(blocks * SWA_BLOCK // Q_TILE):
        head_rows = []
        for kv in range(C_KV_HEADS):
            for g in range(2):
                head_rows.append(jnp.concatenate(
                    [outs[t, kv][:, g * SWA_BLOCK:(g + 1) * SWA_BLOCK] for t in (2 * u, 2 * u + 1)], axis=1))
        o = jnp.concatenate(head_rows, axis=0).T
        gate = gate_ref[0, u * Q_TILE:(u + 1) * Q_TILE, :].astype(F32)
        tiles.append((o * gate).astype(BF16))
    return tiles


def _tables(S):
    tm = IN_TILE
    slopes_c, slopes_a = SLOPES_C, SLOPES_A
    pos = np.arange(S)

    kca = np.zeros((S, AUG), np.float32)
    blk, r = pos // MOBA_BLOCK, pos % MOBA_BLOCK
    kca[pos, COL_ONEHOT + blk] = 1.0
    kca[:, COL_POS] = r
    kca[:, COL_POS + 1] = r
    kca[:, COL_POS + 2] = blk
    kca[:, COL_POS + 3] = blk
    kca = jnp.asarray(kca, BF16)

    c = slopes_a * np.float32(LOG2E)
    hi = c.astype(BF16).astype(np.float32)
    lo = c - hi
    rows = np.stack([hi, lo, hi * MOBA_BLOCK, lo * MOBA_BLOCK], axis=-1)
    rows = np.pad(rows, ((0, 0), (0, AUG - COL_POS - 4)))
    crow_a = jnp.asarray(np.broadcast_to(rows[:, :, None], rows.shape + (Q_TILE,)), BF16)

    nb = S // MOBA_BLOCK
    avg = jnp.asarray((pos[None, :] // MOBA_BLOCK == np.arange(nb)[:, None]) / MOBA_BLOCK, BF16)
    kq = np.arange(MOBA_BLOCK)
    causal = np.where(kq[:, None] <= kq[None, :], 0.0, NEG_BIG)
    causal = jnp.asarray(np.concatenate([causal, np.zeros_like(causal)], axis=0), F32)
    u = np.arange(2 * SWA_BLOCK)[:, None]
    t = np.arange(SWA_BLOCK)[None, :]
    ok = (u > t) & (u <= t + SWA_BLOCK)
    dist = (SWA_BLOCK + t - u).astype(np.float32)
    sc = np.asarray(slopes_c, np.float32) * np.float32(LOG2E)
    band = []
    for kv in range(C_KV_HEADS):
        bias = np.concatenate([-sc[2 * kv] * dist, -sc[2 * kv + 1] * dist], axis=1)
        ok2 = np.concatenate([ok, ok], axis=1)
        band.append(np.stack([np.where(ok2 & (u >= SWA_BLOCK), bias, NEG_BIG),
                              np.where(ok2, bias, NEG_BIG)]))
    band = jnp.asarray(np.stack(band), F32)
    w = np.repeat(np.asarray(POOL_WINDOWS, np.float32), 64)[None, :]
    first = 1.0 / np.minimum(np.arange(tm, dtype=np.float32)[:, None] + 1.0, w)
    invcnt = jnp.asarray(np.stack([first, np.broadcast_to(1.0 / w, (tm, 256))]), F32)
    return dict(kca=kca, crow_a=crow_a, avg=avg, causal=causal, band=band, invcnt=invcnt)


def kernel(x, norm_g, w_in, w_out, a_q_norm, a_k_norm, pool_w, pool_scale, c_q_norm, c_k_norm, c_sinks, conv_w):
    B, S, _ = x.shape
    depth = norm_g.shape[0]
    assert S % IN_TILE == 0 and S // MOBA_BLOCK <= 32
    tb = _tables(S)
    tm = IN_TILE
    d = np.arange(256)
    bd = jnp.asarray(d[:, None] // HEAD_DIM == d[None, :] // HEAD_DIM, BF16)
    qscale = QK_SCALE * LOG2E
    for l in range(depth):
        gq = jnp.broadcast_to(jnp.tile(a_q_norm[l] * qscale, N_HEADS)[:, None], (256, tm))
        gcq = jnp.broadcast_to(jnp.tile(c_q_norm[l] * qscale, N_HEADS)[:, None], (256, tm))
        gk = jnp.tile(a_k_norm[l], N_HEADS)[None, :]
        gck = jnp.tile(c_k_norm[l], C_KV_HEADS)[None, :]
        pw = jnp.zeros((256, 256), F32)
        for g in range(4):
            pw = pw.at[g * 64:(g + 1) * 64, g * 64:(g + 1) * 64].set(pool_w[l, g])
        qT, kaug, vT, cqT, ck, cvT, gates, ybd = _in_proj(
            x, norm_g[l][None, :], w_in[l].astype(BF16), bd, gq, gk, gcq, gck, tb['kca'],
            tb['invcnt'], pw.astype(BF16), pool_scale[l][None, :], conv_w[l])
        sink = jnp.repeat((c_sinks[l] * LOG2E).reshape(C_KV_HEADS, 2), SWA_BLOCK, axis=1)[:, None, :]
        x = _attention(qT, kaug, vT, tb['avg'], tb['crow_a'], tb['causal'],
                       cqT, ck, cvT, tb['band'], sink, gates,
                       x, ybd, w_out[l].astype(BF16))
    return x
```

```python
import math

import jax
import jax.numpy as jnp
import numpy as np
from jax import lax
from jax.experimental import pallas as pl
from jax.experimental.pallas import tpu as pltpu

F32 = jnp.float32
BF16 = jnp.bfloat16

D_MODEL = 1024
HEAD_DIM = 64
N_HEADS = 4
C_KV_HEADS = 2
NORM_EPS = 1e-6
MOBA_BLOCK = 256
MOBA_TOPK = 3
SWA_BLOCK = 128
POOL_WINDOWS = (2, 4, 8, 16)
CONV_WIDTH = 3
HALO = 16

AUG = 128
COL_ONEHOT = 64
COL_POS = 96
NEG_BIG = -(2.0 ** 100)
LOG2E = math.log2(math.e)
QK_SCALE = HEAD_DIM ** -0.5
SLOPES = np.exp2(-(8.0 / (2 * N_HEADS)) * np.arange(1, 2 * N_HEADS + 1)).astype(np.float32)
SLOPES_C, SLOPES_A = SLOPES[:N_HEADS], SLOPES[N_HEADS:]

IN_TILE = 1024
Q_TILE = 256
VMEM_LIMIT = 56 * 1024 * 1024

SEG = dict(aq=0, ak=256, av=512, ag=768, bu=1024, bg=1280, cq=1536, ck=1792, cv=1920,
           cg=2048, dh=2304, db=2560, dc=2816, dg=3072)
IN_PROJ_WIDTH = 3328


def _dot(a, b):
    return jnp.dot(a, b, preferred_element_type=F32)


def _silu(v):
    h = 0.5 * v
    return h + h * jnp.tanh(h)


def _fold8(x, op):
    out = x[0:8]
    for g in range(1, x.shape[0] // 8):
        out = op(out, x[g * 8:(g + 1) * 8])
    return out


def _in_proj_kernel(x_ref, g_ref, w_ref, bd_ref, gq_ref, gk_ref, gcq_ref, gck_ref, kca_ref,
                    invcnt_ref, pw_ref, pscale_ref, cw_ref,
                    qT_ref, kaug_ref, vT_ref, cqT_ref, ck_ref, cvT_ref, gates_ref, ybd_ref,
                    hist_ref):
    tm = x_ref.shape[1]
    s_idx = pl.program_id(1)
    x = x_ref[0]
    ms = jnp.mean(x * x, axis=-1, keepdims=True)
    h = (x * lax.rsqrt(ms + NORM_EPS) * g_ref[...]).astype(BF16)

    def proj(lo, width):
        return _dot(h, w_ref[:, lo:lo + width])

    def norm_t(p, gain_ref):
        pt = p.T
        outs = []
        for hh in range(N_HEADS):
            ph = pt[hh * HEAD_DIM:(hh + 1) * HEAD_DIM]
            ss = jnp.sum(ph * ph, axis=0, keepdims=True) * (1.0 / HEAD_DIM)
            outs.append(ph * lax.rsqrt(ss + NORM_EPS))
        return (jnp.concatenate(outs, axis=0) * gain_ref[...]).astype(BF16)

    def norm_rows(p, gain_ref, width):
        ss = _dot((p * p).astype(BF16), bd_ref[0:width, 0:width]) * (1.0 / HEAD_DIM)
        return p * lax.rsqrt(ss + NORM_EPS) * gain_ref[...]

    def build_kaug(kn, kconst):
        lane = lax.broadcasted_iota(jnp.int32, (tm, AUG), 1)
        blocks = []
        for hh in range(N_HEADS):
            col = kn[:, (hh // 2) * AUG:(hh // 2 + 1) * AUG]
            if hh % 2 == 1:
                col = pltpu.roll(col, HEAD_DIM, axis=1)
            blocks.append(jnp.where(lane < HEAD_DIM, col, kconst).astype(BF16))
        return blocks


    @pl.when(s_idx == 0)
    def _():
        hist_ref[...] = jnp.zeros(hist_ref.shape, F32)

    back = lambda v, k: pltpu.roll(v, k, axis=0)

    pd = proj(SEG['dh'], 1024)
    dh, db, dc, dg = (pd[:, k * 256:(k + 1) * 256] for k in range(4))
    u = dc * dh
    ue = jnp.concatenate([hist_ref[1], u], axis=0)
    hist_ref[1] = u[tm - HALO:]
    conv = (cw_ref[0:1, :] * back(ue, 2) + cw_ref[1:2, :] * back(ue, 1) + cw_ref[2:3, :] * ue)[HALO:]
    ybd_ref[0, :, 256:512] = (db * conv * _silu(dg)).astype(BF16)

    pb = proj(SEG['bu'], 512)
    bu, bg = pb[:, 0:256], pb[:, 256:512]
    e = jnp.concatenate([hist_ref[0], bu], axis=0)
    hist_ref[0] = bu[tm - HALO:]
    a2 = e + back(e, 1)
    a4 = a2 + back(a2, 2)
    a4r = a4[:, 128:256]
    a8 = a4r + back(a4r, 4)
    a16 = a8 + back(a8, 8)
    lane = lax.broadcasted_iota(jnp.int32, (HALO + tm, 128), 1)
    sums = jnp.concatenate([jnp.where(lane < 64, a2[:, 0:128], a4[:, 0:128]),
                            jnp.where(lane < 64, a8, a16)], axis=1)
    pooled = sums[HALO:] * invcnt_ref[0] - bu
    yb = _dot(pooled.astype(BF16), pw_ref[...]) * pscale_ref[...] * _silu(bg)
    ybd_ref[0, :, 0:256] = yb.astype(BF16)

    nblk = tm // Q_TILE
    qT = norm_t(proj(SEG['aq'], 256), gq_ref)
    for t in range(nblk):
        qT_ref[0, t] = qT[:, t * Q_TILE:(t + 1) * Q_TILE]
    kn = norm_rows(proj(SEG['ak'], 256), gk_ref, 256)
    kaug = build_kaug(kn, kca_ref[...].astype(F32))
    for hh in range(N_HEADS):
        kaug_ref[0, hh] = kaug[hh]
    vT = proj(SEG['av'], 256).T.astype(BF16)
    for t in range(nblk):
        vT_ref[0, t] = vT[:, t * Q_TILE:(t + 1) * Q_TILE]
    cqT = norm_t(proj(SEG['cq'], 256), gcq_ref)
    for t in range(nblk):
        cqT_ref[0, t] = cqT[:, t * Q_TILE:(t + 1) * Q_TILE]
    ckv = proj(SEG['ck'], 256)
    ck_ref[0] = norm_rows(ckv[:, 0:128], gck_ref, 128).astype(BF16)
    cvT = ckv[:, 128:256].T.astype(BF16)
    for t in range(tm // SWA_BLOCK):
        cvT_ref[0, t] = cvT[:, t * SWA_BLOCK:(t + 1) * SWA_BLOCK]
    gates_ref[0, :, 0:256] = _silu(proj(SEG['ag'], 256)).astype(BF16)
    gates_ref[0, :, 256:512] = _silu(proj(SEG['cg'], 256)).astype(BF16)


def _in_proj(x, g, w, bd, gq, gk, gcq, gck, kca, invcnt, pw, pscale, cw):
    B, S, _ = x.shape
    tm = IN_TILE
    ns = S // tm
    nq = S // Q_TILE
    nkb = S // SWA_BLOCK
    const = lambda *shape: pl.BlockSpec(shape, lambda b, s: (0,) * len(shape))
    out_shape = (
        jax.ShapeDtypeStruct((B, nq, 256, Q_TILE), BF16),
        jax.ShapeDtypeStruct((B, N_HEADS, S, AUG), BF16),
        jax.ShapeDtypeStruct((B, nq, 256, Q_TILE), BF16),
        jax.ShapeDtypeStruct((B, nq, 256, Q_TILE), BF16),
        jax.ShapeDtypeStruct((B, S, 128), BF16),
        jax.ShapeDtypeStruct((B, nkb, 128, SWA_BLOCK), BF16),
        jax.ShapeDtypeStruct((B, S, 512), BF16),
        jax.ShapeDtypeStruct((B, S, 512), BF16),
    )
    r = tm // Q_TILE
    out_specs = (
        pl.BlockSpec((1, r, 256, Q_TILE), lambda b, s: (b, s, 0, 0)),
        pl.BlockSpec((1, N_HEADS, tm, AUG), lambda b, s: (b, 0, s, 0)),
        pl.BlockSpec((1, r, 256, Q_TILE), lambda b, s: (b, s, 0, 0)),
        pl.BlockSpec((1, r, 256, Q_TILE), lambda b, s: (b, s, 0, 0)),
        pl.BlockSpec((1, tm, 128), lambda b, s: (b, s, 0)),
        pl.BlockSpec((1, tm // SWA_BLOCK, 128, SWA_BLOCK), lambda b, s: (b, s, 0, 0)),
        pl.BlockSpec((1, tm, 512), lambda b, s: (b, s, 0)),
        pl.BlockSpec((1, tm, 512), lambda b, s: (b, s, 0)),
    )
    in_specs = [
        pl.BlockSpec((1, tm, D_MODEL), lambda b, s: (b, s, 0)),
        const(1, D_MODEL),
        const(D_MODEL, IN_PROJ_WIDTH),
        const(256, 256),
        const(256, tm), const(1, 256), const(256, tm), const(1, 128),
        pl.BlockSpec((tm, AUG), lambda b, s: (s, 0)),
        pl.BlockSpec((1, tm, 256), lambda b, s: (jnp.minimum(s, 1), 0, 0)),
        const(256, 256), const(1, 256), const(CONV_WIDTH, 256),
    ]
    return pl.pallas_call(
        _in_proj_kernel, grid=(B, ns), in_specs=in_specs, out_specs=out_specs, out_shape=out_shape,
        scratch_shapes=[pltpu.VMEM((2, HALO, 256), F32)],
        compiler_params=pltpu.CompilerParams(dimension_semantics=("parallel", "arbitrary"),
                                             vmem_limit_bytes=VMEM_LIMIT),
        name="in_proj",
    )(x, g, w, bd, gq, gk, gcq, gck, kca, invcnt, pw, pscale, cw)


def _attn_kernel(qT_ref, kaug_ref, vT_ref, avg_ref, crow_ref, causal_ref, gate_ref,
                 cqT_ref, ck_ref, cvT_ref, band0_ref, band_ref, sink_ref, cgate_ref,
                 x_ref, ybd_ref, wo_ref,
                 out_ref,
                 kmean_ref, qaug_ref, acc_ref, s_ref, p_ref, mx_ref, cs_ref, cp_ref, cmx_ref):
    i = pl.program_id(1)
    swa_blocks = Q_TILE // SWA_BLOCK
    S = kaug_ref.shape[2]
    nb = S // MOBA_BLOCK
    chunk = 1024

    @pl.when(i == 0)
    def _():
        for hh in range(N_HEADS):
            acc = jnp.zeros((nb, AUG), F32)
            for c in range(S // chunk):
                acc = acc + _dot(avg_ref[:, c * chunk:(c + 1) * chunk],
                                 kaug_ref[0, hh, c * chunk:(c + 1) * chunk, :])
            kmean_ref[hh] = acc

    row = lax.broadcasted_iota(jnp.int32, (nb, Q_TILE), 0).astype(F32)
    i_f = i.astype(F32)
    neg_inf = jnp.float32(-jnp.inf)
    heads = range(N_HEADS)
    vh = lambda hh, j: vT_ref[0, j, hh * HEAD_DIM:(hh + 1) * HEAD_DIM, :]
    kb = lambda hh, j: kaug_ref[0, hh, pl.ds(pl.multiple_of(j * MOBA_BLOCK, MOBA_BLOCK), MOBA_BLOCK), :]

    for hh in heads:
        qh = qT_ref[0, 0, hh * HEAD_DIM:(hh + 1) * HEAD_DIM, :]
        km = kmean_ref[hh][:, 0:HEAD_DIM]
        km_hi = km.astype(BF16)
        km_lo = (km - km_hi.astype(F32)).astype(BF16)
        bs = _dot(km_hi, qh) + _dot(km_lo, qh)
        xs = jnp.where(row < i_f, bs, neg_inf)
        sel = row == i_f
        for _ in range(MOBA_TOPK):
            mx = jnp.max(xs, axis=0, keepdims=True)
            first = jnp.min(jnp.where(xs == mx, row, jnp.float32(nb)), axis=0, keepdims=True)
            pick = row == first
            sel = sel | (pick & (mx > neg_inf))
            xs = jnp.where(pick, neg_inf, xs)
        negmask = jnp.where(sel, 0.0, NEG_BIG).astype(BF16)
        pad = jnp.zeros((COL_POS - COL_ONEHOT - nb, Q_TILE), BF16) if nb < 32 else None
        parts = [qh, negmask] + ([pad] if pad is not None else []) + [crow_ref[hh]]
        qaug_ref[hh] = jnp.concatenate(parts, axis=0)

    _swa_front(i, swa_blocks, cqT_ref, ck_ref, band0_ref, band_ref, sink_ref, cs_ref, cp_ref, cmx_ref)

    def blk(pos):
        past = jnp.minimum(i + 1, nb - 1)
        return jnp.where(pos <= 0, i, jnp.where(pos <= i, pos - 1, past))

    kblk = MOBA_BLOCK
    sub = 8
    rows = 64

    def scores(hh, j0, j1, mask=None):
        s = _dot(jnp.concatenate([kb(hh, j0), kb(hh, j1)], axis=0), qaug_ref[hh])
        if mask is not None:
            s = s + mask
        s_ref[hh] = s
        mx_ref[hh] = _fold8(s, jnp.maximum)

    def scores_head(hh, t):
        scores(hh, blk(2 * t), blk(2 * t + 1))

    def scores_step(t):
        for hh in heads:
            scores_head(hh, t)

    def softmax_head(hh, m_old, l_old):
        m_new = jnp.maximum(m_old, jnp.max(mx_ref[hh], axis=0, keepdims=True))
        alpha = jnp.exp2(m_old - m_new)
        l8 = alpha * l_old
        for c in range(2 * kblk // rows):
            p = jnp.exp2(s_ref[hh, c * rows:(c + 1) * rows] - m_new)
            p_ref[hh, c * rows:(c + 1) * rows] = p.astype(BF16)
            l8 = l8 + _fold8(p, jnp.add)
        return m_new, l8, alpha

    def softmax_step(carry):
        ms, ls, _ = carry
        new = [softmax_head(hh, ms[hh], ls[hh]) for hh in heads]
        return tuple(zip(*new))

    def values_head(hh, t, alpha):
        acc_ref[hh] = (alpha * acc_ref[hh]
                       + _dot(vh(hh, blk(2 * t)), p_ref[hh, 0:kblk])
                       + _dot(vh(hh, blk(2 * t + 1)), p_ref[hh, kblk:2 * kblk]))

    def values_step(t, alphas):
        for hh in heads:
            values_head(hh, t, alphas[hh])

    def body(t, carry):
        ms, ls, alphas = carry
        new = []
        for hh in heads:
            values_head(hh, t - 1, alphas[hh])
            new.append(softmax_head(hh, ms[hh], ls[hh]))
            scores_head(hh, t + 1)
        return tuple(zip(*new))

    def drain(carry):
        values_step(steps - 2, carry[2])
        return softmax_step(carry)

    steps = jnp.maximum((i + 2) // 2, 2)
    neg = jnp.full((1, Q_TILE), NEG_BIG, F32)
    zero8 = jnp.zeros((sub, Q_TILE), F32)
    acc_ref[...] = jnp.zeros(acc_ref.shape, F32)
    for hh in heads:
        scores(hh, i, blk(1), causal_ref[...])
    carry = softmax_step(((neg,) * N_HEADS, (zero8,) * N_HEADS, None))
    scores_step(1)
    trips = jnp.maximum(steps - 2, 0)
    carry = lax.fori_loop(0, trips // 2, lambda k, c: body(2 * k + 2, body(2 * k + 1, c)), carry)
    carry = lax.cond(trips % 2 == 1, lambda c: body(steps - 2, c), lambda c: c, carry)
    carry = drain(carry)
    ms, ls, alphas = carry
    values_step(steps - 1, alphas)
    yc = _swa_back(i, swa_blocks, cvT_ref, cp_ref, cmx_ref, cgate_ref)[0]
    part = (x_ref[0] + _dot(ybd_ref[0, :, 0:256], wo_ref[256:512, :]) + _dot(yc, wo_ref[512:768, :])
            + _dot(ybd_ref[0, :, 256:512], wo_ref[768:1024, :]))
    outs = [acc_ref[hh] / jnp.sum(ls[hh], axis=0, keepdims=True) for hh in heads]
    o = jnp.concatenate(outs, axis=0).T
    ya = (o * gate_ref[0].astype(F32)).astype(BF16)
    out_ref[0] = part + _dot(ya, wo_ref[0:256, :])


def _attention(qT, kaug, vT, avg, crow, causal, cqT, ck, cvT, band, sink, gates, x, ybd, wo):
    B, nq = qT.shape[0], qT.shape[1]
    S = kaug.shape[2]
    nb = S // MOBA_BLOCK
    nkb = S // SWA_BLOCK
    nchain = (Q_TILE // SWA_BLOCK) * C_KV_HEADS
    win = 2 * SWA_BLOCK
    tile = lambda col: pl.BlockSpec((1, Q_TILE, 256), lambda b, i: (b, i, col))
    in_specs = [
        pl.BlockSpec((1, 1, 256, Q_TILE), lambda b, i: (b, i, 0, 0)),
        pl.BlockSpec((1, N_HEADS, S, AUG), lambda b, i: (b, 0, 0, 0)),
        pl.BlockSpec((1, nq, 256, Q_TILE), lambda b, i: (b, 0, 0, 0)),
        pl.BlockSpec((nb, S), lambda b, i: (0, 0)),
        pl.BlockSpec((N_HEADS, AUG - COL_POS, Q_TILE), lambda b, i: (0, 0, 0)),
        pl.BlockSpec((2 * MOBA_BLOCK, Q_TILE), lambda b, i: (0, 0)),
        tile(0),
        pl.BlockSpec((1, 1, 256, Q_TILE), lambda b, i: (b, i, 0, 0)),
        pl.BlockSpec((1, S, 128), lambda b, i: (b, 0, 0)),
        pl.BlockSpec((1, nkb, 128, SWA_BLOCK), lambda b, i: (b, 0, 0, 0)),
        pl.BlockSpec((C_KV_HEADS, 1, win, win), lambda b, i: (0, jnp.minimum(i, 1), 0, 0)),
        pl.BlockSpec((C_KV_HEADS, 1, win, win), lambda b, i: (0, 1, 0, 0)),
        pl.BlockSpec((C_KV_HEADS, 1, win), lambda b, i: (0, 0, 0)),
        tile(1),
        pl.BlockSpec((1, Q_TILE, D_MODEL), lambda b, i: (b, i, 0)),
        pl.BlockSpec((1, Q_TILE, 512), lambda b, i: (b, i, 0)),
        pl.BlockSpec((D_MODEL, D_MODEL), lambda b, i: (0, 0)),
    ]
    return pl.pallas_call(
        _attn_kernel, grid=(B, nq), in_specs=in_specs,
        out_specs=pl.BlockSpec((1, Q_TILE, D_MODEL), lambda b, i: (b, i, 0)),
        out_shape=jax.ShapeDtypeStruct((B, S, D_MODEL), F32),
        scratch_shapes=[pltpu.VMEM((N_HEADS, nb, AUG), F32),
                        pltpu.VMEM((N_HEADS, AUG, Q_TILE), BF16),
                        pltpu.VMEM((N_HEADS, HEAD_DIM, Q_TILE), F32),
                        pltpu.VMEM((N_HEADS, 2 * MOBA_BLOCK, Q_TILE), F32),
                        pltpu.VMEM((N_HEADS, 2 * MOBA_BLOCK, Q_TILE), BF16),
                        pltpu.VMEM((N_HEADS, 8, Q_TILE), F32),
                        pltpu.VMEM((nchain, win, win), F32),
                        pltpu.VMEM((nchain, win, win), BF16),
                        pltpu.VMEM((nchain, 8, win), F32)],
        compiler_params=pltpu.CompilerParams(dimension_semantics=("parallel", "arbitrary"),
                                             vmem_limit_bytes=VMEM_LIMIT),
        name="attention",
    )(qT, kaug, vT, avg, crow, causal, gates, cqT, ck, cvT, band, band, sink, gates, x, ybd, wo)


def _swa_front(i, blocks, cqT_ref, ck_ref, band0_ref, band_ref, sink_ref, s_ref, p_ref, mx_ref):
    win = 2 * SWA_BLOCK
    rows = 64
    zrows = jnp.zeros((HEAD_DIM, win), BF16)
    chains = [(t, kv) for t in range(blocks) for kv in range(C_KV_HEADS)]
    for t in range(blocks):
        n = i * blocks + t
        prev = jnp.maximum(n - 1, 0)
        kwin = jnp.concatenate([
            ck_ref[0, pl.ds(pl.multiple_of(prev * SWA_BLOCK, SWA_BLOCK), SWA_BLOCK), :],
            ck_ref[0, pl.ds(pl.multiple_of(n * SWA_BLOCK, SWA_BLOCK), SWA_BLOCK), :]], axis=0)
        for kv in range(C_KV_HEADS):
            c = chains.index((t, kv))
            q2 = jnp.concatenate([
                cqT_ref[0, t // 2, (2 * kv + g) * HEAD_DIM:(2 * kv + g + 1) * HEAD_DIM,
                        (t % 2) * SWA_BLOCK:(t % 2 + 1) * SWA_BLOCK] for g in range(2)], axis=1)
            qz = jnp.concatenate([q2, zrows] if kv == 0 else [zrows, q2], axis=0)
            band = band0_ref[kv, 0] if t == 0 else band_ref[kv, 0]
            s = _dot(kwin, qz) + band
            s_ref[c] = s
            mx_ref[c] = _fold8(s, jnp.maximum)
    for c, (t, kv) in enumerate(chains):
        m = jnp.max(mx_ref[c], axis=0, keepdims=True)
        l8 = jnp.zeros((8, win), F32)
        for r in range(win // rows):
            p = jnp.exp2(s_ref[c, r * rows:(r + 1) * rows] - m)
            p_ref[c, r * rows:(r + 1) * rows] = p.astype(BF16)
            l8 = l8 + _fold8(p, jnp.add)
        mx_ref[c, 0:1] = 1.0 / (jnp.sum(l8, axis=0, keepdims=True) + jnp.exp2(sink_ref[kv] - m))


def _swa_back(i, blocks, cvT_ref, p_ref, mx_ref, gate_ref):
    chains = [(t, kv) for t in range(blocks) for kv in range(C_KV_HEADS)]
    outs = {}
    for c, (t, kv) in enumerate(chains):
        n = i * blocks + t
        prev = jnp.maximum(n - 1, 0)
        vwin = jnp.concatenate([cvT_ref[0, prev, kv * HEAD_DIM:(kv + 1) * HEAD_DIM, :],
                                cvT_ref[0, n, kv * HEAD_DIM:(kv + 1) * HEAD_DIM, :]], axis=1)
        outs[t, kv] = _dot(vwin, p_ref[c]) * mx_ref[c, 0:1]
    tiles = []
    for u in range(blocks * SWA_BLOCK // Q_TILE):
        head_rows = []
        for kv in range(C_KV_HEADS):
            for g in range(2):
                head_rows.append(jnp.concatenate(
                    [outs[t, kv][:, g * SWA_BLOCK:(g + 1) * SWA_BLOCK] for t in (2 * u, 2 * u + 1)], axis=1))
        o = jnp.concatenate(head_rows, axis=0).T
        gate = gate_ref[0, u * Q_TILE:(u + 1) * Q_TILE, :].astype(F32)
        tiles.append((o * gate).astype(BF16))
    return tiles


def _tables(S):
    tm = IN_TILE
    slopes_c, slopes_a = SLOPES_C, SLOPES_A
    pos = np.arange(S)

    kca = np.zeros((S, AUG), np.float32)
    blk, r = pos // MOBA_BLOCK, pos % MOBA_BLOCK
    kca[pos, COL_ONEHOT + blk] = 1.0
    kca[:, COL_POS] = r
    kca[:, COL_POS + 1] = r
    kca[:, COL_POS + 2] = blk
    kca[:, COL_POS + 3] = blk
    kca = jnp.asarray(kca, BF16)

    c = slopes_a * np.float32(LOG2E)
    hi = c.astype(BF16).astype(np.float32)
    lo = c - hi
    rows = np.stack([hi, lo, hi * MOBA_BLOCK, lo * MOBA_BLOCK], axis=-1)
    rows = np.pad(rows, ((0, 0), (0, AUG - COL_POS - 4)))
    crow_a = jnp.asarray(np.broadcast_to(rows[:, :, None], rows.shape + (Q_TILE,)), BF16)

    nb = S // MOBA_BLOCK
    avg = jnp.asarray((pos[None, :] // MOBA_BLOCK == np.arange(nb)[:, None]) / MOBA_BLOCK, BF16)
    kq = np.arange(MOBA_BLOCK)
    causal = np.where(kq[:, None] <= kq[None, :], 0.0, NEG_BIG)
    causal = jnp.asarray(np.concatenate([causal, np.zeros_like(causal)], axis=0), F32)
    u = np.arange(2 * SWA_BLOCK)[:, None]
    t = np.arange(SWA_BLOCK)[None, :]
    ok = (u > t) & (u <= t + SWA_BLOCK)
    dist = (SWA_BLOCK + t - u).astype(np.float32)
    sc = np.asarray(slopes_c, np.float32) * np.float32(LOG2E)
    band = []
    for kv in range(C_KV_HEADS):
        bias = np.concatenate([-sc[2 * kv] * dist, -sc[2 * kv + 1] * dist], axis=1)
        ok2 = np.concatenate([ok, ok], axis=1)
        band.append(np.stack([np.where(ok2 & (u >= SWA_BLOCK), bias, NEG_BIG),
                              np.where(ok2, bias, NEG_BIG)]))
    band = jnp.asarray(np.stack(band), F32)
    w = np.repeat(np.asarray(POOL_WINDOWS, np.float32), 64)[None, :]
    first = 1.0 / np.minimum(np.arange(tm, dtype=np.float32)[:, None] + 1.0, w)
    invcnt = jnp.asarray(np.stack([first, np.broadcast_to(1.0 / w, (tm, 256))]), F32)
    return dict(kca=kca, crow_a=crow_a, avg=avg, causal=causal, band=band, invcnt=invcnt)


def kernel(x, norm_g, w_in, w_out, a_q_norm, a_k_norm, pool_w, pool_scale, c_q_norm, c_k_norm, c_sinks, conv_w):
    B, S, _ = x.shape
    depth = norm_g.shape[0]
    assert S % IN_TILE == 0 and S // MOBA_BLOCK <= 32
    tb = _tables(S)
    tm = IN_TILE
    d = np.arange(256)
    bd = jnp.asarray(d[:, None] // HEAD_DIM == d[None, :] // HEAD_DIM, BF16)
    qscale = QK_SCALE * LOG2E
    for l in range(depth):
        gq = jnp.broadcast_to(jnp.tile(a_q_norm[l] * qscale, N_HEADS)[:, None], (256, tm))
        gcq = jnp.broadcast_to(jnp.tile(c_q_norm[l] * qscale, N_HEADS)[:, None], (256, tm))
        gk = jnp.tile(a_k_norm[l], N_HEADS)[None, :]
        gck = jnp.tile(c_k_norm[l], C_KV_HEADS)[None, :]
        pw = jnp.zeros((256, 256), F32)
        for g in range(4):
            pw = pw.at[g * 64:(g + 1) * 64, g * 64:(g + 1) * 64].set(pool_w[l, g])
        qT, kaug, vT, cqT, ck, cvT, gates, ybd = _in_proj(
            x, norm_g[l][None, :], w_in[l].astype(BF16), bd, gq, gk, gcq, gck, tb['kca'],
            tb['invcnt'], pw.astype(BF16), pool_scale[l][None, :], conv_w[l])
        sink = jnp.repeat((c_sinks[l] * LOG2E).reshape(C_KV_HEADS, 2), SWA_BLOCK, axis=1)[:, None, :]
        x = _attention(qT, kaug, vT, tb['avg'], tb['crow_a'], tb['causal'],
                       cqT, ck, cvT, tb['band'], sink, gates,
                       x, ybd, w_out[l].astype(BF16))
    return x
```

```python
import math

import jax
import jax.numpy as jnp
import numpy as np
from jax import lax
from jax.experimental import pallas as pl
from jax.experimental.pallas import tpu as pltpu

F32 = jnp.float32
BF16 = jnp.bfloat16

D_MODEL = 1024
HEAD_DIM = 64
N_HEADS = 4
C_KV_HEADS = 2
NORM_EPS = 1e-6
MOBA_BLOCK = 256
MOBA_TOPK = 3
SWA_BLOCK = 128
POOL_WINDOWS = (2, 4, 8, 16)
CONV_WIDTH = 3
HALO = 16

AUG = 128
COL_ONEHOT = 64
COL_POS = 96
NEG_BIG = -(2.0 ** 100)
LOG2E = math.log2(math.e)
QK_SCALE = HEAD_DIM ** -0.5
SLOPES = np.exp2(-(8.0 / (2 * N_HEADS)) * np.arange(1, 2 * N_HEADS + 1)).astype(np.float32)
SLOPES_C, SLOPES_A = SLOPES[:N_HEADS], SLOPES[N_HEADS:]

IN_TILE = 1024
Q_TILE = 256
VMEM_LIMIT = 56 * 1024 * 1024

SEG = dict(aq=0, ak=256, av=512, ag=768, bu=1024, bg=1280, cq=1536, ck=1792, cv=1920,
           cg=2048, dh=2304, db=2560, dc=2816, dg=3072)
IN_PROJ_WIDTH = 3328


def _dot(a, b):
    return jnp.dot(a, b, preferred_element_type=F32)


def _silu(v):
    h = 0.5 * v
    return h + h * jnp.tanh(h)


def _fold8(x, op):
    out = x[0:8]
    for g in range(1, x.shape[0] // 8):
        out = op(out, x[g * 8:(g + 1) * 8])
    return out


def _in_proj_kernel(x_ref, g_ref, w_ref, bd_ref, gq_ref, gk_ref, gcq_ref, gck_ref, kca_ref,
                    invcnt_ref, pw_ref, pscale_ref, cw_ref,
                    qT_ref, kaug_ref, vT_ref, cqT_ref, ck_ref, cvT_ref, gates_ref, ybd_ref,
                    hist_ref):
    tm = x_ref.shape[1]
    s_idx = pl.program_id(1)
    chunk = 256
    hs, pb_rows = [], []
    for r in range(tm // chunk):
        xr = x_ref[0, r * chunk:(r + 1) * chunk, :]
        ms = jnp.mean(xr * xr, axis=-1, keepdims=True)
        hs.append((xr * lax.rsqrt(ms + NORM_EPS) * g_ref[...]).astype(BF16))
        pb_rows.append(_dot(hs[-1], w_ref[:, SEG['bu']:SEG['bu'] + 512]))
    h = jnp.concatenate(hs, axis=0)

    def proj(lo, width):
        return _dot(h, w_ref[:, lo:lo + width])

    def norm_t(p, gain_ref):
        pt = p.T
        outs = []
        for hh in range(N_HEADS):
            ph = pt[hh * HEAD_DIM:(hh + 1) * HEAD_DIM]
            ss = jnp.sum(ph * ph, axis=0, keepdims=True) * (1.0 / HEAD_DIM)
            outs.append(ph * lax.rsqrt(ss + NORM_EPS))
        return (jnp.concatenate(outs, axis=0) * gain_ref[...]).astype(BF16)

    def norm_rows(p, gain_ref, width):
        ss = _dot((p * p).astype(BF16), bd_ref[0:width, 0:width]) * (1.0 / HEAD_DIM)
        return p * lax.rsqrt(ss + NORM_EPS) * gain_ref[...]

    def build_kaug(kn, kconst):
        lane = lax.broadcasted_iota(jnp.int32, (tm, AUG), 1)
        blocks = []
        for hh in range(N_HEADS):
            col = kn[:, (hh // 2) * AUG:(hh // 2 + 1) * AUG]
            if hh % 2 == 1:
                col = pltpu.roll(col, HEAD_DIM, axis=1)
            blocks.append(jnp.where(lane < HEAD_DIM, col, kconst).astype(BF16))
        return blocks


    @pl.when(s_idx == 0)
    def _():
        hist_ref[...] = jnp.zeros(hist_ref.shape, F32)

    back = lambda v, k: pltpu.roll(v, k, axis=0)

    pb = jnp.concatenate(pb_rows, axis=0)
    bu, bg = pb[:, 0:256], pb[:, 256:512]
    e = jnp.concatenate([hist_ref[0], bu], axis=0)
    hist_ref[0] = bu[tm - HALO:]
    a2 = e + back(e, 1)
    a4 = a2 + back(a2, 2)
    a4r = a4[:, 128:256]
    a8 = a4r + back(a4r, 4)
    a16 = a8 + back(a8, 8)
    lane = lax.broadcasted_iota(jnp.int32, (HALO + tm, 128), 1)
    sums = jnp.concatenate([jnp.where(lane < 64, a2[:, 0:128], a4[:, 0:128]),
                            jnp.where(lane < 64, a8, a16)], axis=1)
    pooled = sums[HALO:] * invcnt_ref[0] - bu
    yb = _dot(pooled.astype(BF16), pw_ref[...]) * pscale_ref[...] * _silu(bg)
    ybd_ref[0, :, 0:256] = yb.astype(BF16)

    pd = proj(SEG['dh'], 1024)
    dh, db, dc, dg = (pd[:, k * 256:(k + 1) * 256] for k in range(4))
    u = dc * dh
    ue = jnp.concatenate([hist_ref[1], u], axis=0)
    hist_ref[1] = u[tm - HALO:]
    conv = (cw_ref[0:1, :] * back(ue, 2) + cw_ref[1:2, :] * back(ue, 1) + cw_ref[2:3, :] * ue)[HALO:]
    ybd_ref[0, :, 256:512] = (db * conv * _silu(dg)).astype(BF16)

    nblk = tm // Q_TILE
    qT = norm_t(proj(SEG['aq'], 256), gq_ref)
    for t in range(nblk):
        qT_ref[0, t] = qT[:, t * Q_TILE:(t + 1) * Q_TILE]
    kn = norm_rows(proj(SEG['ak'], 256), gk_ref, 256)
    kaug = build_kaug(kn, kca_ref[...].astype(F32))
    for hh in range(N_HEADS):
        kaug_ref[0, hh] = kaug[hh]
    vT = proj(SEG['av'], 256).T.astype(BF16)
    for t in range(nblk):
        vT_ref[0, t] = vT[:, t * Q_TILE:(t + 1) * Q_TILE]
    cqT = norm_t(proj(SEG['cq'], 256), gcq_ref)
    for t in range(nblk):
        cqT_ref[0, t] = cqT[:, t * Q_TILE:(t + 1) * Q_TILE]
    ckv = proj(SEG['ck'], 256)
    ck_ref[0] = norm_rows(ckv[:, 0:128], gck_ref, 128).astype(BF16)
    cvT = ckv[:, 128:256].T.astype(BF16)
    for t in range(tm // SWA_BLOCK):
        cvT_ref[0, t] = cvT[:, t * SWA_BLOCK:(t + 1) * SWA_BLOCK]
    gates_ref[0, :, 0:256] = _silu(proj(SEG['ag'], 256)).astype(BF16)
    gates_ref[0, :, 256:512] = _silu(proj(SEG['cg'], 256)).astype(BF16)


def _in_proj(x, g, w, bd, gq, gk, gcq, gck, kca, invcnt, pw, pscale, cw):
    B, S, _ = x.shape
    tm = IN_TILE
    ns = S // tm
    nq = S // Q_TILE
    nkb = S // SWA_BLOCK
    const = lambda *shape: pl.BlockSpec(shape, lambda b, s: (0,) * len(shape))
    out_shape = (
        jax.ShapeDtypeStruct((B, nq, 256, Q_TILE), BF16),
        jax.ShapeDtypeStruct((B, N_HEADS, S, AUG), BF16),
        jax.ShapeDtypeStruct((B, nq, 256, Q_TILE), BF16),
        jax.ShapeDtypeStruct((B, nq, 256, Q_TILE), BF16),
        jax.ShapeDtypeStruct((B, S, 128), BF16),
        jax.ShapeDtypeStruct((B, nkb, 128, SWA_BLOCK), BF16),
        jax.ShapeDtypeStruct((B, S, 512), BF16),
        jax.ShapeDtypeStruct((B, S, 512), BF16),
    )
    r = tm // Q_TILE
    out_specs = (
        pl.BlockSpec((1, r, 256, Q_TILE), lambda b, s: (b, s, 0, 0)),
        pl.BlockSpec((1, N_HEADS, tm, AUG), lambda b, s: (b, 0, s, 0)),
        pl.BlockSpec((1, r, 256, Q_TILE), lambda b, s: (b, s, 0, 0)),
        pl.BlockSpec((1, r, 256, Q_TILE), lambda b, s: (b, s, 0, 0)),
        pl.BlockSpec((1, tm, 128), lambda b, s: (b, s, 0)),
        pl.BlockSpec((1, tm // SWA_BLOCK, 128, SWA_BLOCK), lambda b, s: (b, s, 0, 0)),
        pl.BlockSpec((1, tm, 512), lambda b, s: (b, s, 0)),
        pl.BlockSpec((1, tm, 512), lambda b, s: (b, s, 0)),
    )
    in_specs = [
        pl.BlockSpec((1, tm, D_MODEL), lambda b, s: (b, s, 0)),
        const(1, D_MODEL),
        const(D_MODEL, IN_PROJ_WIDTH),
        const(256, 256),
        const(256, tm), const(1, 256), const(256, tm), const(1, 128),
        pl.BlockSpec((tm, AUG), lambda b, s: (s, 0)),
        pl.BlockSpec((1, tm, 256), lambda b, s: (jnp.minimum(s, 1), 0, 0)),
        const(256, 256), const(1, 256), const(CONV_WIDTH, 256),
    ]
    return pl.pallas_call(
        _in_proj_kernel, grid=(B, ns), in_specs=in_specs, out_specs=out_specs, out_shape=out_shape,
        scratch_shapes=[pltpu.VMEM((2, HALO, 256), F32)],
        compiler_params=pltpu.CompilerParams(dimension_semantics=("parallel", "arbitrary"),
                                             vmem_limit_bytes=VMEM_LIMIT),
        name="in_proj",
    )(x, g, w, bd, gq, gk, gcq, gck, kca, invcnt, pw, pscale, cw)


def _attn_kernel(qT_ref, kaug_ref, vT_ref, avg_ref, crow_ref, causal_ref, gate_ref,
                 cqT_ref, ck_ref, cvT_ref, band0_ref, band_ref, sink_ref, cgate_ref,
                 x_ref, ybd_ref, wo_ref,
                 out_ref,
                 kmean_ref, qaug_ref, acc_ref, s_ref, p_ref, mx_ref, cs_ref, cp_ref, cmx_ref):
    i = pl.program_id(1)
    swa_blocks = Q_TILE // SWA_BLOCK
    S = kaug_ref.shape[2]
    nb = S // MOBA_BLOCK
    chunk = 1024

    @pl.when(i == 0)
    def _():
        for hh in range(N_HEADS):
            acc = jnp.zeros((nb, AUG), F32)
            for c in range(S // chunk):
                acc = acc + _dot(avg_ref[:, c * chunk:(c + 1) * chunk],
                                 kaug_ref[0, hh, c * chunk:(c + 1) * chunk, :])
            kmean_ref[hh] = acc

    row = lax.broadcasted_iota(jnp.int32, (nb, Q_TILE), 0).astype(F32)
    i_f = i.astype(F32)
    neg_inf = jnp.float32(-jnp.inf)
    heads = range(N_HEADS)
    vh = lambda hh, j: vT_ref[0, j, hh * HEAD_DIM:(hh + 1) * HEAD_DIM, :]
    kb = lambda hh, j: kaug_ref[0, hh, pl.ds(pl.multiple_of(j * MOBA_BLOCK, MOBA_BLOCK), MOBA_BLOCK), :]

    for hh in heads:
        qh = qT_ref[0, 0, hh * HEAD_DIM:(hh + 1) * HEAD_DIM, :]
        km = kmean_ref[hh][:, 0:HEAD_DIM]
        km_hi = km.astype(BF16)
        km_lo = (km - km_hi.astype(F32)).astype(BF16)
        bs = _dot(km_hi, qh) + _dot(km_lo, qh)
        xs = jnp.where(row < i_f, bs, neg_inf)
        sel = row == i_f
        for _ in range(MOBA_TOPK):
            mx = jnp.max(xs, axis=0, keepdims=True)
            first = jnp.min(jnp.where(xs == mx, row, jnp.float32(nb)), axis=0, keepdims=True)
            pick = row == first
            sel = sel | (pick & (mx > neg_inf))
            xs = jnp.where(pick, neg_inf, xs)
        negmask = jnp.where(sel, 0.0, NEG_BIG).astype(BF16)
        pad = jnp.zeros((COL_POS - COL_ONEHOT - nb, Q_TILE), BF16) if nb < 32 else None
        parts = [qh, negmask] + ([pad] if pad is not None else []) + [crow_ref[hh]]
        qaug_ref[hh] = jnp.concatenate(parts, axis=0)

    _swa_front(i, swa_blocks, cqT_ref, ck_ref, band0_ref, band_ref, sink_ref, cs_ref, cp_ref, cmx_ref)

    def blk(pos):
        past = jnp.minimum(i + 1, nb - 1)
        return jnp.where(pos <= 0, i, jnp.where(pos <= i, pos - 1, past))

    kblk = MOBA_BLOCK
    sub = 8
    rows = 64

    def scores(hh, j0, j1, mask=None):
        s = _dot(jnp.concatenate([kb(hh, j0), kb(hh, j1)], axis=0), qaug_ref[hh])
        if mask is not None:
            s = s + mask
        s_ref[hh] = s
        mx_ref[hh] = _fold8(s, jnp.maximum)

    def scores_head(hh, t):
        scores(hh, blk(2 * t), blk(2 * t + 1))

    def scores_step(t):
        for hh in heads:
            scores_head(hh, t)

    def softmax_head(hh, m_old, l_old):
        m_new = jnp.maximum(m_old, jnp.max(mx_ref[hh], axis=0, keepdims=True))
        alpha = jnp.exp2(m_old - m_new)
        l8 = alpha * l_old
        for c in range(2 * kblk // rows):
            p = jnp.exp2(s_ref[hh, c * rows:(c + 1) * rows] - m_new)
            p_ref[hh, c * rows:(c + 1) * rows] = p.astype(BF16)
            l8 = l8 + _fold8(p, jnp.add)
        return m_new, l8, alpha

    def softmax_step(carry):
        ms, ls, _ = carry
        new = [softmax_head(hh, ms[hh], ls[hh]) for hh in heads]
        return tuple(zip(*new))

    def values_head(hh, t, alpha):
        acc_ref[hh] = (alpha * acc_ref[hh]
                       + _dot(vh(hh, blk(2 * t)), p_ref[hh, 0:kblk])
                       + _dot(vh(hh, blk(2 * t + 1)), p_ref[hh, kblk:2 * kblk]))

    def values_step(t, alphas):
        for hh in heads:
            values_head(hh, t, alphas[hh])

    def body(t, carry):
        ms, ls, alphas = carry
        new = []
        for hh in heads:
            values_head(hh, t - 1, alphas[hh])
            new.append(softmax_head(hh, ms[hh], ls[hh]))
            scores_head(hh, t + 1)
        return tuple(zip(*new))

    def drain(carry):
        values_step(steps - 2, carry[2])
        return softmax_step(carry)

    steps = jnp.maximum((i + 2) // 2, 2)
    neg = jnp.full((1, Q_TILE), NEG_BIG, F32)
    zero8 = jnp.zeros((sub, Q_TILE), F32)
    acc_ref[...] = jnp.zeros(acc_ref.shape, F32)
    for hh in heads:
        scores(hh, i, blk(1), causal_ref[...])
    carry = softmax_step(((neg,) * N_HEADS, (zero8,) * N_HEADS, None))
    scores_step(1)
    trips = jnp.maximum(steps - 2, 0)
    carry = lax.fori_loop(0, trips // 2, lambda k, c: body(2 * k + 2, body(2 * k + 1, c)), carry)
    carry = lax.cond(trips % 2 == 1, lambda c: body(steps - 2, c), lambda c: c, carry)
    carry = drain(carry)
    ms, ls, alphas = carry
    values_step(steps - 1, alphas)
    yc = _swa_back(i, swa_blocks, cvT_ref, cp_ref, cmx_ref, cgate_ref)[0]
    part = (x_ref[0] + _dot(ybd_ref[0, :, 0:256], wo_ref[256:512, :]) + _dot(yc, wo_ref[512:768, :])
            + _dot(ybd_ref[0, :, 256:512], wo_ref[768:1024, :]))
    outs = [acc_ref[hh] / jnp.sum(ls[hh], axis=0, keepdims=True) for hh in heads]
    o = jnp.concatenate(outs, axis=0).T
    ya = (o * gate_ref[0].astype(F32)).astype(BF16)
    out_ref[0] = part + _dot(ya, wo_ref[0:256, :])


def _attention(qT, kaug, vT, avg, crow, causal, cqT, ck, cvT, band, sink, gates, x, ybd, wo):
    B, nq = qT.shape[0], qT.shape[1]
    S = kaug.shape[2]
    nb = S // MOBA_BLOCK
    nkb = S // SWA_BLOCK
    nchain = (Q_TILE // SWA_BLOCK) * C_KV_HEADS
    win = 2 * SWA_BLOCK
    tile = lambda col: pl.BlockSpec((1, Q_TILE, 256), lambda b, i: (b, i, col))
    in_specs = [
        pl.BlockSpec((1, 1, 256, Q_TILE), lambda b, i: (b, i, 0, 0)),
        pl.BlockSpec((1, N_HEADS, S, AUG), lambda b, i: (b, 0, 0, 0)),
        pl.BlockSpec((1, nq, 256, Q_TILE), lambda b, i: (b, 0, 0, 0)),
        pl.BlockSpec((nb, S), lambda b, i: (0, 0)),
        pl.BlockSpec((N_HEADS, AUG - COL_POS, Q_TILE), lambda b, i: (0, 0, 0)),
        pl.BlockSpec((2 * MOBA_BLOCK, Q_TILE), lambda b, i: (0, 0)),
        tile(0),
        pl.BlockSpec((1, 1, 256, Q_TILE), lambda b, i: (b, i, 0, 0)),
        pl.BlockSpec((1, S, 128), lambda b, i: (b, 0, 0)),
        pl.BlockSpec((1, nkb, 128, SWA_BLOCK), lambda b, i: (b, 0, 0, 0)),
        pl.BlockSpec((C_KV_HEADS, 1, win, win), lambda b, i: (0, jnp.minimum(i, 1), 0, 0)),
        pl.BlockSpec((C_KV_HEADS, 1, win, win), lambda b, i: (0, 1, 0, 0)),
        pl.BlockSpec((C_KV_HEADS, 1, win), lambda b, i: (0, 0, 0)),
        tile(1),
        pl.BlockSpec((1, Q_TILE, D_MODEL), lambda b, i: (b, i, 0)),
        pl.BlockSpec((1, Q_TILE, 512), lambda b, i: (b, i, 0)),
        pl.BlockSpec((D_MODEL, D_MODEL), lambda b, i: (0, 0)),
    ]
    return pl.pallas_call(
        _attn_kernel, grid=(B, nq), in_specs=in_specs,
        out_specs=pl.BlockSpec((1, Q_TILE, D_MODEL), lambda b, i: (b, i, 0)),
        out_shape=jax.ShapeDtypeStruct((B, S, D_MODEL), F32),
        scratch_shapes=[pltpu.VMEM((N_HEADS, nb, AUG), F32),
                        pltpu.VMEM((N_HEADS, AUG, Q_TILE), BF16),
                        pltpu.VMEM((N_HEADS, HEAD_DIM, Q_TILE), F32),
                        pltpu.VMEM((N_HEADS, 2 * MOBA_BLOCK, Q_TILE), F32),
                        pltpu.VMEM((N_HEADS, 2 * MOBA_BLOCK, Q_TILE), BF16),
                        pltpu.VMEM((N_HEADS, 8, Q_TILE), F32),
                        pltpu.VMEM((nchain, win, win), F32),
                        pltpu.VMEM((nchain, win, win), BF16),
                        pltpu.VMEM((nchain, 8, win), F32)],
        compiler_params=pltpu.CompilerParams(dimension_semantics=("parallel", "arbitrary"),
                                             vmem_limit_bytes=VMEM_LIMIT),
        name="attention",
    )(qT, kaug, vT, avg, crow, causal, gates, cqT, ck, cvT, band, band, sink, gates, x, ybd, wo)


def _swa_front(i, blocks, cqT_ref, ck_ref, band0_ref, band_ref, sink_ref, s_ref, p_ref, mx_ref):
    win = 2 * SWA_BLOCK
    rows = 64
    zrows = jnp.zeros((HEAD_DIM, win), BF16)
    chains = [(t, kv) for t in range(blocks) for kv in range(C_KV_HEADS)]
    for t in range(blocks):
        n = i * blocks + t
        prev = jnp.maximum(n - 1, 0)
        kwin = jnp.concatenate([
            ck_ref[0, pl.ds(pl.multiple_of(prev * SWA_BLOCK, SWA_BLOCK), SWA_BLOCK), :],
            ck_ref[0, pl.ds(pl.multiple_of(n * SWA_BLOCK, SWA_BLOCK), SWA_BLOCK), :]], axis=0)
        for kv in range(C_KV_HEADS):
            c = chains.index((t, kv))
            q2 = jnp.concatenate([
                cqT_ref[0, t // 2, (2 * kv + g) * HEAD_DIM:(2 * kv + g + 1) * HEAD_DIM,
                        (t % 2) * SWA_BLOCK:(t % 2 + 1) * SWA_BLOCK] for g in range(2)], axis=1)
            qz = jnp.concatenate([q2, zrows] if kv == 0 else [zrows, q2], axis=0)
            band = band0_ref[kv, 0] if t == 0 else band_ref[kv, 0]
            s = _dot(kwin, qz) + band
            s_ref[c] = s
            mx_ref[c] = _fold8(s, jnp.maximum)
    for c, (t, kv) in enumerate(chains):
        m = jnp.max(mx_ref[c], axis=0, keepdims=True)
        l8 = jnp.zeros((8, win), F32)
        for r in range(win // rows):
            p = jnp.exp2(s_ref[c, r * rows:(r + 1) * rows] - m)
            p_ref[c, r * rows:(r + 1) * rows] = p.astype(BF16)
            l8 = l8 + _fold8(p, jnp.add)
        mx_ref[c, 0:1] = 1.0 / (jnp.sum(l8, axis=0, keepdims=True) + jnp.exp2(sink_ref[kv] - m))


def _swa_back(i, blocks, cvT_ref, p_ref, mx_ref, gate_ref):
    chains = [(t, kv) for t in range(blocks) for kv in range(C_KV_HEADS)]
    outs = {}
    for c, (t, kv) in enumerate(chains):
        n = i * blocks + t
        prev = jnp.maximum(n - 1, 0)
        vwin = jnp.concatenate([cvT_ref[0, prev, kv * HEAD_DIM:(kv + 1) * HEAD_DIM, :],
                                cvT_ref[0, n, kv * HEAD_DIM:(kv + 1) * HEAD_DIM, :]], axis=1)
        outs[t, kv] = _dot(vwin, p_ref[c]) * mx_ref[c, 0:1]
    tiles = []
    for u in range(blocks * SWA_BLOCK // Q_TILE):
        head_rows = []
        for kv in range(C_KV_HEADS):
            for g in range(2):
                head_rows.append(jnp.concatenate(
                    [outs[t, kv][:, g * SWA_BLOCK:(g + 1) * SWA_BLOCK] for t in (2 * u, 2 * u + 1)], axis=1))
        o = jnp.concatenate(head_rows, axis=0).T
        gate = gate_ref[0, u * Q_TILE:(u + 1) * Q_TILE, :].astype(F32)
        tiles.append((o * gate).astype(BF16))
    return tiles


def _tables(S):
    tm = IN_TILE
    slopes_c, slopes_a = SLOPES_C, SLOPES_A
    pos = np.arange(S)

    kca = np.zeros((S, AUG), np.float32)
    blk, r = pos // MOBA_BLOCK, pos % MOBA_BLOCK
    kca[pos, COL_ONEHOT + blk] = 1.0
    kca[:, COL_POS] = r
    kca[:, COL_POS + 1] = r
    kca[:, COL_POS + 2] = blk
    kca[:, COL_POS + 3] = blk
    kca = jnp.asarray(kca, BF16)

    c = slopes_a * np.float32(LOG2E)
    hi = c.astype(BF16).astype(np.float32)
    lo = c - hi
    rows = np.stack([hi, lo, hi * MOBA_BLOCK, lo * MOBA_BLOCK], axis=-1)
    rows = np.pad(rows, ((0, 0), (0, AUG - COL_POS - 4)))
    crow_a = jnp.asarray(np.broadcast_to(rows[:, :, None], rows.shape + (Q_TILE,)), BF16)

    nb = S // MOBA_BLOCK
    avg = jnp.asarray((pos[None, :] // MOBA_BLOCK == np.arange(nb)[:, None]) / MOBA_BLOCK, BF16)
    kq = np.arange(MOBA_BLOCK)
    causal = np.where(kq[:, None] <= kq[None, :], 0.0, NEG_BIG)
    causal = jnp.asarray(np.concatenate([causal, np.zeros_like(causal)], axis=0), F32)
    u = np.arange(2 * SWA_BLOCK)[:, None]
    t = np.arange(SWA_BLOCK)[None, :]
    ok = (u > t) & (u <= t + SWA_BLOCK)
    dist = (SWA_BLOCK + t - u).astype(np.float32)
    sc = np.asarray(slopes_c, np.float32) * np.float32(LOG2E)
    band = []
    for kv in range(C_KV_HEADS):
        bias = np.concatenate([-sc[2 * kv] * dist, -sc[2 * kv + 1] * dist], axis=1)
        ok2 = np.concatenate([ok, ok], axis=1)
        band.append(np.stack([np.where(ok2 & (u >= SWA_BLOCK), bias, NEG_BIG),
                              np.where(ok2, bias, NEG_BIG)]))
    band = jnp.asarray(np.stack(band), F32)
    w = np.repeat(np.asarray(POOL_WINDOWS, np.float32), 64)[None, :]
    first = 1.0 / np.minimum(np.arange(tm, dtype=np.float32)[:, None] + 1.0, w)
    invcnt = jnp.asarray(np.stack([first, np.broadcast_to(1.0 / w, (tm, 256))]), F32)
    return dict(kca=kca, crow_a=crow_a, avg=avg, causal=causal, band=band, invcnt=invcnt)


def kernel(x, norm_g, w_in, w_out, a_q_norm, a_k_norm, pool_w, pool_scale, c_q_norm, c_k_norm, c_sinks, conv_w):
    B, S, _ = x.shape
    depth = norm_g.shape[0]
    assert S % IN_TILE == 0 and S // MOBA_BLOCK <= 32
    tb = _tables(S)
    tm = IN_TILE
    d = np.arange(256)
    bd = jnp.asarray(d[:, None] // HEAD_DIM == d[None, :] // HEAD_DIM, BF16)
    qscale = QK_SCALE * LOG2E
    for l in range(depth):
        gq = jnp.broadcast_to(jnp.tile(a_q_norm[l] * qscale, N_HEADS)[:, None], (256, tm))
        gcq = jnp.broadcast_to(jnp.tile(c_q_norm[l] * qscale, N_HEADS)[:, None], (256, tm))
        gk = jnp.tile(a_k_norm[l], N_HEADS)[None, :]
        gck = jnp.tile(c_k_norm[l], C_KV_HEADS)[None, :]
        pw = jnp.zeros((256, 256), F32)
        for g in range(4):
            pw = pw.at[g * 64:(g + 1) * 64, g * 64:(g + 1) * 64].set(pool_w[l, g])
        qT, kaug, vT, cqT, ck, cvT, gates, ybd = _in_proj(
            x, norm_g[l][None, :], w_in[l].astype(BF16), bd, gq, gk, gcq, gck, tb['kca'],
            tb['invcnt'], pw.astype(BF16), pool_scale[l][None, :], conv_w[l])
        sink = jnp.repeat((c_sinks[l] * LOG2E).reshape(C_KV_HEADS, 2), SWA_BLOCK, axis=1)[:, None, :]
        x = _attention(qT, kaug, vT, tb['avg'], tb['crow_a'], tb['causal'],
                       cqT, ck, cvT, tb['band'], sink, gates,
                       x, ybd, w_out[l].astype(BF16))
    return x
```

```python
import math

import jax
import jax.numpy as jnp
import numpy as np
from jax import lax
from jax.experimental import pallas as pl
from jax.experimental.pallas import tpu as pltpu

F32 = jnp.float32
BF16 = jnp.bfloat16

D_MODEL = 1024
HEAD_DIM = 64
N_HEADS = 4
C_KV_HEADS = 2
NORM_EPS = 1e-6
MOBA_BLOCK = 256
MOBA_TOPK = 3
SWA_BLOCK = 128
POOL_WINDOWS = (2, 4, 8, 16)
CONV_WIDTH = 3
HALO = 16

AUG = 128
COL_ONEHOT = 64
COL_POS = 96
NEG_BIG = -(2.0 ** 100)
LOG2E = math.log2(math.e)
QK_SCALE = HEAD_DIM ** -0.5
SLOPES = np.exp2(-(8.0 / (2 * N_HEADS)) * np.arange(1, 2 * N_HEADS + 1)).astype(np.float32)
SLOPES_C, SLOPES_A = SLOPES[:N_HEADS], SLOPES[N_HEADS:]

IN_TILE = 1024
Q_TILE = 256
VMEM_LIMIT = 56 * 1024 * 1024

SEG = dict(aq=0, ak=256, av=512, ag=768, bu=1024, bg=1280, cq=1536, ck=1792, cv=1920,
           cg=2048, dh=2304, db=2560, dc=2816, dg=3072)
IN_PROJ_WIDTH = 3328


def _dot(a, b):
    return jnp.dot(a, b, preferred_element_type=F32)


def _silu(v):
    h = 0.5 * v
    return h + h * jnp.tanh(h)


def _fold8(x, op):
    out = x[0:8]
    for g in range(1, x.shape[0] // 8):
        out = op(out, x[g * 8:(g + 1) * 8])
    return out


def _in_proj_kernel(x_ref, g_ref, w_ref, bd_ref, gq_ref, gk_ref, gcq_ref, gck_ref, kca_ref,
                    invcnt_ref, pw_ref, pscale_ref, cw_ref,
                    qT_ref, kaug_ref, vT_ref, cqT_ref, ck_ref, cvT_ref, gates_ref, ybd_ref,
                    hist_ref):
    tm = x_ref.shape[1]
    s_idx = pl.program_id(1)
    chunk = 256
    hs, pb_rows = [], []
    for r in range(tm // chunk):
        xr = x_ref[0, r * chunk:(r + 1) * chunk, :]
        ms = jnp.mean(xr * xr, axis=-1, keepdims=True)
        hs.append((xr * lax.rsqrt(ms + NORM_EPS) * g_ref[...]).astype(BF16))
        pb_rows.append(_dot(hs[-1], w_ref[:, SEG['bu']:SEG['bu'] + 512]))
    h = jnp.concatenate(hs, axis=0)

    def proj(lo, width):
        return _dot(h, w_ref[:, lo:lo + width])

    def norm_t(p, gain_ref):
        pt = p.T
        outs = []
        for hh in range(N_HEADS):
            ph = pt[hh * HEAD_DIM:(hh + 1) * HEAD_DIM]
            ss = jnp.sum(ph * ph, axis=0, keepdims=True) * (1.0 / HEAD_DIM)
            outs.append(ph * lax.rsqrt(ss + NORM_EPS))
        return (jnp.concatenate(outs, axis=0) * gain_ref[...]).astype(BF16)

    def norm_rows(p, gain_ref, width):
        ss = _dot((p * p).astype(BF16), bd_ref[0:width, 0:width]) * (1.0 / HEAD_DIM)
        return p * lax.rsqrt(ss + NORM_EPS) * gain_ref[...]

    def build_kaug(kn, kconst):
        lane = lax.broadcasted_iota(jnp.int32, (tm, AUG), 1)
        blocks = []
        for hh in range(N_HEADS):
            col = kn[:, (hh // 2) * AUG:(hh // 2 + 1) * AUG]
            if hh % 2 == 1:
                col = pltpu.roll(col, HEAD_DIM, axis=1)
            blocks.append(jnp.where(lane < HEAD_DIM, col, kconst).astype(BF16))
        return blocks


    @pl.when(s_idx == 0)
    def _():
        hist_ref[...] = jnp.zeros(hist_ref.shape, F32)

    back = lambda v, k: pltpu.roll(v, k, axis=0)

    pb = jnp.concatenate(pb_rows, axis=0)
    bu, bg = pb[:, 0:256], pb[:, 256:512]
    e = jnp.concatenate([hist_ref[0], bu], axis=0)
    hist_ref[0] = bu[tm - HALO:]
    a2 = e + back(e, 1)
    a4 = a2 + back(a2, 2)
    a4r = a4[:, 128:256]
    a8 = a4r + back(a4r, 4)
    a16 = a8 + back(a8, 8)
    lane = lax.broadcasted_iota(jnp.int32, (HALO + tm, 128), 1)
    sums = jnp.concatenate([jnp.where(lane < 64, a2[:, 0:128], a4[:, 0:128]),
                            jnp.where(lane < 64, a8, a16)], axis=1)
    pooled = sums[HALO:] * invcnt_ref[0] - bu
    yb = _dot(pooled.astype(BF16), pw_ref[...]) * pscale_ref[...] * _silu(bg)
    ybd_ref[0, :, 0:256] = yb.astype(BF16)

    pd = proj(SEG['dh'], 1024)
    dh, db, dc, dg = (pd[:, k * 256:(k + 1) * 256] for k in range(4))
    u = dc * dh
    ue = jnp.concatenate([hist_ref[1], u], axis=0)
    hist_ref[1] = u[tm - HALO:]
    conv = (cw_ref[0:1, :] * back(ue, 2) + cw_ref[1:2, :] * back(ue, 1) + cw_ref[2:3, :] * ue)[HALO:]
    ybd_ref[0, :, 256:512] = (db * conv * _silu(dg)).astype(BF16)

    nblk = tm // Q_TILE
    vT = proj(SEG['av'], 256).T.astype(BF16)
    for t in range(nblk):
        vT_ref[0, t] = vT[:, t * Q_TILE:(t + 1) * Q_TILE]
    qT = norm_t(proj(SEG['aq'], 256), gq_ref)
    for t in range(nblk):
        qT_ref[0, t] = qT[:, t * Q_TILE:(t + 1) * Q_TILE]
    kn = norm_rows(proj(SEG['ak'], 256), gk_ref, 256)
    kaug = build_kaug(kn, kca_ref[...].astype(F32))
    for hh in range(N_HEADS):
        kaug_ref[0, hh] = kaug[hh]
    cqT = norm_t(proj(SEG['cq'], 256), gcq_ref)
    for t in range(nblk):
        cqT_ref[0, t] = cqT[:, t * Q_TILE:(t + 1) * Q_TILE]
    ckv = proj(SEG['ck'], 256)
    ck_ref[0] = norm_rows(ckv[:, 0:128], gck_ref, 128).astype(BF16)
    cvT = ckv[:, 128:256].T.astype(BF16)
    for t in range(tm // SWA_BLOCK):
        cvT_ref[0, t] = cvT[:, t * SWA_BLOCK:(t + 1) * SWA_BLOCK]
    gates_ref[0, :, 0:256] = _silu(proj(SEG['ag'], 256)).astype(BF16)
    gates_ref[0, :, 256:512] = _silu(proj(SEG['cg'], 256)).astype(BF16)


def _in_proj(x, g, w, bd, gq, gk, gcq, gck, kca, invcnt, pw, pscale, cw):
    B, S, _ = x.shape
    tm = IN_TILE
    ns = S // tm
    nq = S // Q_TILE
    nkb = S // SWA_BLOCK
    const = lambda *shape: pl.BlockSpec(shape, lambda b, s: (0,) * len(shape))
    out_shape = (
        jax.ShapeDtypeStruct((B, nq, 256, Q_TILE), BF16),
        jax.ShapeDtypeStruct((B, N_HEADS, S, AUG), BF16),
        jax.ShapeDtypeStruct((B, nq, 256, Q_TILE), BF16),
        jax.ShapeDtypeStruct((B, nq, 256, Q_TILE), BF16),
        jax.ShapeDtypeStruct((B, S, 128), BF16),
        jax.ShapeDtypeStruct((B, nkb, 128, SWA_BLOCK), BF16),
        jax.ShapeDtypeStruct((B, S, 512), BF16),
        jax.ShapeDtypeStruct((B, S, 512), BF16),
    )
    r = tm // Q_TILE
    out_specs = (
        pl.BlockSpec((1, r, 256, Q_TILE), lambda b, s: (b, s, 0, 0)),
        pl.BlockSpec((1, N_HEADS, tm, AUG), lambda b, s: (b, 0, s, 0)),
        pl.BlockSpec((1, r, 256, Q_TILE), lambda b, s: (b, s, 0, 0)),
        pl.BlockSpec((1, r, 256, Q_TILE), lambda b, s: (b, s, 0, 0)),
        pl.BlockSpec((1, tm, 128), lambda b, s: (b, s, 0)),
        pl.BlockSpec((1, tm // SWA_BLOCK, 128, SWA_BLOCK), lambda b, s: (b, s, 0, 0)),
        pl.BlockSpec((1, tm, 512), lambda b, s: (b, s, 0)),
        pl.BlockSpec((1, tm, 512), lambda b, s: (b, s, 0)),
    )
    in_specs = [
        pl.BlockSpec((1, tm, D_MODEL), lambda b, s: (b, s, 0)),
        const(1, D_MODEL),
        const(D_MODEL, IN_PROJ_WIDTH),
        const(256, 256),
        const(256, tm), const(1, 256), const(256, tm), const(1, 128),
        pl.BlockSpec((tm, AUG), lambda b, s: (s, 0)),
        pl.BlockSpec((1, tm, 256), lambda b, s: (jnp.minimum(s, 1), 0, 0)),
        const(256, 256), const(1, 256), const(CONV_WIDTH, 256),
    ]
    return pl.pallas_call(
        _in_proj_kernel, grid=(B, ns), in_specs=in_specs, out_specs=out_specs, out_shape=out_shape,
        scratch_shapes=[pltpu.VMEM((2, HALO, 256), F32)],
        compiler_params=pltpu.CompilerParams(dimension_semantics=("parallel", "arbitrary"),
                                             vmem_limit_bytes=VMEM_LIMIT),
        name="in_proj",
    )(x, g, w, bd, gq, gk, gcq, gck, kca, invcnt, pw, pscale, cw)


def _attn_kernel(qT_ref, kaug_ref, vT_ref, avg_ref, crow_ref, causal_ref, gate_ref,
                 cqT_ref, ck_ref, cvT_ref, band0_ref, band_ref, sink_ref, cgate_ref,
                 x_ref, ybd_ref, wo_ref,
                 out_ref,
                 kmean_ref, qaug_ref, acc_ref, s_ref, p_ref, mx_ref, cs_ref, cp_ref, cmx_ref):
    i = pl.program_id(1)
    swa_blocks = Q_TILE // SWA_BLOCK
    S = kaug_ref.shape[2]
    nb = S // MOBA_BLOCK
    chunk = 1024

    @pl.when(i == 0)
    def _():
        for hh in range(N_HEADS):
            acc = jnp.zeros((nb, AUG), F32)
            for c in range(S // chunk):
                acc = acc + _dot(avg_ref[:, c * chunk:(c + 1) * chunk],
                                 kaug_ref[0, hh, c * chunk:(c + 1) * chunk, :])
            kmean_ref[hh] = acc

    row = lax.broadcasted_iota(jnp.int32, (nb, Q_TILE), 0).astype(F32)
    i_f = i.astype(F32)
    neg_inf = jnp.float32(-jnp.inf)
    heads = range(N_HEADS)
    vh = lambda hh, j: vT_ref[0, j, hh * HEAD_DIM:(hh + 1) * HEAD_DIM, :]
    kb = lambda hh, j: kaug_ref[0, hh, pl.ds(pl.multiple_of(j * MOBA_BLOCK, MOBA_BLOCK), MOBA_BLOCK), :]

    for hh in heads:
        qh = qT_ref[0, 0, hh * HEAD_DIM:(hh + 1) * HEAD_DIM, :]
        km = kmean_ref[hh][:, 0:HEAD_DIM]
        km_hi = km.astype(BF16)
        km_lo = (km - km_hi.astype(F32)).astype(BF16)
        bs = _dot(km_hi, qh) + _dot(km_lo, qh)
        xs = jnp.where(row < i_f, bs, neg_inf)
        sel = row == i_f
        for _ in range(MOBA_TOPK):
            mx = jnp.max(xs, axis=0, keepdims=True)
            first = jnp.min(jnp.where(xs == mx, row, jnp.float32(nb)), axis=0, keepdims=True)
            pick = row == first
            sel = sel | (pick & (mx > neg_inf))
            xs = jnp.where(pick, neg_inf, xs)
        negmask = jnp.where(sel, 0.0, NEG_BIG).astype(BF16)
        pad = jnp.zeros((COL_POS - COL_ONEHOT - nb, Q_TILE), BF16) if nb < 32 else None
        parts = [qh, negmask] + ([pad] if pad is not None else []) + [crow_ref[hh]]
        qaug_ref[hh] = jnp.concatenate(parts, axis=0)

    _swa_front(i, swa_blocks, cqT_ref, ck_ref, band0_ref, band_ref, sink_ref, cs_ref, cp_ref, cmx_ref)

    def blk(pos):
        past = jnp.minimum(i + 1, nb - 1)
        return jnp.where(pos <= 0, i, jnp.where(pos <= i, pos - 1, past))

    kblk = MOBA_BLOCK
    sub = 8
    rows = 64

    def scores(hh, j0, j1, mask=None):
        s = _dot(jnp.concatenate([kb(hh, j0), kb(hh, j1)], axis=0), qaug_ref[hh])
        if mask is not None:
            s = s + mask
        s_ref[hh] = s
        mx_ref[hh] = _fold8(s, jnp.maximum)

    def scores_head(hh, t):
        scores(hh, blk(2 * t), blk(2 * t + 1))

    def scores_step(t):
        for hh in heads:
            scores_head(hh, t)

    def softmax_head(hh, m_old, l_old):
        m_new = jnp.maximum(m_old, jnp.max(mx_ref[hh], axis=0, keepdims=True))
        alpha = jnp.exp2(m_old - m_new)
        l8 = alpha * l_old
        for c in range(2 * kblk // rows):
            p = jnp.exp2(s_ref[hh, c * rows:(c + 1) * rows] - m_new)
            p_ref[hh, c * rows:(c + 1) * rows] = p.astype(BF16)
            l8 = l8 + _fold8(p, jnp.add)
        return m_new, l8, alpha

    def softmax_step(carry):
        ms, ls, _ = carry
        new = [softmax_head(hh, ms[hh], ls[hh]) for hh in heads]
        return tuple(zip(*new))

    def values_head(hh, t, alpha):
        acc_ref[hh] = (alpha * acc_ref[hh]
                       + _dot(vh(hh, blk(2 * t)), p_ref[hh, 0:kblk])
                       + _dot(vh(hh, blk(2 * t + 1)), p_ref[hh, kblk:2 * kblk]))

    def values_step(t, alphas):
        for hh in heads:
            values_head(hh, t, alphas[hh])

    def body(t, carry):
        ms, ls, alphas = carry
        new = []
        for hh in heads:
            values_head(hh, t - 1, alphas[hh])
            new.append(softmax_head(hh, ms[hh], ls[hh]))
            scores_head(hh, t + 1)
        return tuple(zip(*new))

    def drain(carry):
        values_step(steps - 2, carry[2])
        return softmax_step(carry)

    steps = jnp.maximum((i + 2) // 2, 2)
    neg = jnp.full((1, Q_TILE), NEG_BIG, F32)
    zero8 = jnp.zeros((sub, Q_TILE), F32)
    acc_ref[...] = jnp.zeros(acc_ref.shape, F32)
    for hh in heads:
        scores(hh, i, blk(1), causal_ref[...])
    carry = softmax_step(((neg,) * N_HEADS, (zero8,) * N_HEADS, None))
    scores_step(1)
    trips = jnp.maximum(steps - 2, 0)
    carry = lax.fori_loop(0, trips // 2, lambda k, c: body(2 * k + 2, body(2 * k + 1, c)), carry)
    carry = lax.cond(trips % 2 == 1, lambda c: body(steps - 2, c), lambda c: c, carry)
    carry = drain(carry)
    ms, ls, alphas = carry
    values_step(steps - 1, alphas)
    yc = _swa_back(i, swa_blocks, cvT_ref, cp_ref, cmx_ref, cgate_ref)[0]
    part = (x_ref[0] + _dot(ybd_ref[0, :, 0:256], wo_ref[256:512, :]) + _dot(yc, wo_ref[512:768, :])
            + _dot(ybd_ref[0, :, 256:512], wo_ref[768:1024, :]))
    outs = [acc_ref[hh] / jnp.sum(ls[hh], axis=0, keepdims=True) for hh in heads]
    o = jnp.concatenate(outs, axis=0).T
    ya = (o * gate_ref[0].astype(F32)).astype(BF16)
    out_ref[0] = part + _dot(ya, wo_ref[0:256, :])


def _attention(qT, kaug, vT, avg, crow, causal, cqT, ck, cvT, band, sink, gates, x, ybd, wo):
    B, nq = qT.shape[0], qT.shape[1]
    S = kaug.shape[2]
    nb = S // MOBA_BLOCK
    nkb = S // SWA_BLOCK
    nchain = (Q_TILE // SWA_BLOCK) * C_KV_HEADS
    win = 2 * SWA_BLOCK
    tile = lambda col: pl.BlockSpec((1, Q_TILE, 256), lambda b, i: (b, i, col))
    in_specs = [
        pl.BlockSpec((1, 1, 256, Q_TILE), lambda b, i: (b, i, 0, 0)),
        pl.BlockSpec((1, N_HEADS, S, AUG), lambda b, i: (b, 0, 0, 0)),
        pl.BlockSpec((1, nq, 256, Q_TILE), lambda b, i: (b, 0, 0, 0)),
        pl.BlockSpec((nb, S), lambda b, i: (0, 0)),
        pl.BlockSpec((N_HEADS, AUG - COL_POS, Q_TILE), lambda b, i: (0, 0, 0)),
        pl.BlockSpec((2 * MOBA_BLOCK, Q_TILE), lambda b, i: (0, 0)),
        tile(0),
        pl.BlockSpec((1, 1, 256, Q_TILE), lambda b, i: (b, i, 0, 0)),
        pl.BlockSpec((1, S, 128), lambda b, i: (b, 0, 0)),
        pl.BlockSpec((1, nkb, 128, SWA_BLOCK), lambda b, i: (b, 0, 0, 0)),
        pl.BlockSpec((C_KV_HEADS, 1, win, win), lambda b, i: (0, jnp.minimum(i, 1), 0, 0)),
        pl.BlockSpec((C_KV_HEADS, 1, win, win), lambda b, i: (0, 1, 0, 0)),
        pl.BlockSpec((C_KV_HEADS, 1, win), lambda b, i: (0, 0, 0)),
        tile(1),
        pl.BlockSpec((1, Q_TILE, D_MODEL), lambda b, i: (b, i, 0)),
        pl.BlockSpec((1, Q_TILE, 512), lambda b, i: (b, i, 0)),
        pl.BlockSpec((D_MODEL, D_MODEL), lambda b, i: (0, 0)),
    ]
    return pl.pallas_call(
        _attn_kernel, grid=(B, nq), in_specs=in_specs,
        out_specs=pl.BlockSpec((1, Q_TILE, D_MODEL), lambda b, i: (b, i, 0)),
        out_shape=jax.ShapeDtypeStruct((B, S, D_MODEL), F32),
        scratch_shapes=[pltpu.VMEM((N_HEADS, nb, AUG), F32),
                        pltpu.VMEM((N_HEADS, AUG, Q_TILE), BF16),
                        pltpu.VMEM((N_HEADS, HEAD_DIM, Q_TILE), F32),
                        pltpu.VMEM((N_HEADS, 2 * MOBA_BLOCK, Q_TILE), F32),
                        pltpu.VMEM((N_HEADS, 2 * MOBA_BLOCK, Q_TILE), BF16),
                        pltpu.VMEM((N_HEADS, 8, Q_TILE), F32),
                        pltpu.VMEM((nchain, win, win), F32),
                        pltpu.VMEM((nchain, win, win), BF16),
                        pltpu.VMEM((nchain, 8, win), F32)],
        compiler_params=pltpu.CompilerParams(dimension_semantics=("parallel", "arbitrary"),
                                             vmem_limit_bytes=VMEM_LIMIT),
        name="attention",
    )(qT, kaug, vT, avg, crow, causal, gates, cqT, ck, cvT, band, band, sink, gates, x, ybd, wo)


def _swa_front(i, blocks, cqT_ref, ck_ref, band0_ref, band_ref, sink_ref, s_ref, p_ref, mx_ref):
    win = 2 * SWA_BLOCK
    rows = 64
    zrows = jnp.zeros((HEAD_DIM, win), BF16)
    chains = [(t, kv) for t in range(blocks) for kv in range(C_KV_HEADS)]
    for t in range(blocks):
        n = i * blocks + t
        prev = jnp.maximum(n - 1, 0)
        kwin = jnp.concatenate([
            ck_ref[0, pl.ds(pl.multiple_of(prev * SWA_BLOCK, SWA_BLOCK), SWA_BLOCK), :],
            ck_ref[0, pl.ds(pl.multiple_of(n * SWA_BLOCK, SWA_BLOCK), SWA_BLOCK), :]], axis=0)
        for kv in range(C_KV_HEADS):
            c = chains.index((t, kv))
            q2 = jnp.concatenate([
                cqT_ref[0, t // 2, (2 * kv + g) * HEAD_DIM:(2 * kv + g + 1) * HEAD_DIM,
                        (t % 2) * SWA_BLOCK:(t % 2 + 1) * SWA_BLOCK] for g in range(2)], axis=1)
            qz = jnp.concatenate([q2, zrows] if kv == 0 else [zrows, q2], axis=0)
            band = band0_ref[kv, 0] if t == 0 else band_ref[kv, 0]
            s = _dot(kwin, qz) + band
            s_ref[c] = s
            mx_ref[c] = _fold8(s, jnp.maximum)
    for c, (t, kv) in enumerate(chains):
        m = jnp.max(mx_ref[c], axis=0, keepdims=True)
        l8 = jnp.zeros((8, win), F32)
        for r in range(win // rows):
            p = jnp.exp2(s_ref[c, r * rows:(r + 1) * rows] - m)
            p_ref[c, r * rows:(r + 1) * rows] = p.astype(BF16)
            l8 = l8 + _fold8(p, jnp.add)
        mx_ref[c, 0:1] = 1.0 / (jnp.sum(l8, axis=0, keepdims=True) + jnp.exp2(sink_ref[kv] - m))


def _swa_back(i, blocks, cvT_ref, p_ref, mx_ref, gate_ref):
    chains = [(t, kv) for t in range(blocks) for kv in range(C_KV_HEADS)]
    outs = {}
    for c, (t, kv) in enumerate(chains):
        n = i * blocks + t
        prev = jnp.maximum(n - 1, 0)
        vwin = jnp.concatenate([cvT_ref[0, prev, kv * HEAD_DIM:(kv + 1) * HEAD_DIM, :],
                                cvT_ref[0, n, kv * HEAD_DIM:(kv + 1) * HEAD_DIM, :]], axis=1)
        outs[t, kv] = _dot(vwin, p_ref[c]) * mx_ref[c, 0:1]
    tiles = []
    for u in range(blocks * SWA_BLOCK // Q_TILE):
        head_rows = []
        for kv in range(C_KV_HEADS):
            for g in range(2):
                head_rows.append(jnp.concatenate(
                    [outs[t, kv][:, g * SWA_BLOCK:(g + 1) * SWA_BLOCK] for t in (2 * u, 2 * u + 1)], axis=1))
        o = jnp.concatenate(head_rows, axis=0).T
        gate = gate_ref[0, u * Q_TILE:(u + 1) * Q_TILE, :].astype(F32)
        tiles.append((o * gate).astype(BF16))
    return tiles


def _tables(S):
    tm = IN_TILE
    slopes_c, slopes_a = SLOPES_C, SLOPES_A
    pos = np.arange(S)

    kca = np.zeros((S, AUG), np.float32)
    blk, r = pos // MOBA_BLOCK, pos % MOBA_BLOCK
    kca[pos, COL_ONEHOT + blk] = 1.0
    kca[:, COL_POS] = r
    kca[:, COL_POS + 1] = r
    kca[:, COL_POS + 2] = blk
    kca[:, COL_POS + 3] = blk
    kca = jnp.asarray(kca, BF16)

    c = slopes_a * np.float32(LOG2E)
    hi = c.astype(BF16).astype(np.float32)
    lo = c - hi
    rows = np.stack([hi, lo, hi * MOBA_BLOCK, lo * MOBA_BLOCK], axis=-1)
    rows = np.pad(rows, ((0, 0), (0, AUG - COL_POS - 4)))
    crow_a = jnp.asarray(np.broadcast_to(rows[:, :, None], rows.shape + (Q_TILE,)), BF16)

    nb = S // MOBA_BLOCK
    avg = jnp.asarray((pos[None, :] // MOBA_BLOCK == np.arange(nb)[:, None]) / MOBA_BLOCK, BF16)
    kq = np.arange(MOBA_BLOCK)
    causal = np.where(kq[:, None] <= kq[None, :], 0.0, NEG_BIG)
    causal = jnp.asarray(np.concatenate([causal, np.zeros_like(causal)], axis=0), F32)
    u = np.arange(2 * SWA_BLOCK)[:, None]
    t = np.arange(SWA_BLOCK)[None, :]
    ok = (u > t) & (u <= t + SWA_BLOCK)
    dist = (SWA_BLOCK + t - u).astype(np.float32)
    sc = np.asarray(slopes_c, np.float32) * np.float32(LOG2E)
    band = []
    for kv in range(C_KV_HEADS):
        bias = np.concatenate([-sc[2 * kv] * dist, -sc[2 * kv + 1] * dist], axis=1)
        ok2 = np.concatenate([ok, ok], axis=1)
        band.append(np.stack([np.where(ok2 & (u >= SWA_BLOCK), bias, NEG_BIG),
                              np.where(ok2, bias, NEG_BIG)]))
    band = jnp.asarray(np.stack(band), F32)
    w = np.repeat(np.asarray(POOL_WINDOWS, np.float32), 64)[None, :]
    first = 1.0 / np.minimum(np.arange(tm, dtype=np.float32)[:, None] + 1.0, w)
    invcnt = jnp.asarray(np.stack([first, np.broadcast_to(1.0 / w, (tm, 256))]), F32)
    return dict(kca=kca, crow_a=crow_a, avg=avg, causal=causal, band=band, invcnt=invcnt)


def kernel(x, norm_g, w_in, w_out, a_q_norm, a_k_norm, pool_w, pool_scale, c_q_norm, c_k_norm, c_sinks, conv_w):
    B, S, _ = x.shape
    depth = norm_g.shape[0]
    assert S % IN_TILE == 0 and S // MOBA_BLOCK <= 32
    tb = _tables(S)
    tm = IN_TILE
    d = np.arange(256)
    bd = jnp.asarray(d[:, None] // HEAD_DIM == d[None, :] // HEAD_DIM, BF16)
    qscale = QK_SCALE * LOG2E
    for l in range(depth):
        gq = jnp.broadcast_to(jnp.tile(a_q_norm[l] * qscale, N_HEADS)[:, None], (256, tm))
        gcq = jnp.broadcast_to(jnp.tile(c_q_norm[l] * qscale, N_HEADS)[:, None], (256, tm))
        gk = jnp.tile(a_k_norm[l], N_HEADS)[None, :]
        gck = jnp.tile(c_k_norm[l], C_KV_HEADS)[None, :]
        pw = jnp.zeros((256, 256), F32)
        for g in range(4):
            pw = pw.at[g * 64:(g + 1) * 64, g * 64:(g + 1) * 64].set(pool_w[l, g])
        qT, kaug, vT, cqT, ck, cvT, gates, ybd = _in_proj(
            x, norm_g[l][None, :], w_in[l].astype(BF16), bd, gq, gk, gcq, gck, tb['kca'],
            tb['invcnt'], pw.astype(BF16), pool_scale[l][None, :], conv_w[l])
        sink = jnp.repeat((c_sinks[l] * LOG2E).reshape(C_KV_HEADS, 2), SWA_BLOCK, axis=1)[:, None, :]
        x = _attention(qT, kaug, vT, tb['avg'], tb['crow_a'], tb['causal'],
                       cqT, ck, cvT, tb['band'], sink, gates,
                       x, ybd, w_out[l].astype(BF16))
    return x
```

```python
import math

import jax
import jax.numpy as jnp
import numpy as np
from jax import lax
from jax.experimental import pallas as pl
from jax.experimental.pallas import tpu as pltpu

F32 = jnp.float32
BF16 = jnp.bfloat16

D_MODEL = 1024
HEAD_DIM = 64
N_HEADS = 4
C_KV_HEADS = 2
NORM_EPS = 1e-6
MOBA_BLOCK = 256
MOBA_TOPK = 3
SWA_BLOCK = 128
POOL_WINDOWS = (2, 4, 8, 16)
CONV_WIDTH = 3
HALO = 16

AUG = 128
COL_ONEHOT = 64
COL_POS = 96
NEG_BIG = -(2.0 ** 100)
LOG2E = math.log2(math.e)
QK_SCALE = HEAD_DIM ** -0.5
SLOPES = np.exp2(-(8.0 / (2 * N_HEADS)) * np.arange(1, 2 * N_HEADS + 1)).astype(np.float32)
SLOPES_C, SLOPES_A = SLOPES[:N_HEADS], SLOPES[N_HEADS:]

IN_TILE = 1024
Q_TILE = 256
VMEM_LIMIT = 56 * 1024 * 1024

SEG = dict(aq=0, ak=256, av=512, ag=768, bu=1024, bg=1280, cq=1536, ck=1792, cv=1920,
           cg=2048, dh=2304, db=2560, dc=2816, dg=3072)
IN_PROJ_WIDTH = 3328


def _dot(a, b):
    return jnp.dot(a, b, preferred_element_type=F32)


def _silu(v):
    h = 0.5 * v
    return h + h * jnp.tanh(h)


def _fold8(x, op):
    out = x[0:8]
    for g in range(1, x.shape[0] // 8):
        out = op(out, x[g * 8:(g + 1) * 8])
    return out


def _in_proj_kernel(x_ref, g_ref, w_ref, bd_ref, gq_ref, gk_ref, gcq_ref, gck_ref, kca_ref,
                    invcnt_ref, pw_ref, pscale_ref, cw_ref,
                    qT_ref, kaug_ref, vT_ref, cqT_ref, ck_ref, cvT_ref, gates_ref, ybd_ref,
                    hist_ref):
    tm = x_ref.shape[1]
    s_idx = pl.program_id(1)
    chunk = 256
    hs, pb_rows = [], []
    for r in range(tm // chunk):
        xr = x_ref[0, r * chunk:(r + 1) * chunk, :]
        ms = jnp.mean(xr * xr, axis=-1, keepdims=True)
        hs.append((xr * lax.rsqrt(ms + NORM_EPS) * g_ref[...]).astype(BF16))
        pb_rows.append(_dot(hs[-1], w_ref[:, SEG['bu']:SEG['bu'] + 512]))
    h = jnp.concatenate(hs, axis=0)

    def proj(lo, width):
        return _dot(h, w_ref[:, lo:lo + width])

    def norm_t(p, gain_ref):
        pt = p.T
        outs = []
        for hh in range(N_HEADS):
            ph = pt[hh * HEAD_DIM:(hh + 1) * HEAD_DIM]
            ss = jnp.sum(ph * ph, axis=0, keepdims=True) * (1.0 / HEAD_DIM)
            outs.append(ph * lax.rsqrt(ss + NORM_EPS))
        return (jnp.concatenate(outs, axis=0) * gain_ref[...]).astype(BF16)

    def norm_rows(p, gain_ref, width):
        ss = _dot((p * p).astype(BF16), bd_ref[0:width, 0:width]) * (1.0 / HEAD_DIM)
        return p * lax.rsqrt(ss + NORM_EPS) * gain_ref[...]

    def build_kaug(kn, kconst):
        lane = lax.broadcasted_iota(jnp.int32, (tm, AUG), 1)
        blocks = []
        for hh in range(N_HEADS):
            col = kn[:, (hh // 2) * AUG:(hh // 2 + 1) * AUG]
            if hh % 2 == 1:
                col = pltpu.roll(col, HEAD_DIM, axis=1)
            blocks.append(jnp.where(lane < HEAD_DIM, col, kconst).astype(BF16))
        return blocks


    @pl.when(s_idx == 0)
    def _():
        hist_ref[...] = jnp.zeros(hist_ref.shape, F32)

    back = lambda v, k: pltpu.roll(v, k, axis=0)

    pb = jnp.concatenate(pb_rows, axis=0)
    bu, bg = pb[:, 0:256], pb[:, 256:512]
    e = jnp.concatenate([hist_ref[0], bu], axis=0)
    hist_ref[0] = bu[tm - HALO:]
    a2 = e + back(e, 1)
    a4 = a2 + back(a2, 2)
    a4r = a4[:, 128:256]
    a8 = a4r + back(a4r, 4)
    a16 = a8 + back(a8, 8)
    lane = lax.broadcasted_iota(jnp.int32, (HALO + tm, 128), 1)
    sums = jnp.concatenate([jnp.where(lane < 64, a2[:, 0:128], a4[:, 0:128]),
                            jnp.where(lane < 64, a8, a16)], axis=1)
    pooled = sums[HALO:] * invcnt_ref[0] - bu
    yb = _dot(pooled.astype(BF16), pw_ref[...]) * pscale_ref[...] * _silu(bg)
    ybd_ref[0, :, 0:256] = yb.astype(BF16)

    pd = proj(SEG['dh'], 1024)
    dh, db, dc, dg = (pd[:, k * 256:(k + 1) * 256] for k in range(4))
    u = dc * dh
    ue = jnp.concatenate([hist_ref[1], u], axis=0)
    hist_ref[1] = u[tm - HALO:]
    conv = (cw_ref[0:1, :] * back(ue, 2) + cw_ref[1:2, :] * back(ue, 1) + cw_ref[2:3, :] * ue)[HALO:]
    ybd_ref[0, :, 256:512] = (db * conv * _silu(dg)).astype(BF16)

    nblk = tm // Q_TILE
    qT = norm_t(proj(SEG['aq'], 256), gq_ref)
    for t in range(nblk):
        qT_ref[0, t] = qT[:, t * Q_TILE:(t + 1) * Q_TILE]
    kn = norm_rows(proj(SEG['ak'], 256), gk_ref, 256)
    kaug = build_kaug(kn, kca_ref[...].astype(F32))
    for hh in range(N_HEADS):
        kaug_ref[0, hh] = kaug[hh]
    vT = proj(SEG['av'], 256).T.astype(BF16)
    for t in range(nblk):
        vT_ref[0, t] = vT[:, t * Q_TILE:(t + 1) * Q_TILE]
    cqT = norm_t(proj(SEG['cq'], 256), gcq_ref)
    for t in range(nblk):
        cqT_ref[0, t] = cqT[:, t * Q_TILE:(t + 1) * Q_TILE]
    ckv = proj(SEG['ck'], 256)
    ck_ref[0] = norm_rows(ckv[:, 0:128], gck_ref, 128).astype(BF16)
    cvT = ckv[:, 128:256].T.astype(BF16)
    for t in range(tm // SWA_BLOCK):
        cvT_ref[0, t] = cvT[:, t * SWA_BLOCK:(t + 1) * SWA_BLOCK]
    gates_ref[0, :, 0:256] = _silu(proj(SEG['ag'], 256)).astype(BF16)
    gates_ref[0, :, 256:512] = _silu(proj(SEG['cg'], 256)).astype(BF16)


def _in_proj(x, g, w, bd, gq, gk, gcq, gck, kca, invcnt, pw, pscale, cw):
    B, S, _ = x.shape
    tm = IN_TILE
    ns = S // tm
    nq = S // Q_TILE
    nkb = S // SWA_BLOCK
    const = lambda *shape: pl.BlockSpec(shape, lambda b, s: (0,) * len(shape))
    out_shape = (
        jax.ShapeDtypeStruct((B, nq, 256, Q_TILE), BF16),
        jax.ShapeDtypeStruct((B, N_HEADS, S, AUG), BF16),
        jax.ShapeDtypeStruct((B, nq, 256, Q_TILE), BF16),
        jax.ShapeDtypeStruct((B, nq, 256, Q_TILE), BF16),
        jax.ShapeDtypeStruct((B, S, 128), BF16),
        jax.ShapeDtypeStruct((B, nkb, 128, SWA_BLOCK), BF16),
        jax.ShapeDtypeStruct((B, S, 512), BF16),
        jax.ShapeDtypeStruct((B, S, 512), BF16),
    )
    r = tm // Q_TILE
    out_specs = (
        pl.BlockSpec((1, r, 256, Q_TILE), lambda b, s: (b, s, 0, 0)),
        pl.BlockSpec((1, N_HEADS, tm, AUG), lambda b, s: (b, 0, s, 0)),
        pl.BlockSpec((1, r, 256, Q_TILE), lambda b, s: (b, s, 0, 0)),
        pl.BlockSpec((1, r, 256, Q_TILE), lambda b, s: (b, s, 0, 0)),
        pl.BlockSpec((1, tm, 128), lambda b, s: (b, s, 0)),
        pl.BlockSpec((1, tm // SWA_BLOCK, 128, SWA_BLOCK), lambda b, s: (b, s, 0, 0)),
        pl.BlockSpec((1, tm, 512), lambda b, s: (b, s, 0)),
        pl.BlockSpec((1, tm, 512), lambda b, s: (b, s, 0)),
    )
    in_specs = [
        pl.BlockSpec((1, tm, D_MODEL), lambda b, s: (b, s, 0)),
        const(1, D_MODEL),
        const(D_MODEL, IN_PROJ_WIDTH),
        const(256, 256),
        const(256, tm), const(1, 256), const(256, tm), const(1, 128),
        pl.BlockSpec((tm, AUG), lambda b, s: (s, 0)),
        pl.BlockSpec((1, tm, 256), lambda b, s: (jnp.minimum(s, 1), 0, 0)),
        const(256, 256), const(1, 256), const(CONV_WIDTH, 256),
    ]
    return pl.pallas_call(
        _in_proj_kernel, grid=(B, ns), in_specs=in_specs, out_specs=out_specs, out_shape=out_shape,
        scratch_shapes=[pltpu.VMEM((2, HALO, 256), F32)],
        compiler_params=pltpu.CompilerParams(dimension_semantics=("parallel", "arbitrary"),
                                             vmem_limit_bytes=VMEM_LIMIT),
        name="in_proj",
    )(x, g, w, bd, gq, gk, gcq, gck, kca, invcnt, pw, pscale, cw)


def _attn_kernel(qT_ref, kaug_ref, vT_ref, avg_ref, crow_ref, causal_ref, gate_ref,
                 cqT_ref, ck_ref, cvT_ref, band0_ref, band_ref, sink_ref, cgate_ref,
                 x_ref, ybd_ref, wo_ref,
                 out_ref,
                 kmean_ref, qaug_ref, acc_ref, s_ref, p_ref, mx_ref, cs_ref, cp_ref, cmx_ref):
    i = pl.program_id(1)
    swa_blocks = Q_TILE // SWA_BLOCK
    S = kaug_ref.shape[2]
    nb = S // MOBA_BLOCK
    chunk = 1024

    @pl.when(i == 0)
    def _():
        for hh in range(N_HEADS):
            acc = jnp.zeros((nb, AUG), F32)
            for c in range(S // chunk):
                acc = acc + _dot(avg_ref[:, c * chunk:(c + 1) * chunk],
                                 kaug_ref[0, hh, c * chunk:(c + 1) * chunk, :])
            kmean_ref[hh] = acc

    row = lax.broadcasted_iota(jnp.int32, (nb, Q_TILE), 0).astype(F32)
    i_f = i.astype(F32)
    neg_inf = jnp.float32(-jnp.inf)
    heads = range(N_HEADS)
    vh = lambda hh, j: vT_ref[0, j, hh * HEAD_DIM:(hh + 1) * HEAD_DIM, :]
    kb = lambda hh, j: kaug_ref[0, hh, pl.ds(pl.multiple_of(j * MOBA_BLOCK, MOBA_BLOCK), MOBA_BLOCK), :]

    for hh in heads:
        qh = qT_ref[0, 0, hh * HEAD_DIM:(hh + 1) * HEAD_DIM, :]
        km = kmean_ref[hh][:, 0:HEAD_DIM]
        km_hi = km.astype(BF16)
        km_lo = (km - km_hi.astype(F32)).astype(BF16)
        bs = _dot(km_hi, qh) + _dot(km_lo, qh)
        xs = jnp.where(row < i_f, bs, neg_inf)
        sel = row == i_f
        for _ in range(MOBA_TOPK):
            mx = jnp.max(xs, axis=0, keepdims=True)
            first = jnp.min(jnp.where(xs == mx, row, jnp.float32(nb)), axis=0, keepdims=True)
            pick = row == first
            sel = sel | (pick & (mx > neg_inf))
            xs = jnp.where(pick, neg_inf, xs)
        negmask = jnp.where(sel, 0.0, NEG_BIG).astype(BF16)
        pad = jnp.zeros((COL_POS - COL_ONEHOT - nb, Q_TILE), BF16) if nb < 32 else None
        parts = [qh, negmask] + ([pad] if pad is not None else []) + [crow_ref[hh]]
        qaug_ref[hh] = jnp.concatenate(parts, axis=0)

    _swa_front(i, swa_blocks, cqT_ref, ck_ref, band0_ref, band_ref, sink_ref, cs_ref, cp_ref, cmx_ref)

    def blk(pos):
        past = jnp.minimum(i + 1, nb - 1)
        return jnp.where(pos <= 0, i, jnp.where(pos <= i, pos - 1, past))

    kblk = MOBA_BLOCK
    sub = 8
    rows = 64

    def scores(hh, j0, j1, mask=None):
        s = _dot(jnp.concatenate([kb(hh, j0), kb(hh, j1)], axis=0), qaug_ref[hh])
        if mask is not None:
            s = s + mask
        s_ref[hh] = s
        mx_ref[hh] = _fold8(s, jnp.maximum)

    def scores_head(hh, t):
        scores(hh, blk(2 * t), blk(2 * t + 1))

    def scores_step(t):
        for hh in heads:
            scores_head(hh, t)

    def softmax_head(hh, m_old, l_old):
        m_new = jnp.maximum(m_old, jnp.max(mx_ref[hh], axis=0, keepdims=True))
        alpha = jnp.exp2(m_old - m_new)
        l8 = alpha * l_old
        for c in range(2 * kblk // rows):
            p = jnp.exp2(s_ref[hh, c * rows:(c + 1) * rows] - m_new)
            p_ref[hh, c * rows:(c + 1) * rows] = p.astype(BF16)
            l8 = l8 + _fold8(p, jnp.add)
        return m_new, l8, alpha

    def softmax_step(carry):
        ms, ls, _ = carry
        new = [softmax_head(hh, ms[hh], ls[hh]) for hh in heads]
        return tuple(zip(*new))

    def values_head(hh, t, alpha):
        acc_ref[hh] = (alpha * acc_ref[hh]
                       + _dot(vh(hh, blk(2 * t)), p_ref[hh, 0:kblk])
                       + _dot(vh(hh, blk(2 * t + 1)), p_ref[hh, kblk:2 * kblk]))

    def values_step(t, alphas):
        for hh in heads:
            values_head(hh, t, alphas[hh])

    def body(t, carry):
        ms, ls, alphas = carry
        new = []
        for hh in heads:
            values_head(hh, t - 1, alphas[hh])
            new.append(softmax_head(hh, ms[hh], ls[hh]))
            scores_head(hh, t + 1)
        return tuple(zip(*new))

    def drain(carry):
        values_step(steps - 2, carry[2])
        return softmax_step(carry)

    steps = jnp.maximum((i + 2) // 2, 2)
    neg = jnp.full((1, Q_TILE), NEG_BIG, F32)
    zero8 = jnp.zeros((sub, Q_TILE), F32)
    acc_ref[...] = jnp.zeros(acc_ref.shape, F32)
    for hh in heads:
        scores(hh, i, blk(1), causal_ref[...])
    carry = softmax_step(((neg,) * N_HEADS, (zero8,) * N_HEADS, None))
    scores_step(1)
    trips = jnp.maximum(steps - 2, 0)
    carry = lax.fori_loop(0, trips // 2, lambda k, c: body(2 * k + 2, body(2 * k + 1, c)), carry)
    carry = lax.cond(trips % 2 == 1, lambda c: body(steps - 2, c), lambda c: c, carry)
    carry = drain(carry)
    ms, ls, alphas = carry
    values_step(steps - 1, alphas)
    yc = _swa_back(i, swa_blocks, cvT_ref, cp_ref, cmx_ref, cgate_ref)[0]
    part = (x_ref[0] + _dot(ybd_ref[0, :, 0:256], wo_ref[256:512, :]) + _dot(yc, wo_ref[512:768, :])
            + _dot(ybd_ref[0, :, 256:512], wo_ref[768:1024, :]))
    outs = [acc_ref[hh] / jnp.sum(ls[hh], axis=0, keepdims=True) for hh in heads]
    o = jnp.concatenate(outs, axis=0).T
    ya = (o * gate_ref[0].astype(F32)).astype(BF16)
    out_ref[0] = part + _dot(ya, wo_ref[0:256, :])


def _attention(qT, kaug, vT, avg, crow, causal, cqT, ck, cvT, band, sink, gates, x, ybd, wo):
    B, nq = qT.shape[0], qT.shape[1]
    S = kaug.shape[2]
    nb = S // MOBA_BLOCK
    nkb = S // SWA_BLOCK
    nchain = (Q_TILE // SWA_BLOCK) * C_KV_HEADS
    win = 2 * SWA_BLOCK
    tile = lambda col: pl.BlockSpec((1, Q_TILE, 256), lambda b, i: (b, i, col))
    in_specs = [
        pl.BlockSpec((1, 1, 256, Q_TILE), lambda b, i: (b, i, 0, 0)),
        pl.BlockSpec((1, N_HEADS, S, AUG), lambda b, i: (b, 0, 0, 0)),
        pl.BlockSpec((1, nq, 256, Q_TILE), lambda b, i: (b, 0, 0, 0)),
        pl.BlockSpec((nb, S), lambda b, i: (0, 0)),
        pl.BlockSpec((N_HEADS, AUG - COL_POS, Q_TILE), lambda b, i: (0, 0, 0)),
        pl.BlockSpec((2 * MOBA_BLOCK, Q_TILE), lambda b, i: (0, 0)),
        tile(0),
        pl.BlockSpec((1, 1, 256, Q_TILE), lambda b, i: (b, i, 0, 0)),
        pl.BlockSpec((1, S, 128), lambda b, i: (b, 0, 0)),
        pl.BlockSpec((1, nkb, 128, SWA_BLOCK), lambda b, i: (b, 0, 0, 0)),
        pl.BlockSpec((C_KV_HEADS, 1, win, win), lambda b, i: (0, jnp.minimum(i, 1), 0, 0)),
        pl.BlockSpec((C_KV_HEADS, 1, win, win), lambda b, i: (0, 1, 0, 0)),
        pl.BlockSpec((C_KV_HEADS, 1, win), lambda b, i: (0, 0, 0)),
        tile(1),
        pl.BlockSpec((1, Q_TILE, D_MODEL), lambda b, i: (b, i, 0)),
        pl.BlockSpec((1, Q_TILE, 512), lambda b, i: (b, i, 0)),
        pl.BlockSpec((D_MODEL, D_MODEL), lambda b, i: (0, 0)),
    ]
    return pl.pallas_call(
        _attn_kernel, grid=(B, nq), in_specs=in_specs,
        out_specs=pl.BlockSpec((1, Q_TILE, D_MODEL), lambda b, i: (b, i, 0)),
        out_shape=jax.ShapeDtypeStruct((B, S, D_MODEL), F32),
        scratch_shapes=[pltpu.VMEM((N_HEADS, nb, AUG), F32),
                        pltpu.VMEM((N_HEADS, AUG, Q_TILE), BF16),
                        pltpu.VMEM((N_HEADS, HEAD_DIM, Q_TILE), F32),
                        pltpu.VMEM((N_HEADS, 2 * MOBA_BLOCK, Q_TILE), F32),
                        pltpu.VMEM((N_HEADS, 2 * MOBA_BLOCK, Q_TILE), BF16),
                        pltpu.VMEM((N_HEADS, 8, Q_TILE), F32),
                        pltpu.VMEM((nchain, win, win), F32),
                        pltpu.VMEM((nchain, win, win), BF16),
                        pltpu.VMEM((nchain, 8, win), F32)],
        compiler_params=pltpu.CompilerParams(dimension_semantics=("parallel", "arbitrary"),
                                             vmem_limit_bytes=VMEM_LIMIT),
        name="attention",
    )(qT, kaug, vT, avg, crow, causal, gates, cqT, ck, cvT, band, band, sink, gates, x, ybd, wo)


def _swa_front(i, blocks, cqT_ref, ck_ref, band0_ref, band_ref, sink_ref, s_ref, p_ref, mx_ref):
    win = 2 * SWA_BLOCK
    rows = 64
    zrows = jnp.zeros((HEAD_DIM, win), BF16)
    chains = [(t, kv) for t in range(blocks) for kv in range(C_KV_HEADS)]
    for t in range(blocks):
        n = i * blocks + t
        prev = jnp.maximum(n - 1, 0)
        kwin = jnp.concatenate([
            ck_ref[0, pl.ds(pl.multiple_of(prev * SWA_BLOCK, SWA_BLOCK), SWA_BLOCK), :],
            ck_ref[0, pl.ds(pl.multiple_of(n * SWA_BLOCK, SWA_BLOCK), SWA_BLOCK), :]], axis=0)
        for kv in range(C_KV_HEADS):
            c = chains.index((t, kv))
            q2 = jnp.concatenate([
                cqT_ref[0, t // 2, (2 * kv + g) * HEAD_DIM:(2 * kv + g + 1) * HEAD_DIM,
                        (t % 2) * SWA_BLOCK:(t % 2 + 1) * SWA_BLOCK] for g in range(2)], axis=1)
            qz = jnp.concatenate([q2, zrows] if kv == 0 else [zrows, q2], axis=0)
            band = band0_ref[kv, 0] if t == 0 else band_ref[kv, 0]
            s = _dot(kwin, qz) + band
            s_ref[c] = s
            mx_ref[c] = _fold8(s, jnp.maximum)
    for c, (t, kv) in enumerate(chains):
        m = jnp.max(mx_ref[c], axis=0, keepdims=True)
        l8 = jnp.zeros((8, win), F32)
        for r in range(win // rows):
            p = jnp.exp2(s_ref[c, r * rows:(r + 1) * rows] - m)
            p_ref[c, r * rows:(r + 1) * rows] = p.astype(BF16)
            l8 = l8 + _fold8(p, jnp.add)
        mx_ref[c, 0:1] = 1.0 / (jnp.sum(l8, axis=0, keepdims=True) + jnp.exp2(sink_ref[kv] - m))


def _swa_back(i, blocks, cvT_ref, p_ref, mx_ref, gate_ref):
    chains = [(t, kv) for t in range(blocks) for kv in range(C_KV_HEADS)]
    outs = {}
    for c, (t, kv) in enumerate(chains):
        n = i * blocks + t
        prev = jnp.maximum(n - 1, 0)
        vwin = jnp.concatenate([cvT_ref[0, prev, kv * HEAD_DIM:(kv + 1) * HEAD_DIM, :],
                                cvT_ref[0, n, kv * HEAD_DIM:(kv + 1) * HEAD_DIM, :]], axis=1)
        outs[t, kv] = _dot(vwin, p_ref[c]) * mx_ref[c, 0:1]
    tiles = []
    for u in range(blocks * SWA_BLOCK // Q_TILE):
        head_rows = []
        for kv in range(C_KV_HEADS):
            for g in range(2):
                head_rows.append(jnp.concatenate(
                    [outs[t, kv][:, g * SWA_BLOCK:(g + 1) * SWA_BLOCK] for t in (2 * u, 2 * u + 1)], axis=1))
        o = jnp.concatenate(head_rows, axis=0).T
        gate = gate_ref[0, u * Q_TILE:(u + 1) * Q_TILE, :].astype(F32)
        tiles.append((o * gate).astype(BF16))
    return tiles


def _tables(S):
    tm = IN_TILE
    slopes_c, slopes_a = SLOPES_C, SLOPES_A
    pos = np.arange(S)

    kca = np.zeros((S, AUG), np.float32)
    blk, r = pos // MOBA_BLOCK, pos % MOBA_BLOCK
    kca[pos, COL_ONEHOT + blk] = 1.0
    kca[:, COL_POS] = r
    kca[:, COL_POS + 1] = r
    kca[:, COL_POS + 2] = blk
    kca[:, COL_POS + 3] = blk
    kca = jnp.asarray(kca, BF16)

    c = slopes_a * np.float32(LOG2E)
    hi = c.astype(BF16).astype(np.float32)
    lo = c - hi
    rows = np.stack([hi, lo, hi * MOBA_BLOCK, lo * MOBA_BLOCK], axis=-1)
    rows = np.pad(rows, ((0, 0), (0, AUG - COL_POS - 4)))
    crow_a = jnp.asarray(np.broadcast_to(rows[:, :, None], rows.shape + (Q_TILE,)), BF16)

    nb = S // MOBA_BLOCK
    avg = jnp.asarray((pos[None, :] // MOBA_BLOCK == np.arange(nb)[:, None]) / MOBA_BLOCK, BF16)
    kq = np.arange(MOBA_BLOCK)
    causal = np.where(kq[:, None] <= kq[None, :], 0.0, NEG_BIG)
    causal = jnp.asarray(np.concatenate([causal, np.zeros_like(causal)], axis=0), F32)
    u = np.arange(2 * SWA_BLOCK)[:, None]
    t = np.arange(SWA_BLOCK)[None, :]
    ok = (u > t) & (u <= t + SWA_BLOCK)
    dist = (SWA_BLOCK + t - u).astype(np.float32)
    sc = np.asarray(slopes_c, np.float32) * np.float32(LOG2E)
    band = []
    for kv in range(C_KV_HEADS):
        bias = np.concatenate([-sc[2 * kv] * dist, -sc[2 * kv + 1] * dist], axis=1)
        ok2 = np.concatenate([ok, ok], axis=1)
        band.append(np.stack([np.where(ok2 & (u >= SWA_BLOCK), bias, NEG_BIG),
                              np.where(ok2, bias, NEG_BIG)]))
    band = jnp.asarray(np.stack(band), F32)
    w = np.repeat(np.asarray(POOL_WINDOWS, np.float32), 64)[None, :]
    first = 1.0 / np.minimum(np.arange(tm, dtype=np.float32)[:, None] + 1.0, w)
    invcnt = jnp.asarray(np.stack([first, np.broadcast_to(1.0 / w, (tm, 256))]), F32)
    return dict(kca=kca, crow_a=crow_a, avg=avg, causal=causal, band=band, invcnt=invcnt)


def kernel(x, norm_g, w_in, w_out, a_q_norm, a_k_norm, pool_w, pool_scale, c_q_norm, c_k_norm, c_sinks, conv_w):
    B, S, _ = x.shape
    depth = norm_g.shape[0]
    assert S % IN_TILE == 0 and S // MOBA_BLOCK <= 32
    tb = _tables(S)
    tm = IN_TILE
    d = np.arange(256)
    bd = jnp.asarray(d[:, None] // HEAD_DIM == d[None, :] // HEAD_DIM, BF16)
    qscale = QK_SCALE * LOG2E
    for l in range(depth):
        gq = jnp.broadcast_to(jnp.tile(a_q_norm[l] * qscale, N_HEADS)[:, None], (256, tm))
        gcq = jnp.broadcast_to(jnp.tile(c_q_norm[l] * qscale, N_HEADS)[:, None], (256, tm))
        gk = jnp.tile(a_k_norm[l], N_HEADS)[None, :]
        gck = jnp.tile(c_k_norm[l], C_KV_HEADS)[None, :]
        pw = jnp.zeros((256, 256), F32)
        for g in range(4):
            pw = pw.at[g * 64:(g + 1) * 64, g * 64:(g + 1) * 64].set(pool_w[l, g])
        qT, kaug, vT, cqT, ck, cvT, gates, ybd = _in_proj(
            x, norm_g[l][None, :], w_in[l].astype(BF16), bd, gq, gk, gcq, gck, tb['kca'],
            tb['invcnt'], pw.astype(BF16), pool_scale[l][None, :], conv_w[l])
        sink = jnp.repeat((c_sinks[l] * LOG2E).reshape(C_KV_HEADS, 2), SWA_BLOCK, axis=1)[:, None, :]
        x = _attention(qT, kaug, vT, tb['avg'], tb['crow_a'], tb['causal'],
                       cqT, ck, cvT, tb['band'], sink, gates,
                       x, ybd, w_out[l].astype(BF16))
    return x
```

```python
import math

import jax
import jax.numpy as jnp
import numpy as np
from jax import lax
from jax.experimental import pallas as pl
from jax.experimental.pallas import tpu as pltpu

F32 = jnp.float32
BF16 = jnp.bfloat16

D_MODEL = 1024
HEAD_DIM = 64
N_HEADS = 4
C_KV_HEADS = 2
NORM_EPS = 1e-6
MOBA_BLOCK = 256
MOBA_TOPK = 3
SWA_BLOCK = 128
POOL_WINDOWS = (2, 4, 8, 16)
CONV_WIDTH = 3
HALO = 16

AUG = 128
COL_ONEHOT = 64
COL_POS = 96
NEG_BIG = -(2.0 ** 100)
LOG2E = math.log2(math.e)
QK_SCALE = HEAD_DIM ** -0.5
SLOPES = np.exp2(-(8.0 / (2 * N_HEADS)) * np.arange(1, 2 * N_HEADS + 1)).astype(np.float32)
SLOPES_C, SLOPES_A = SLOPES[:N_HEADS], SLOPES[N_HEADS:]

LANES = 128
IN_TILE = 1024
Q_TILE = 256
VMEM_LIMIT = 56 * 1024 * 1024

SEG = dict(aq=0, ak=256, av=512, ag=768, bu=1024, bg=1280, cq=1536, ck=1792, cv=1920,
           cg=2048, dh=2304, db=2560, dc=2816, dg=3072)
IN_PROJ_WIDTH = 3328


def _dot(a, b):
    return jnp.dot(a, b, preferred_element_type=F32)


def _silu(v):
    h = 0.5 * v
    return h + h * jnp.tanh(h)


def _fold8(x, op):
    out = x[0:8]
    for g in range(1, x.shape[0] // 8):
        out = op(out, x[g * 8:(g + 1) * 8])
    return out


def _in_proj_kernel(x_ref, g_ref, w_ref, bd_ref, gq_ref, gk_ref, gcq_ref, gck_ref, kca_ref,
                    invcnt_ref, pw_ref, pscale_ref, cw_ref,
                    qT_ref, kaug_ref, vT_ref, cqT_ref, ck_ref, cvT_ref, gates_ref, ybd_ref,
                    hist_ref):
    tm = x_ref.shape[1]
    s_idx = pl.program_id(1)
    chunk = 256
    hs, pb_rows = [], []
    for r in range(tm // chunk):
        xr = x_ref[0, r * chunk:(r + 1) * chunk, :]
        ms = jnp.mean(xr * xr, axis=-1, keepdims=True)
        hs.append((xr * lax.rsqrt(ms + NORM_EPS) * g_ref[...]).astype(BF16))
        pb_rows.append(_dot(hs[-1], w_ref[:, SEG['bu']:SEG['bu'] + 512]))
    h = jnp.concatenate(hs, axis=0)

    def proj(lo, width):
        return _dot(h, w_ref[:, lo:lo + width])

    def norm_t(p, gain_ref):
        pt = p.T
        outs = []
        for hh in range(N_HEADS):
            ph = pt[hh * HEAD_DIM:(hh + 1) * HEAD_DIM]
            ss = jnp.sum(ph * ph, axis=0, keepdims=True) * (1.0 / HEAD_DIM)
            outs.append(ph * lax.rsqrt(ss + NORM_EPS))
        gain = jnp.tile(gain_ref[...], (1, p.shape[0] // LANES))
        return (jnp.concatenate(outs, axis=0) * gain).astype(BF16)

    def norm_rows(p, gain_ref, width):
        ss = _dot((p * p).astype(BF16), bd_ref[0:width, 0:width]) * (1.0 / HEAD_DIM)
        return p * lax.rsqrt(ss + NORM_EPS) * gain_ref[...]

    def build_kaug(kn, kconst):
        lane = lax.broadcasted_iota(jnp.int32, (tm, AUG), 1)
        blocks = []
        for hh in range(N_HEADS):
            col = kn[:, (hh // 2) * AUG:(hh // 2 + 1) * AUG]
            if hh % 2 == 1:
                col = pltpu.roll(col, HEAD_DIM, axis=1)
            blocks.append(jnp.where(lane < HEAD_DIM, col, kconst).astype(BF16))
        return blocks


    @pl.when(s_idx == 0)
    def _():
        hist_ref[...] = jnp.zeros(hist_ref.shape, F32)

    back = lambda v, k: pltpu.roll(v, k, axis=0)

    pb = jnp.concatenate(pb_rows, axis=0)
    bu, bg = pb[:, 0:256], pb[:, 256:512]
    e = jnp.concatenate([hist_ref[0], bu], axis=0)
    hist_ref[0] = bu[tm - HALO:]
    a2 = e + back(e, 1)
    a4 = a2 + back(a2, 2)
    a4r = a4[:, 128:256]
    a8 = a4r + back(a4r, 4)
    a16 = a8 + back(a8, 8)
    lane = lax.broadcasted_iota(jnp.int32, (HALO + tm, 128), 1)
    sums = jnp.concatenate([jnp.where(lane < 64, a2[:, 0:128], a4[:, 0:128]),
                            jnp.where(lane < 64, a8, a16)], axis=1)
    pooled = sums[HALO:] * invcnt_ref[0] - bu
    yb = _dot(pooled.astype(BF16), pw_ref[...]) * pscale_ref[...] * _silu(bg)
    ybd_ref[0, :, 0:256] = yb.astype(BF16)

    pd = proj(SEG['dh'], 1024)
    dh, db, dc, dg = (pd[:, k * 256:(k + 1) * 256] for k in range(4))
    u = dc * dh
    ue = jnp.concatenate([hist_ref[1], u], axis=0)
    hist_ref[1] = u[tm - HALO:]
    conv = (cw_ref[0:1, :] * back(ue, 2) + cw_ref[1:2, :] * back(ue, 1) + cw_ref[2:3, :] * ue)[HALO:]
    ybd_ref[0, :, 256:512] = (db * conv * _silu(dg)).astype(BF16)

    nblk = tm // Q_TILE
    qT = norm_t(proj(SEG['aq'], 256), gq_ref)
    for t in range(nblk):
        qT_ref[0, t] = qT[:, t * Q_TILE:(t + 1) * Q_TILE]
    kn = norm_rows(proj(SEG['ak'], 256), gk_ref, 256)
    kaug = build_kaug(kn, kca_ref[...].astype(F32))
    for hh in range(N_HEADS):
        kaug_ref[0, hh] = kaug[hh]
    vT = proj(SEG['av'], 256).T.astype(BF16)
    for t in range(nblk):
        vT_ref[0, t] = vT[:, t * Q_TILE:(t + 1) * Q_TILE]
    cqT = norm_t(proj(SEG['cq'], 256), gcq_ref)
    for t in range(nblk):
        cqT_ref[0, t] = cqT[:, t * Q_TILE:(t + 1) * Q_TILE]
    ckv = proj(SEG['ck'], 256)
    ck_ref[0] = norm_rows(ckv[:, 0:128], gck_ref, 128).astype(BF16)
    cvT = ckv[:, 128:256].T.astype(BF16)
    for t in range(tm // SWA_BLOCK):
        cvT_ref[0, t] = cvT[:, t * SWA_BLOCK:(t + 1) * SWA_BLOCK]
    gates_ref[0, :, 0:256] = _silu(proj(SEG['ag'], 256)).astype(BF16)
    gates_ref[0, :, 256:512] = _silu(proj(SEG['cg'], 256)).astype(BF16)


def _in_proj(x, g, w, bd, gq, gk, gcq, gck, kca, invcnt, pw, pscale, cw):
    B, S, _ = x.shape
    tm = IN_TILE
    ns = S // tm
    nq = S // Q_TILE
    nkb = S // SWA_BLOCK
    const = lambda *shape: pl.BlockSpec(shape, lambda b, s: (0,) * len(shape))
    out_shape = (
        jax.ShapeDtypeStruct((B, nq, 256, Q_TILE), BF16),
        jax.ShapeDtypeStruct((B, N_HEADS, S, AUG), BF16),
        jax.ShapeDtypeStruct((B, nq, 256, Q_TILE), BF16),
        jax.ShapeDtypeStruct((B, nq, 256, Q_TILE), BF16),
        jax.ShapeDtypeStruct((B, S, 128), BF16),
        jax.ShapeDtypeStruct((B, nkb, 128, SWA_BLOCK), BF16),
        jax.ShapeDtypeStruct((B, S, 512), BF16),
        jax.ShapeDtypeStruct((B, S, 512), BF16),
    )
    r = tm // Q_TILE
    out_specs = (
        pl.BlockSpec((1, r, 256, Q_TILE), lambda b, s: (b, s, 0, 0)),
        pl.BlockSpec((1, N_HEADS, tm, AUG), lambda b, s: (b, 0, s, 0)),
        pl.BlockSpec((1, r, 256, Q_TILE), lambda b, s: (b, s, 0, 0)),
        pl.BlockSpec((1, r, 256, Q_TILE), lambda b, s: (b, s, 0, 0)),
        pl.BlockSpec((1, tm, 128), lambda b, s: (b, s, 0)),
        pl.BlockSpec((1, tm // SWA_BLOCK, 128, SWA_BLOCK), lambda b, s: (b, s, 0, 0)),
        pl.BlockSpec((1, tm, 512), lambda b, s: (b, s, 0)),
        pl.BlockSpec((1, tm, 512), lambda b, s: (b, s, 0)),
    )
    in_specs = [
        pl.BlockSpec((1, tm, D_MODEL), lambda b, s: (b, s, 0)),
        const(1, D_MODEL),
        const(D_MODEL, IN_PROJ_WIDTH),
        const(256, 256),
        const(256, LANES), const(1, 256), const(256, LANES), const(1, 128),
        pl.BlockSpec((tm, AUG), lambda b, s: (s, 0)),
        pl.BlockSpec((1, tm, 256), lambda b, s: (jnp.minimum(s, 1), 0, 0)),
        const(256, 256), const(1, 256), const(CONV_WIDTH, 256),
    ]
    return pl.pallas_call(
        _in_proj_kernel, grid=(B, ns), in_specs=in_specs, out_specs=out_specs, out_shape=out_shape,
        scratch_shapes=[pltpu.VMEM((2, HALO, 256), F32)],
        compiler_params=pltpu.CompilerParams(dimension_semantics=("parallel", "arbitrary"),
                                             vmem_limit_bytes=VMEM_LIMIT),
        name="in_proj",
    )(x, g, w, bd, gq, gk, gcq, gck, kca, invcnt, pw, pscale, cw)


def _attn_kernel(qT_ref, kaug_ref, vT_ref, avg_ref, crow_ref, causal_ref, gate_ref,
                 cqT_ref, ck_ref, cvT_ref, band0_ref, band_ref, sink_ref, cgate_ref,
                 x_ref, ybd_ref, wo_ref,
                 out_ref,
                 kmean_ref, qaug_ref, acc_ref, s_ref, p_ref, mx_ref, cs_ref, cp_ref, cmx_ref):
    i = pl.program_id(1)
    swa_blocks = Q_TILE // SWA_BLOCK
    S = kaug_ref.shape[2]
    nb = S // MOBA_BLOCK
    chunk = 1024

    @pl.when(i == 0)
    def _():
        for hh in range(N_HEADS):
            acc = jnp.zeros((nb, AUG), F32)
            for c in range(S // chunk):
                acc = acc + _dot(avg_ref[:, c * chunk:(c + 1) * chunk],
                                 kaug_ref[0, hh, c * chunk:(c + 1) * chunk, :])
            kmean_ref[hh] = acc

    row = lax.broadcasted_iota(jnp.int32, (nb, Q_TILE), 0).astype(F32)
    i_f = i.astype(F32)
    neg_inf = jnp.float32(-jnp.inf)
    heads = range(N_HEADS)
    vh = lambda hh, j: vT_ref[0, j, hh * HEAD_DIM:(hh + 1) * HEAD_DIM, :]
    kb = lambda hh, j: kaug_ref[0, hh, pl.ds(pl.multiple_of(j * MOBA_BLOCK, MOBA_BLOCK), MOBA_BLOCK), :]

    for hh in heads:
        qh = qT_ref[0, 0, hh * HEAD_DIM:(hh + 1) * HEAD_DIM, :]
        km = kmean_ref[hh][:, 0:HEAD_DIM]
        km_hi = km.astype(BF16)
        km_lo = (km - km_hi.astype(F32)).astype(BF16)
        bs = _dot(km_hi, qh) + _dot(km_lo, qh)
        xs = jnp.where(row < i_f, bs, neg_inf)
        sel = row == i_f
        for _ in range(MOBA_TOPK):
            mx = jnp.max(xs, axis=0, keepdims=True)
            first = jnp.min(jnp.where(xs == mx, row, jnp.float32(nb)), axis=0, keepdims=True)
            pick = row == first
            sel = sel | (pick & (mx > neg_inf))
            xs = jnp.where(pick, neg_inf, xs)
        negmask = jnp.where(sel, 0.0, NEG_BIG).astype(BF16)
        pad = jnp.zeros((COL_POS - COL_ONEHOT - nb, Q_TILE), BF16) if nb < 32 else None
        parts = [qh, negmask] + ([pad] if pad is not None else []) + [crow_ref[hh]]
        qaug_ref[hh] = jnp.concatenate(parts, axis=0)

    _swa_front(i, swa_blocks, cqT_ref, ck_ref, band0_ref, band_ref, sink_ref, cs_ref, cp_ref, cmx_ref)

    def blk(pos):
        past = jnp.minimum(i + 1, nb - 1)
        return jnp.where(pos <= 0, i, jnp.where(pos <= i, pos - 1, past))

    kblk = MOBA_BLOCK
    sub = 8
    rows = 64

    def scores(hh, j0, j1, mask=None):
        s = _dot(jnp.concatenate([kb(hh, j0), kb(hh, j1)], axis=0), qaug_ref[hh])
        if mask is not None:
            s = s + mask
        s_ref[hh] = s
        mx_ref[hh] = _fold8(s, jnp.maximum)

    def scores_head(hh, t):
        scores(hh, blk(2 * t), blk(2 * t + 1))

    def scores_step(t):
        for hh in heads:
            scores_head(hh, t)

    def softmax_head(hh, m_old, l_old):
        m_new = jnp.maximum(m_old, jnp.max(mx_ref[hh], axis=0, keepdims=True))
        alpha = jnp.exp2(m_old - m_new)
        l8 = alpha * l_old
        for c in range(2 * kblk // rows):
            p = jnp.exp2(s_ref[hh, c * rows:(c + 1) * rows] - m_new)
            p_ref[hh, c * rows:(c + 1) * rows] = p.astype(BF16)
            l8 = l8 + _fold8(p, jnp.add)
        return m_new, l8, alpha

    def softmax_step(carry):
        ms, ls, _ = carry
        new = [softmax_head(hh, ms[hh], ls[hh]) for hh in heads]
        return tuple(zip(*new))

    def values_head(hh, t, alpha):
        acc_ref[hh] = (alpha * acc_ref[hh]
                       + _dot(vh(hh, blk(2 * t)), p_ref[hh, 0:kblk])
                       + _dot(vh(hh, blk(2 * t + 1)), p_ref[hh, kblk:2 * kblk]))

    def values_step(t, alphas):
        for hh in heads:
            values_head(hh, t, alphas[hh])

    def body(t, carry):
        ms, ls, alphas = carry
        new = []
        for hh in heads:
            values_head(hh, t - 1, alphas[hh])
            new.append(softmax_head(hh, ms[hh], ls[hh]))
            scores_head(hh, t + 1)
        return tuple(zip(*new))

    def drain(carry):
        values_step(steps - 2, carry[2])
        return softmax_step(carry)

    steps = jnp.maximum((i + 2) // 2, 2)
    neg = jnp.full((1, Q_TILE), NEG_BIG, F32)
    zero8 = jnp.zeros((sub, Q_TILE), F32)
    acc_ref[...] = jnp.zeros(acc_ref.shape, F32)
    for hh in heads:
        scores(hh, i, blk(1), causal_ref[...])
    carry = softmax_step(((neg,) * N_HEADS, (zero8,) * N_HEADS, None))
    scores_step(1)
    trips = jnp.maximum(steps - 2, 0)
    carry = lax.fori_loop(0, trips // 2, lambda k, c: body(2 * k + 2, body(2 * k + 1, c)), carry)
    carry = lax.cond(trips % 2 == 1, lambda c: body(steps - 2, c), lambda c: c, carry)
    carry = drain(carry)
    ms, ls, alphas = carry
    values_step(steps - 1, alphas)
    yc = _swa_back(i, swa_blocks, cvT_ref, cp_ref, cmx_ref, cgate_ref)[0]
    part = (x_ref[0] + _dot(ybd_ref[0, :, 0:256], wo_ref[256:512, :]) + _dot(yc, wo_ref[512:768, :])
            + _dot(ybd_ref[0, :, 256:512], wo_ref[768:1024, :]))
    outs = [acc_ref[hh] / jnp.sum(ls[hh], axis=0, keepdims=True) for hh in heads]
    o = jnp.concatenate(outs, axis=0).T
    ya = (o * gate_ref[0].astype(F32)).astype(BF16)
    out_ref[0] = part + _dot(ya, wo_ref[0:256, :])


def _attention(qT, kaug, vT, avg, crow, causal, cqT, ck, cvT, band, sink, gates, x, ybd, wo):
    B, nq = qT.shape[0], qT.shape[1]
    S = kaug.shape[2]
    nb = S // MOBA_BLOCK
    nkb = S // SWA_BLOCK
    nchain = (Q_TILE // SWA_BLOCK) * C_KV_HEADS
    win = 2 * SWA_BLOCK
    tile = lambda col: pl.BlockSpec((1, Q_TILE, 256), lambda b, i: (b, i, col))
    in_specs = [
        pl.BlockSpec((1, 1, 256, Q_TILE), lambda b, i: (b, i, 0, 0)),
        pl.BlockSpec((1, N_HEADS, S, AUG), lambda b, i: (b, 0, 0, 0)),
        pl.BlockSpec((1, nq, 256, Q_TILE), lambda b, i: (b, 0, 0, 0)),
        pl.BlockSpec((nb, S), lambda b, i: (0, 0)),
        pl.BlockSpec((N_HEADS, AUG - COL_POS, Q_TILE), lambda b, i: (0, 0, 0)),
        pl.BlockSpec((2 * MOBA_BLOCK, Q_TILE), lambda b, i: (0, 0)),
        tile(0),
        pl.BlockSpec((1, 1, 256, Q_TILE), lambda b, i: (b, i, 0, 0)),
        pl.BlockSpec((1, S, 128), lambda b, i: (b, 0, 0)),
        pl.BlockSpec((1, nkb, 128, SWA_BLOCK), lambda b, i: (b, 0, 0, 0)),
        pl.BlockSpec((C_KV_HEADS, 1, win, win), lambda b, i: (0, jnp.minimum(i, 1), 0, 0)),
        pl.BlockSpec((C_KV_HEADS, 1, win, win), lambda b, i: (0, 1, 0, 0)),
        pl.BlockSpec((C_KV_HEADS, 1, win), lambda b, i: (0, 0, 0)),
        tile(1),
        pl.BlockSpec((1, Q_TILE, D_MODEL), lambda b, i: (b, i, 0)),
        pl.BlockSpec((1, Q_TILE, 512), lambda b, i: (b, i, 0)),
        pl.BlockSpec((D_MODEL, D_MODEL), lambda b, i: (0, 0)),
    ]
    return pl.pallas_call(
        _attn_kernel, grid=(B, nq), in_specs=in_specs,
        out_specs=pl.BlockSpec((1, Q_TILE, D_MODEL), lambda b, i: (b, i, 0)),
        out_shape=jax.ShapeDtypeStruct((B, S, D_MODEL), F32),
        scratch_shapes=[pltpu.VMEM((N_HEADS, nb, AUG), F32),
                        pltpu.VMEM((N_HEADS, AUG, Q_TILE), BF16),
                        pltpu.VMEM((N_HEADS, HEAD_DIM, Q_TILE), F32),
                        pltpu.VMEM((N_HEADS, 2 * MOBA_BLOCK, Q_TILE), F32),
                        pltpu.VMEM((N_HEADS, 2 * MOBA_BLOCK, Q_TILE), BF16),
                        pltpu.VMEM((N_HEADS, 8, Q_TILE), F32),
                        pltpu.VMEM((nchain, win, win), F32),
                        pltpu.VMEM((nchain, win, win), BF16),
                        pltpu.VMEM((nchain, 8, win), F32)],
        compiler_params=pltpu.CompilerParams(dimension_semantics=("parallel", "arbitrary"),
                                             vmem_limit_bytes=VMEM_LIMIT),
        name="attention",
    )(qT, kaug, vT, avg, crow, causal, gates, cqT, ck, cvT, band, band, sink, gates, x, ybd, wo)


def _swa_front(i, blocks, cqT_ref, ck_ref, band0_ref, band_ref, sink_ref, s_ref, p_ref, mx_ref):
    win = 2 * SWA_BLOCK
    rows = 64
    zrows = jnp.zeros((HEAD_DIM, win), BF16)
    chains = [(t, kv) for t in range(blocks) for kv in range(C_KV_HEADS)]
    for t in range(blocks):
        n = i * blocks + t
        prev = jnp.maximum(n - 1, 0)
        kwin = jnp.concatenate([
            ck_ref[0, pl.ds(pl.multiple_of(prev * SWA_BLOCK, SWA_BLOCK), SWA_BLOCK), :],
            ck_ref[0, pl.ds(pl.multiple_of(n * SWA_BLOCK, SWA_BLOCK), SWA_BLOCK), :]], axis=0)
        for kv in range(C_KV_HEADS):
            c = chains.index((t, kv))
            q2 = jnp.concatenate([
                cqT_ref[0, t // 2, (2 * kv + g) * HEAD_DIM:(2 * kv + g + 1) * HEAD_DIM,
                        (t % 2) * SWA_BLOCK:(t % 2 + 1) * SWA_BLOCK] for g in range(2)], axis=1)
            qz = jnp.concatenate([q2, zrows] if kv == 0 else [zrows, q2], axis=0)
            band = band0_ref[kv, 0] if t == 0 else band_ref[kv, 0]
            s = _dot(kwin, qz) + band
            s_ref[c] = s
            mx_ref[c] = _fold8(s, jnp.maximum)
    for c, (t, kv) in enumerate(chains):
        m = jnp.max(mx_ref[c], axis=0, keepdims=True)
        l8 = jnp.zeros((8, win), F32)
        for r in range(win // rows):
            p = jnp.exp2(s_ref[c, r * rows:(r + 1) * rows] - m)
            p_ref[c, r * rows:(r + 1) * rows] = p.astype(BF16)
            l8 = l8 + _fold8(p, jnp.add)
        mx_ref[c, 0:1] = 1.0 / (jnp.sum(l8, axis=0, keepdims=True) + jnp.exp2(sink_ref[kv] - m))


def _swa_back(i, blocks, cvT_ref, p_ref, mx_ref, gate_ref):
    chains = [(t, kv) for t in range(blocks) for kv in range(C_KV_HEADS)]
    outs = {}
    for c, (t, kv) in enumerate(chains):
        n = i * blocks + t
        prev = jnp.maximum(n - 1, 0)
        vwin = jnp.concatenate([cvT_ref[0, prev, kv * HEAD_DIM:(kv + 1) * HEAD_DIM, :],
                                cvT_ref[0, n, kv * HEAD_DIM:(kv + 1) * HEAD_DIM, :]], axis=1)
        outs[t, kv] = _dot(vwin, p_ref[c]) * mx_ref[c, 0:1]
    tiles = []
    for u in range(blocks * SWA_BLOCK // Q_TILE):
        head_rows = []
        for kv in range(C_KV_HEADS):
            for g in range(2):
                head_rows.append(jnp.concatenate(
                    [outs[t, kv][:, g * SWA_BLOCK:(g + 1) * SWA_BLOCK] for t in (2 * u, 2 * u + 1)], axis=1))
        o = jnp.concatenate(head_rows, axis=0).T
        gate = gate_ref[0, u * Q_TILE:(u + 1) * Q_TILE, :].astype(F32)
        tiles.append((o * gate).astype(BF16))
    return tiles


def _tables(S):
    tm = IN_TILE
    slopes_c, slopes_a = SLOPES_C, SLOPES_A
    pos = np.arange(S)

    kca = np.zeros((S, AUG), np.float32)
    blk, r = pos // MOBA_BLOCK, pos % MOBA_BLOCK
    kca[pos, COL_ONEHOT + blk] = 1.0
    kca[:, COL_POS] = r
    kca[:, COL_POS + 1] = r
    kca[:, COL_POS + 2] = blk
    kca[:, COL_POS + 3] = blk
    kca = jnp.asarray(kca, BF16)

    c = slopes_a * np.float32(LOG2E)
    hi = c.astype(BF16).astype(np.float32)
    lo = c - hi
    rows = np.stack([hi, lo, hi * MOBA_BLOCK, lo * MOBA_BLOCK], axis=-1)
    rows = np.pad(rows, ((0, 0), (0, AUG - COL_POS - 4)))
    crow_a = jnp.asarray(np.broadcast_to(rows[:, :, None], rows.shape + (Q_TILE,)), BF16)

    nb = S // MOBA_BLOCK
    avg = jnp.asarray((pos[None, :] // MOBA_BLOCK == np.arange(nb)[:, None]) / MOBA_BLOCK, BF16)
    kq = np.arange(MOBA_BLOCK)
    causal = np.where(kq[:, None] <= kq[None, :], 0.0, NEG_BIG)
    causal = jnp.asarray(np.concatenate([causal, np.zeros_like(causal)], axis=0), F32)
    u = np.arange(2 * SWA_BLOCK)[:, None]
    t = np.arange(SWA_BLOCK)[None, :]
    ok = (u > t) & (u <= t + SWA_BLOCK)
    dist = (SWA_BLOCK + t - u).astype(np.float32)
    sc = np.asarray(slopes_c, np.float32) * np.float32(LOG2E)
    band = []
    for kv in range(C_KV_HEADS):
        bias = np.concatenate([-sc[2 * kv] * dist, -sc[2 * kv + 1] * dist], axis=1)
        ok2 = np.concatenate([ok, ok], axis=1)
        band.append(np.stack([np.where(ok2 & (u >= SWA_BLOCK), bias, NEG_BIG),
                              np.where(ok2, bias, NEG_BIG)]))
    band = jnp.asarray(np.stack(band), F32)
    w = np.repeat(np.asarray(POOL_WINDOWS, np.float32), 64)[None, :]
    first = 1.0 / np.minimum(np.arange(tm, dtype=np.float32)[:, None] + 1.0, w)
    invcnt = jnp.asarray(np.stack([first, np.broadcast_to(1.0 / w, (tm, 256))]), F32)
    return dict(kca=kca, crow_a=crow_a, avg=avg, causal=causal, band=band, invcnt=invcnt)


def kernel(x, norm_g, w_in, w_out, a_q_norm, a_k_norm, pool_w, pool_scale, c_q_norm, c_k_norm, c_sinks, conv_w):
    B, S, _ = x.shape
    depth = norm_g.shape[0]
    assert S % IN_TILE == 0 and S // MOBA_BLOCK <= 32
    tb = _tables(S)
    tm = IN_TILE
    d = np.arange(256)
    bd = jnp.asarray(d[:, None] // HEAD_DIM == d[None, :] // HEAD_DIM, BF16)
    qscale = QK_SCALE * LOG2E
    for l in range(depth):
        gq = jnp.broadcast_to(jnp.tile(a_q_norm[l] * qscale, N_HEADS)[:, None], (256, LANES))
        gcq = jnp.broadcast_to(jnp.tile(c_q_norm[l] * qscale, N_HEADS)[:, None], (256, LANES))
        gk = jnp.tile(a_k_norm[l], N_HEADS)[None, :]
        gck = jnp.tile(c_k_norm[l], C_KV_HEADS)[None, :]
        pw = jnp.zeros((256, 256), F32)
        for g in range(4):
            pw = pw.at[g * 64:(g + 1) * 64, g * 64:(g + 1) * 64].set(pool_w[l, g])
        qT, kaug, vT, cqT, ck, cvT, gates, ybd = _in_proj(
            x, norm_g[l][None, :], w_in[l].astype(BF16), bd, gq, gk, gcq, gck, tb['kca'],
            tb['invcnt'], pw.astype(BF16), pool_scale[l][None, :], conv_w[l])
        sink = jnp.repeat((c_sinks[l] * LOG2E).reshape(C_KV_HEADS, 2), SWA_BLOCK, axis=1)[:, None, :]
        x = _attention(qT, kaug, vT, tb['avg'], tb['crow_a'], tb['causal'],
                       cqT, ck, cvT, tb['band'], sink, gates,
                       x, ybd, w_out[l].astype(BF16))
    return x
```
